```python
import math
import jax, jax.numpy as jnp
from jax import lax
import numpy as np

D_MODEL = 1024
BATCH = 16
SEQ = 2048
DEPTH = 2
DEC_BATCH = 128
DEC_SEQ = 8
PAST_LEN = 8192
PAGE_SIZE = 128

N_EVEN = (DEPTH + 1) // 2
N_ODD = DEPTH // 2

D_A = D_MODEL // 2
POOL_WINDOWS = (2, 4, 8, 16)
N_POOL_GROUPS = len(POOL_WINDOWS)
G_A = D_A // N_POOL_GROUPS
POOL_STATE = max(POOL_WINDOWS) - 1

H_B = 8
DH_B = 64
D_B = H_B * DH_B
H_IDX = 4
D_IDX = 64
ROPE_IDX = 32
TOPK_MAX = 256
N_BUCKETS = 32
MAX_DISTANCE = 128

H_C = 16
Q_LORA = 384
KV_LORA = 256
NOPE_C = 64
ROPE_C = 32
DV_C = 64
D_C = H_C * DV_C
MLA_SCALE = (NOPE_C + ROPE_C) ** -0.5

ROPE_THETA = 10000.0
Q_BLOCK = 128
LN_EPS = 1e-5
RMS_EPS = 1e-6
ALPHA = (2 * DEPTH) ** 0.25
BETA = (8 * DEPTH) ** -0.25

EVEN_SIZES = (D_A, D_A, D_B, D_B, D_B, D_B, H_IDX * D_IDX, D_IDX, H_IDX)
D_IN_EVEN = sum(EVEN_SIZES)
ODD_SIZES = (Q_LORA, KV_LORA, ROPE_C, D_C)
D_IN_ODD = sum(ODD_SIZES)

kernel_name = 'hybrid_pool_dsa_mla_decoder_step'

F32 = jnp.float32


def split_cols(h, sizes):
    out, st = [], 0
    for s in sizes:
        out.append(h[..., st:st + s])
        st += s
    return out


def layer_norm(x, g, b):
    xf = x.astype(F32)
    mu = jnp.mean(xf, -1, keepdims=True)
    var = jnp.mean(jnp.square(xf - mu), -1, keepdims=True)
    return ((xf - mu) * lax.rsqrt(var + LN_EPS) * g + b).astype(x.dtype)


def rms_norm(x, g):
    xf = x.astype(F32)
    return (xf * lax.rsqrt(jnp.mean(xf * xf, -1, keepdims=True) + RMS_EPS) * g).astype(x.dtype)


def rope(x, pos):
    half = x.shape[-1] // 2
    inv = ROPE_THETA ** (-jnp.arange(half, dtype=F32) / half)
    ang = pos.astype(F32)[:, None] * inv
    ang = ang.reshape(ang.shape[0], *([1] * (x.ndim - 3)), half)
    cos, sin = jnp.cos(ang), jnp.sin(ang)
    xf = x.astype(F32)
    x1, x2 = xf[..., :half], xf[..., half:]
    return jnp.concatenate([x1 * cos - x2 * sin, x2 * cos + x1 * sin], -1).astype(x.dtype)


def partial_rope(x, pos):
    return jnp.concatenate([rope(x[..., :ROPE_IDX], pos), x[..., ROPE_IDX:]], -1)


def t5_bucket(dist):
    max_exact = N_BUCKETS // 2
    d = jnp.maximum(dist, 0)
    large = max_exact + (jnp.log(jnp.maximum(d, 1).astype(F32) / max_exact)
                         / math.log(MAX_DISTANCE / max_exact) * (N_BUCKETS - max_exact)).astype(jnp.int32)
    large = jnp.minimum(large, N_BUCKETS - 1)
    return jnp.where(d < max_exact, d, large)


def pool_mix(u_ext, n_new, w_grp, scale):
    Bn, E, _ = u_ext.shape
    P = E - n_new
    cs = jnp.cumsum(jnp.pad(u_ext.astype(F32), ((0, 0), (1, 0), (0, 0))), axis=1)
    e = jnp.arange(P, E)
    hi = cs[:, P + 1:]
    outs = []
    for g, w in enumerate(POOL_WINDOWS):
        sl = slice(g * G_A, (g + 1) * G_A)
        lo = cs[:, jnp.maximum(e + 1 - w, 0), sl]
        cnt = jnp.minimum(e + 1, w).astype(F32)[None, :, None]
        outs.append((hi[..., sl] - lo) / cnt - u_ext[:, P:, sl].astype(F32))
    pooled = jnp.stack(outs, axis=2)
    mixed = jnp.einsum('bngc,gcd->bngd', pooled, w_grp).reshape(Bn, n_new, D_A)
    return (mixed * scale).astype(u_ext.dtype)


def index_select(q_idx, w_idx, k_idx, q_pos, k_pos, top_k):
    s = jnp.einsum('qhd,kd->qhk', q_idx, k_idx, preferred_element_type=F32) * D_IDX ** -0.5
    score = jnp.einsum('qh,qhk->qk', w_idx.astype(F32), jax.nn.relu(s))
    score = jnp.where(k_pos[None, :] <= q_pos[:, None], score, -jnp.inf)
    _, sel = lax.top_k(score, top_k)
    return sel


def sparse_attend(q, k_sel, v_sel, q_pos, sel_pos, rel_bias):
    logits = jnp.einsum('qhd,qkhd->qhk', q, k_sel, preferred_element_type=F32) * DH_B ** -0.5
    dist = q_pos[:, None] - sel_pos
    bias = rel_bias[t5_bucket(dist)].astype(F32)
    logits = logits + jnp.transpose(bias, (0, 2, 1))
    logits = jnp.where((dist >= 0)[:, None, :], logits, -jnp.inf)
    p = jax.nn.softmax(logits, axis=-1).astype(v_sel.dtype)
    return jnp.einsum('qhk,qkhd->qhd', p, v_sel)


def prompt_sparse_attention(q, k, v, q_idx, w_idx, k_idx, rel_bias):
    Bn, S = q.shape[:2]
    nblk = S // Q_BLOCK
    top_k = min(TOPK_MAX, S // 4)
    kpos = jnp.arange(S)

    def block(args):
        b, st, qb, qib, wb = args
        qpos = st + jnp.arange(Q_BLOCK)
        sel = index_select(qib, wb, k_idx[b], qpos, kpos, top_k)
        return sparse_attend(qb, k[b][sel], v[b][sel], qpos, sel, rel_bias)

    bs = jnp.repeat(jnp.arange(Bn), nblk)
    sts = jnp.tile(jnp.arange(nblk) * Q_BLOCK, Bn)
    fl = lambda t: t.reshape(Bn * nblk, Q_BLOCK, *t.shape[2:])
    out = lax.map(block, (bs, sts, fl(q), fl(q_idx), fl(w_idx)))
    return out.reshape(Bn, S, H_B, DH_B)


def sample_sparse_attention(q, k_new, v_new, q_idx, w_idx, kidx_new, page_table,
                            k_pool, v_pool, kidx_pool, j, rel_bias):
    T = q.shape[1]
    past = page_table.shape[1] * PAGE_SIZE
    top_k = min(TOPK_MAX, (past + T) // 4)
    qpos = past + jnp.arange(T)
    kpos = jnp.arange(past + T)

    def seq(args):
        pt, qb, qib, wb, kn, vn, kin = args
        kidx_all = jnp.concatenate([kidx_pool[j, pt].reshape(past, D_IDX), kin], 0)
        sel = index_select(qib, wb, kidx_all, qpos, kpos, top_k)
        in_past = (sel < past)[..., None, None]
        page = pt[jnp.minimum(sel, past - 1) // PAGE_SIZE]
        row = sel % PAGE_SIZE
        new_i = jnp.clip(sel - past, 0, T - 1)
        ks = jnp.where(in_past, k_pool[j, page, row], kn[new_i])
        vs = jnp.where(in_past, v_pool[j, page, row], vn[new_i])
        return sparse_attend(qb, ks, vs, qpos, sel, rel_bias)

    return lax.map(seq, (page_table, q, q_idx, w_idx, k_new, v_new, kidx_new))


def even_project(x, w_in, pos):
    Bn, L, _ = x.shape
    u, g_a, q, k, v, g_b, qi, ki, wi = split_cols(x @ w_in, EVEN_SIZES)
    hd = lambda t: t.reshape(Bn, L, H_B, DH_B)
    qi = partial_rope(qi.reshape(Bn, L, H_IDX, D_IDX), pos)
    ki = partial_rope(ki, pos)
    return u, g_a, hd(q), hd(k), hd(v), g_b, qi, ki, wi * H_IDX ** -0.5


def even_out(a, o_b, g_a, g_b, w_out):
    Bn, L = a.shape[:2]
    h = jnp.concatenate([a * jax.nn.silu(g_a), o_b.reshape(Bn, L, D_B) * jax.nn.silu(g_b)], -1)
    return h @ w_out


def mla_project(x, w_in, q_norm, w_q_b, kv_norm, pos):
    cq, ckv, kpe, g = split_cols(x @ w_in, ODD_SIZES)
    q = jnp.einsum('blc,chd->blhd', rms_norm(cq, q_norm), w_q_b)
    q_nope, q_pe = q[..., :NOPE_C], rope(q[..., NOPE_C:], pos)
    return q_nope, q_pe, rms_norm(ckv, kv_norm), rope(kpe, pos), g


def mla_prompt_attention(q_nope, q_pe, ckv, kpe, w_uk, w_uv):
    Bn, S = q_nope.shape[:2]
    k_nope = jnp.einsum('bkc,chn->bkhn', ckv, w_uk)
    v = jnp.einsum('bkc,chv->bkhv', ckv, w_uv)
    kpos = jnp.arange(S)

    def block(st):
        qn = lax.dynamic_slice_in_dim(q_nope, st, Q_BLOCK, 1)
        qp = lax.dynamic_slice_in_dim(q_pe, st, Q_BLOCK, 1)
        qpos = st + jnp.arange(Q_BLOCK)
        s = (jnp.einsum('bqhn,bkhn->bhqk', qn, k_nope, preferred_element_type=F32)
             + jnp.einsum('bqhr,bkr->bhqk', qp, kpe, preferred_element_type=F32)) * MLA_SCALE
        s = jnp.where(kpos[None, :] <= qpos[:, None], s, -jnp.inf)
        p = jax.nn.softmax(s, axis=-1).astype(v.dtype)
        return jnp.einsum('bhqk,bkhv->bqhv', p, v)

    out = lax.map(block, jnp.arange(S // Q_BLOCK) * Q_BLOCK)
    return out.transpose(1, 0, 2, 3, 4).reshape(Bn, S, H_C, DV_C)


def mla_sample_attention(q_nope, q_pe, ckv_new, kpe_new, page_table, ckv_pool, kpe_pool, j, w_uk, w_uv):
    T = q_nope.shape[1]
    past = page_table.shape[1] * PAGE_SIZE
    q_lat = jnp.einsum('bqhn,chn->bqhc', q_nope, w_uk)
    qpos = past + jnp.arange(T)
    kpos = jnp.arange(past + T)

    def seq(args):
        pt, ql, qp, cn, kn = args
        c_all = jnp.concatenate([ckv_pool[j, pt].reshape(past, KV_LORA), cn], 0)
        k_all = jnp.concatenate([kpe_pool[j, pt].reshape(past, ROPE_C), kn], 0)
        s = (jnp.einsum('qhc,kc->hqk', ql, c_all, preferred_element_type=F32)
             + jnp.einsum('qhr,kr->hqk', qp, k_all, preferred_element_type=F32)) * MLA_SCALE
        s = jnp.where(kpos[None, :] <= qpos[:, None], s, -jnp.inf)
        p = jax.nn.softmax(s, axis=-1).astype(c_all.dtype)
        return jnp.einsum('hqk,kc->qhc', p, c_all)

    o_lat = lax.map(seq, (page_table, q_lat, q_pe, ckv_new, kpe_new))
    return jnp.einsum('bqhc,chv->bqhv', o_lat, w_uv)


def mla_out(o, g, w_out):
    Bn, L = o.shape[:2]
    return (o.reshape(Bn, L, D_C) * jax.nn.silu(g)) @ w_out


def setup_inputs(seed: int = 0) -> dict:
    key = jax.random.key(seed)
    ks = jax.random.split(key, 24)
    n_pages = PAST_LEN // PAGE_SIZE
    n_phys = (5 * DEC_BATCH * n_pages) // 4

    def nrm(k, shape, scale=1.0):
        return scale * jax.random.normal(k, shape, F32)

    page_table = jax.random.permutation(ks[8], n_phys)[:DEC_BATCH * n_pages]
    page_table = page_table.reshape(DEC_BATCH, n_pages).astype(jnp.int32)
    return {
        'x_prompt': nrm(ks[0], (BATCH, SEQ, D_MODEL)),
        'x_sample': nrm(ks[1], (DEC_BATCH, DEC_SEQ, D_MODEL)),
        'state_pool': nrm(ks[2], (N_EVEN, DEC_BATCH, POOL_STATE, D_A)),
        'cache_k_b': nrm(ks[3], (N_EVEN, n_phys, PAGE_SIZE, H_B, DH_B)),
        'cache_v_b': nrm(ks[4], (N_EVEN, n_phys, PAGE_SIZE, H_B, DH_B)),
        'cache_kidx_b': nrm(ks[5], (N_EVEN, n_phys, PAGE_SIZE, D_IDX)),
        'cache_ckv': nrm(ks[6], (N_ODD, n_phys, PAGE_SIZE, KV_LORA)),
        'cache_kpe': nrm(ks[7], (N_ODD, n_phys, PAGE_SIZE, ROPE_C)),
        'page_table': page_table,
        'w_in_even': nrm(ks[9], (N_EVEN, D_MODEL, D_IN_EVEN), D_MODEL ** -0.5),
        'pool_w': nrm(ks[10], (N_EVEN, N_POOL_GROUPS, G_A, G_A), G_A ** -0.5),
        'pool_scale': 1.0 + nrm(ks[11], (N_EVEN, D_A), 0.1),
        'w_out_even': nrm(ks[12], (N_EVEN, D_A + D_B, D_MODEL), BETA * (D_A + D_B) ** -0.5),
        'rel_bias': nrm(ks[13], (N_BUCKETS, H_B), 0.5),
        'w_in_odd': nrm(ks[14], (N_ODD, D_MODEL, D_IN_ODD), D_MODEL ** -0.5),
        'q_norm': 1.0 + nrm(ks[15], (N_ODD, Q_LORA), 0.1),
        'w_q_b': nrm(ks[16], (N_ODD, Q_LORA, H_C, NOPE_C + ROPE_C), Q_LORA ** -0.5),
        'kv_norm': 1.0 + nrm(ks[17], (N_ODD, KV_LORA), 0.1),
        'w_uk': nrm(ks[18], (N_ODD, KV_LORA, H_C, NOPE_C), KV_LORA ** -0.5),
        'w_uv': nrm(ks[19], (N_ODD, KV_LORA, H_C, DV_C), KV_LORA ** -0.5),
        'w_out_odd': nrm(ks[20], (N_ODD, D_C, D_MODEL), BETA * D_C ** -0.5),
        'ln_g': 1.0 + nrm(ks[21], (DEPTH, D_MODEL), 0.1),
        'ln_b': nrm(ks[22], (DEPTH, D_MODEL), 0.02),
    }


def reference(x_prompt, x_sample, state_pool, cache_k_b, cache_v_b, cache_kidx_b, cache_ckv, cache_kpe,
              page_table, w_in_even, pool_w, pool_scale, w_out_even, rel_bias, w_in_odd, q_norm, w_q_b,
              kv_norm, w_uk, w_uv, w_out_odd, ln_g, ln_b):
    S = x_prompt.shape[1]
    T = x_sample.shape[1]
    past = page_table.shape[1] * PAGE_SIZE
    pos_p = jnp.arange(S)
    pos_s = past + jnp.arange(T)
    xp, xs = x_prompt, x_sample
    pool_p, pool_s, kb_p, kb_s, vb_p, vb_s, ki_p, ki_s = [], [], [], [], [], [], [], []
    ckv_p, ckv_s, kpe_p, kpe_s = [], [], [], []
    for i in range(DEPTH):
        j = i // 2
        if i % 2 == 0:
            u, ga, q, k, v, gb, qi, ki, wi = even_project(xp, w_in_even[j], pos_p)
            a = pool_mix(u, S, pool_w[j], pool_scale[j])
            o = prompt_sparse_attention(q, k, v, qi, wi, ki, rel_bias)
            hp = even_out(a, o, ga, gb, w_out_even[j])
            pool_p.append(u[:, S - POOL_STATE:])
            kb_p.append(k)
            vb_p.append(v)
            ki_p.append(ki)
            u, ga, q, k, v, gb, qi, ki, wi = even_project(xs, w_in_even[j], pos_s)
            u_ext = jnp.concatenate([state_pool[j].astype(u.dtype), u], 1)
            a = pool_mix(u_ext, T, pool_w[j], pool_scale[j])
            o = sample_sparse_attention(q, k, v, qi, wi, ki, page_table, cache_k_b, cache_v_b,
                                        cache_kidx_b, j, rel_bias)
            hs = even_out(a, o, ga, gb, w_out_even[j])
            pool_s.append(u_ext[:, u_ext.shape[1] - POOL_STATE:])
            kb_s.append(k)
            vb_s.append(v)
            ki_s.append(ki)
        else:
            qn, qp, ckv, kpe, g = mla_project(xp, w_in_odd[j], q_norm[j], w_q_b[j], kv_norm[j], pos_p)
            hp = mla_out(mla_prompt_attention(qn, qp, ckv, kpe, w_uk[j], w_uv[j]), g, w_out_odd[j])
            ckv_p.append(ckv)
            kpe_p.append(kpe)
            qn, qp, ckv, kpe, g = mla_project(xs, w_in_odd[j], q_norm[j], w_q_b[j], kv_norm[j], pos_s)
            o = mla_sample_attention(qn, qp, ckv, kpe, page_table, cache_ckv, cache_kpe, j, w_uk[j], w_uv[j])
            hs = mla_out(o, g, w_out_odd[j])
            ckv_s.append(ckv)
            kpe_s.append(kpe)
        xp = layer_norm(ALPHA * xp + hp, ln_g[i], ln_b[i])
        xs = layer_norm(ALPHA * xs + hs, ln_g[i], ln_b[i])
    return (xp, xs,
            jnp.stack(pool_p), jnp.stack(pool_s),
            jnp.stack(kb_p), jnp.stack(kb_s),
            jnp.stack(vb_p), jnp.stack(vb_s),
            jnp.stack(ki_p), jnp.stack(ki_s),
            jnp.stack(ckv_p), jnp.stack(ckv_s),
            jnp.stack(kpe_p), jnp.stack(kpe_s))
```

```python
import functools
import math

import numpy as np
import jax
import jax.numpy as jnp
from jax import lax
from jax.experimental import pallas as pl
from jax.experimental.pallas import tpu as pltpu

F32 = jnp.float32
BF16 = jnp.bfloat16
I32 = jnp.int32

D_MODEL = 1024
DEPTH = 2
PAGE_SIZE = 128
D_A = D_MODEL // 2
POOL_WINDOWS = (2, 4, 8, 16)
G_A = D_A // len(POOL_WINDOWS)
POOL_STATE = max(POOL_WINDOWS) - 1
H_B = 8
DH_B = 64
D_B = H_B * DH_B
H_IDX = 4
D_IDX = 64
TOPK_MAX = 256
N_BUCKETS = 32
MAX_DISTANCE = 128
H_C = 16
Q_LORA = 384
KV_LORA = 256
NOPE_C = 64
ROPE_C = 32
DV_C = 64
D_C = H_C * DV_C
MLA_SCALE = (NOPE_C + ROPE_C) ** -0.5
ROPE_THETA = 10000.0
ROPE_HALF = 16
LN_EPS = 1e-5
RMS_EPS = 1e-6
ALPHA = (2 * DEPTH) ** 0.25

LANES = 128
IDX_W = 512
IDX_KI = 256
IDX_WI = 384
KVPE_W = 384
QCAT_W = H_C * LANES

VMEM_LIMIT = 48 * 1024 * 1024

TM_PROJ = 256
TS_POOL = 512
G_POOL = 32
TM_OUT = 256
TQ = 128
TK_MLA = 256
RG_MLA = 256
G_SEL = 32
G_DSA = 8
G_MLA = 16

NEG_INF = float("-inf")
INT_MIN = -(2 ** 31)


def _cparams(n_axes):
    return pltpu.CompilerParams(dimension_semantics=("arbitrary",) * n_axes,
                                vmem_limit_bytes=VMEM_LIMIT)


def _dot(a, b):
    return jnp.dot(a, b, preferred_element_type=F32)


def _dot_nt(a, b):
    return lax.dot_general(a, b, (((1,), (1,)), ((), ())), preferred_element_type=F32)


def _t5_breaks():
    max_d = 2 * MAX_DISTANCE
    d = np.arange(max_d + 1)
    me = N_BUCKETS // 2
    large = me + (np.log(np.maximum(d, 1) / me) / math.log(MAX_DISTANCE / me) * (N_BUCKETS - me)).astype(np.int64)
    large = np.minimum(large, N_BUCKETS - 1)
    bucket = np.where(d < me, d, large)
    assert np.all(bucket[MAX_DISTANCE:] == N_BUCKETS - 1)
    return int(bucket[0]), [(int(i), int(bucket[i])) for i in range(1, max_d + 1) if bucket[i] != bucket[i - 1]]


_T5_FIRST, _T5_BREAKS = _t5_breaks()


def _bias_of_dist(dist, relb_ref, h):
    val = jnp.full(dist.shape, relb_ref[_T5_FIRST, h], F32)
    for p, bk in _T5_BREAKS:
        val = jnp.where(dist >= p, relb_ref[bk, h], val)
    return val


def _sortable_key(score):
    score = jnp.where(score == 0.0, 0.0, score)
    bits = lax.bitcast_convert_type(score, I32)
    return jnp.where(bits < 0, bits ^ jnp.int32(0x7FFFFFFF), bits)


def _kth_largest_key(keys_ref, rows, k):
    def body(it, t):
        cand = t + jnp.left_shift(jnp.int32(1), jnp.int32(31) - it)
        cnt = jnp.sum(jnp.where(keys_ref[...] >= cand, 1.0, 0.0), axis=1, keepdims=True)
        return jnp.where(cnt >= k, cand, t)
    return lax.fori_loop(0, 32, body, jnp.full((rows, 1), INT_MIN, I32))


def _strict_upper():
    r = lax.broadcasted_iota(I32, (LANES, LANES), 0)
    c = lax.broadcasted_iota(I32, (LANES, LANES), 1)
    return jnp.where(r < c, 1.0, 0.0).astype(BF16)


def _silu_gate(v, g):
    return v * (g * (1.0 / (1.0 + jnp.exp(-g))))


def _even_proj_body(x_ref, wm_ref, ws_ref, wr_ref, cos_ref, sin_ref,
                    u_ref, ga_ref, q_ref, k_ref, v_ref, gb_ref, idx_ref, qb_ref, kb_ref, vb_ref, kib_ref):
    xb = x_ref[...].astype(BF16)

    def mm(n):
        return _dot(xb, wm_ref[:, n * D_A:(n + 1) * D_A])

    u_ref[...] = mm(0)
    ga_ref[...] = mm(1)
    q = mm(2)
    q_ref[...] = q
    qb_ref[...] = (q * DH_B ** -0.5).astype(BF16)
    k = mm(3)
    k_ref[...] = k
    kb_ref[...] = k.astype(BF16)
    v = mm(4)
    v_ref[...] = v
    vb_ref[...] = v.astype(BF16)
    gb_ref[...] = mm(5)
    idx = _dot(xb, ws_ref[...]) * cos_ref[...] + _dot(xb, wr_ref[...]) * sin_ref[...]
    idx_ref[...] = idx
    kib_ref[...] = idx[:, IDX_KI:IDX_KI + LANES].astype(BF16)


def _even_proj(x2d, wm, ws, wr, cos_t, sin_t):
    m = x2d.shape[0]
    tm = min(TM_PROJ, m)
    nt = cos_t.shape[0] // tm
    row = lambda w: pl.BlockSpec((tm, w), lambda i: (i, 0))
    full = lambda a: pl.BlockSpec(a.shape, lambda i: (0, 0))
    tab = pl.BlockSpec((tm, IDX_W), lambda i: (i % nt, 0))
    f = lambda w, dt: jax.ShapeDtypeStruct((m, w), dt)
    return pl.pallas_call(
        _even_proj_body,
        grid=(m // tm,),
        in_specs=[row(D_MODEL), full(wm), full(ws), full(wr), tab, tab],
        out_specs=[row(D_A)] * 6 + [row(IDX_W)] + [row(D_B)] * 3 + [row(LANES)],
        out_shape=[f(D_A, F32)] * 6 + [f(IDX_W, F32)] + [f(D_B, BF16)] * 3 + [f(LANES, BF16)],
        compiler_params=_cparams(1),
        name="even_proj",
    )(x2d, wm, ws, wr, cos_t, sin_t)


def _pool_mix_group(win_sum, cur, inv_cnt, pw_ref, scale_ref, g):
    pooled = win_sum * inv_cnt - cur
    mixed = _dot(pooled.astype(BF16), pw_ref[g])
    return mixed * scale_ref[:, g * G_A:(g + 1) * G_A]


def _pool_prompt_body(u_ref, halo_ref, pw_ref, scale_ref, a_ref, ext_ref):
    s = pl.program_id(1)
    ts = u_ref.shape[1]
    halo = halo_ref[0]
    ext_ref[0:16, :] = jnp.where(s == 0, 0.0, halo)
    ext_ref[16:, :] = u_ref[0]
    pos = s * ts + lax.broadcasted_iota(I32, (ts, 1), 0)
    for g, w in enumerate(POOL_WINDOWS):
        sl = pl.ds(g * G_A, G_A)
        acc = ext_ref[pl.ds(16, ts), sl]
        for kk in range(1, w):
            acc = acc + ext_ref[pl.ds(16 - kk, ts), sl]
        cnt = jnp.minimum(pos + 1, w).astype(F32)
        a_ref[0, :, g * G_A:(g + 1) * G_A] = _pool_mix_group(
            acc, ext_ref[pl.ds(16, ts), sl], 1.0 / cnt, pw_ref, scale_ref, g)


def _pool_prompt(u3, pw, scale):
    b, s, _ = u3.shape
    ts = min(TS_POOL, s)
    hb = ts // 16
    return pl.pallas_call(
        _pool_prompt_body,
        grid=(b, s // ts),
        in_specs=[pl.BlockSpec((1, ts, D_A), lambda i, j: (i, j, 0)),
                  pl.BlockSpec((1, 16, D_A), lambda i, j: (i, jnp.maximum(j * hb - 1, 0), 0)),
                  pl.BlockSpec(pw.shape, lambda i, j: (0, 0, 0)),
                  pl.BlockSpec(scale.shape, lambda i, j: (0, 0))],
        out_specs=pl.BlockSpec((1, ts, D_A), lambda i, j: (i, j, 0)),
        out_shape=jax.ShapeDtypeStruct((b, s, D_A), F32),
        scratch_shapes=[pltpu.VMEM((ts + 16, D_A), F32)],
        compiler_params=_cparams(2),
        name="pool_prompt",
    )(u3, u3, pw, scale)


def _pool_sample_body(ext_ref, pw_ref, scale_ref, a_ref):
    gs = ext_ref.shape[0]
    t = ext_ref.shape[1] - 16
    for g, w in enumerate(POOL_WINDOWS):
        sl = pl.ds(g * G_A, G_A)
        cur = ext_ref[:, pl.ds(16, t), sl]
        acc = cur
        for kk in range(1, w):
            acc = acc + ext_ref[:, pl.ds(16 - kk, t), sl]
        acc = acc.reshape(gs * t, G_A)
        cur = cur.reshape(gs * t, G_A)
        a_ref[:, g * G_A:(g + 1) * G_A] = _pool_mix_group(acc, cur, 1.0 / w, pw_ref, scale_ref, g)


def _pool_sample(ext, pw, scale):
    db, e, _ = ext.shape
    t = e - 16
    gs = min(G_POOL, db)
    return pl.pallas_call(
        _pool_sample_body,
        grid=(db // gs,),
        in_specs=[pl.BlockSpec((gs, e, D_A), lambda i: (i, 0, 0)),
                  pl.BlockSpec(pw.shape, lambda i: (0, 0, 0)),
                  pl.BlockSpec(scale.shape, lambda i: (0, 0))],
        out_specs=pl.BlockSpec((gs * t, D_A), lambda i: (i, 0)),
        out_shape=jax.ShapeDtypeStruct((db * t, D_A), F32),
        compiler_params=_cparams(1),
        name="pool_sample",
    )(ext, pw, scale)


def _gate_out_ln_body(nparts, x_ref, *refs):
    parts = refs[:2 * nparts]
    w_ref, g_ref, b_ref, y_ref = refs[2 * nparts:]
    hs = [_silu_gate(parts[2 * p][...], parts[2 * p + 1][...]).astype(BF16) for p in range(nparts)]
    h = hs[0] if nparts == 1 else jnp.concatenate(hs, axis=1)
    z = ALPHA * x_ref[...] + _dot(h, w_ref[...])
    mu = jnp.mean(z, axis=1, keepdims=True)
    zc = z - mu
    var = jnp.mean(zc * zc, axis=1, keepdims=True)
    y_ref[...] = zc * lax.rsqrt(var + LN_EPS) * g_ref[...] + b_ref[...]


def _gate_out_ln(x2d, parts, w, ln_g, ln_b, name):
    m = x2d.shape[0]
    tm = min(TM_OUT, m)
    row = lambda a: pl.BlockSpec((tm, a.shape[1]), lambda i: (i, 0))
    full = lambda a: pl.BlockSpec(a.shape, lambda i: (0, 0))
    flat = [a for pair in parts for a in pair]
    return pl.pallas_call(
        functools.partial(_gate_out_ln_body, len(parts)),
        grid=(m // tm,),
        in_specs=[row(x2d)] + [row(a) for a in flat] + [full(w), full(ln_g), full(ln_b)],
        out_specs=row(x2d),
        out_shape=jax.ShapeDtypeStruct(x2d.shape, F32),
        compiler_params=_cparams(1),
        name=name,
    )(x2d, *flat, w, ln_g, ln_b)


def _select_mask(keys_ref, valid_fn, topk, rows, n, out_fn):
    t = _kth_largest_key(keys_ref, rows, topk)
    cnt_gt = jnp.sum(jnp.where(keys_ref[...] > t, 1.0, 0.0), axis=1, keepdims=True)
    need = topk - cnt_gt
    su = _strict_upper()
    off = jnp.zeros((rows, 1), F32)
    for c in range(n // LANES):
        kc = keys_ref[:, c * LANES:(c + 1) * LANES]
        eq = jnp.where(kc == t, 1.0, 0.0)
        before = _dot(eq.astype(BF16), su) + off
        sel = (kc > t) | ((kc == t) & (before < need))
        out_fn(c, sel & valid_fn(c))
        off = off + jnp.sum(eq, axis=1, keepdims=True)


def _dsa_prompt_body(topk, relb_ref, qb_ref, idxq_ref, kb_ref, vb_ref, kib_ref, o_ref,
                     keys_scr, madd_scr, bias_scr):
    b = pl.program_id(0)
    i = pl.program_id(1)
    tq = qb_ref.shape[1]
    s = kb_ref.shape[1]
    nc = s // LANES

    @pl.when((b == 0) & (i == 0))
    def _():
        r = lax.broadcasted_iota(I32, (LANES, LANES), 0)
        c = lax.broadcasted_iota(I32, (LANES, LANES), 1)
        for h in range(H_B):
            bias_scr[h, 0] = _bias_of_dist(r - c, relb_ref, h)
            bias_scr[h, 1] = _bias_of_dist(LANES + r - c, relb_ref, h)
            bias_scr[h, 2] = jnp.full((LANES, LANES), relb_ref[N_BUCKETS - 1, h], F32)

    lane = lax.broadcasted_iota(I32, (tq, LANES), 1)
    lo = lane < DH_B
    idxq = idxq_ref[0]
    kib = kib_ref[0]

    score = None
    for j in range(H_IDX // 2):
        pair = idxq[:, j * LANES:(j + 1) * LANES]
        for half in range(2):
            h = 2 * j + half
            lhs = jnp.where(lo if half == 0 else ~lo, pair, 0.0).astype(BF16)
            sh = _dot_nt(lhs, kib) * D_IDX ** -0.5
            term = idxq[:, IDX_WI + h:IDX_WI + h + 1] * jnp.maximum(sh, 0.0)
            score = term if score is None else score + term
    qpos = i * tq + lax.broadcasted_iota(I32, (tq, s), 0)
    kpos = lax.broadcasted_iota(I32, (tq, s), 1)
    keys_scr[...] = _sortable_key(jnp.where(kpos <= qpos, score, NEG_INF))

    def valid_fn(c):
        qp = i * tq + lax.broadcasted_iota(I32, (tq, LANES), 0)
        kp = c * LANES + lax.broadcasted_iota(I32, (tq, LANES), 1)
        return kp <= qp

    def out_fn(c, sel):
        madd_scr[:, c * LANES:(c + 1) * LANES] = jnp.where(sel, 0.0, NEG_INF)

    _select_mask(keys_scr, valid_fn, topk, tq, s, out_fn)

    for j in range(H_B // 2):
        qpair = qb_ref[0, :, j * LANES:(j + 1) * LANES]
        kpair = kb_ref[0, :, j * LANES:(j + 1) * LANES]
        vpair = vb_ref[0, :, j * LANES:(j + 1) * LANES]
        opair = None
        for half in range(2):
            h = 2 * j + half
            hm = lo if half == 0 else ~lo
            lhs = jnp.where(hm, qpair, jnp.zeros_like(qpair))
            logit = _dot_nt(lhs, kpair)
            bias = jnp.concatenate([bias_scr[h, jnp.clip(i - c, 0, 2)] for c in range(nc)], axis=1)
            logit = logit + bias + madd_scr[...]
            mx = jnp.max(logit, axis=1, keepdims=True)
            e = jnp.exp(logit - mx)
            p = e / jnp.sum(e, axis=1, keepdims=True)
            ov = _dot(p.astype(BF16), vpair)
            opair = ov if opair is None else jnp.where(hm, ov, opair)
        o_ref[0, :, j * LANES:(j + 1) * LANES] = opair


def _dsa_prompt(rel_bias, qb3, idx3, kb3, vb3, kib3):
    b, s, _ = qb3.shape
    assert s % TQ == 0
    topk = min(TOPK_MAX, s // 4)
    qspec = lambda w: pl.BlockSpec((1, TQ, w), lambda i, j: (i, j, 0))
    kspec = lambda w: pl.BlockSpec((1, s, w), lambda i, j: (i, 0, 0))
    return pl.pallas_call(
        functools.partial(_dsa_prompt_body, topk),
        grid=(b, s // TQ),
        in_specs=[pl.BlockSpec(memory_space=pltpu.SMEM),
                  qspec(D_B), qspec(IDX_W), kspec(D_B), kspec(D_B), kspec(LANES)],
        out_specs=qspec(D_B),
        out_shape=jax.ShapeDtypeStruct((b, s, D_B), F32),
        scratch_shapes=[pltpu.VMEM((TQ, s), I32), pltpu.VMEM((TQ, s), F32),
                        pltpu.VMEM((H_B, 3, LANES, LANES), F32)],
        compiler_params=_cparams(2),
        name="dsa_prompt",
    )(rel_bias, qb3, idx3, kb3, vb3, kib3)


def _dsa_select_body(topk, g_pages, pt_ref, idx_ref, *refs):
    pages = refs[:g_pages]
    sel_ref, kcat_scr, score_scr, keys_scr = refs[g_pages:]
    c = pl.program_id(1)
    nsteps = pl.num_programs(1)
    t = idx_ref.shape[1]
    lp = sel_ref.shape[2]
    past = lp - LANES
    idx = idx_ref[0]
    qrows = jnp.concatenate([idx[:, h * D_IDX:(h + 1) * D_IDX] for h in range(H_IDX)], axis=0).astype(BF16)

    def score_of(qk):
        sh = jnp.maximum(qk * D_IDX ** -0.5, 0.0)
        out = None
        for h in range(H_IDX):
            term = idx[:, IDX_WI + h:IDX_WI + h + 1] * sh[h * t:(h + 1) * t]
            out = term if out is None else out + term
        return out

    for g in range(g_pages):
        kcat_scr[:, g * PAGE_SIZE:(g + 1) * PAGE_SIZE] = pages[g][...].astype(BF16)
    w = g_pages * PAGE_SIZE
    score_scr[:, pl.ds(pl.multiple_of(c * w, LANES), w)] = score_of(_dot(qrows, kcat_scr[...]))

    @pl.when(c == nsteps - 1)
    def _():
        knew = jnp.concatenate([idx[:, IDX_KI:IDX_KI + D_IDX], jnp.zeros((LANES - t, D_IDX), F32)], axis=0)
        sn = score_of(_dot_nt(qrows, knew.astype(BF16)))
        tq = lax.broadcasted_iota(I32, (t, LANES), 0)
        tk = lax.broadcasted_iota(I32, (t, LANES), 1)
        score_scr[:, past:] = jnp.where(tk <= tq, sn, NEG_INF)
        keys_scr[...] = _sortable_key(score_scr[...])

        def valid_fn(cc):
            if cc < past // LANES:
                return jnp.full((t, LANES), True)
            return tk <= tq

        def out_fn(cc, sel):
            sel_ref[0, :, cc * LANES:(cc + 1) * LANES] = jnp.where(sel, 1.0, 0.0)

        _select_mask(keys_scr, valid_fn, topk, t, lp, out_fn)


def _page_specs(n, g_pages, shape_tail):
    def spec(k):
        return pl.BlockSpec((None, None) + shape_tail,
                            lambda b, c, pt: (0, pt[b, c * g_pages + k]) + (0,) * len(shape_tail))
    return [spec(k) for k in range(n)]


def _dsa_select(page_table, idx3, cache_kidx):
    db, t, _ = idx3.shape
    n_pages = page_table.shape[1]
    past = n_pages * PAGE_SIZE
    g = min(G_SEL, n_pages)
    assert n_pages % g == 0
    lp = past + LANES
    topk = min(TOPK_MAX, (past + t) // 4)
    grid_spec = pltpu.PrefetchScalarGridSpec(
        num_scalar_prefetch=1,
        grid=(db, n_pages // g),
        in_specs=[pl.BlockSpec((1, t, IDX_W), lambda b, c, pt: (b, 0, 0))]
                 + _page_specs(g, g, (D_IDX, PAGE_SIZE)),
        out_specs=pl.BlockSpec((1, t, lp), lambda b, c, pt: (b, 0, 0)),
        scratch_shapes=[pltpu.VMEM((D_IDX, g * PAGE_SIZE), BF16),
                        pltpu.VMEM((t, lp), F32), pltpu.VMEM((t, lp), I32)],
    )
    return pl.pallas_call(
        functools.partial(_dsa_select_body, topk, g),
        grid_spec=grid_spec,
        out_shape=jax.ShapeDtypeStruct((db, t, lp), F32),
        compiler_params=_cparams(2),
        name="dsa_select",
    )(page_table, idx3, *([cache_kidx] * g))


def _dsa_sample_body(g_pages, pt_ref, relb_ref, qb_ref, sel_ref, knew_ref, vnew_ref, *refs):
    kpages = refs[:g_pages]
    vpages = refs[g_pages:2 * g_pages]
    o_ref, qbd_scr, kcat_scr, vcat_scr, bias_scr, m_scr, l_scr, acc_scr = refs[2 * g_pages:]
    b = pl.program_id(0)
    c = pl.program_id(1)
    nsteps = pl.num_programs(1)
    t = qb_ref.shape[1]
    rows = H_B * t
    w = g_pages * PAGE_SIZE
    past = sel_ref.shape[2] - LANES

    @pl.when((b == 0) & (c == 0))
    def _():
        tq = lax.broadcasted_iota(I32, (t, LANES), 0)
        col = lax.broadcasted_iota(I32, (t, LANES), 1)
        for h in range(H_B):
            rs = slice(h * t, (h + 1) * t)
            bias_scr[0, rs, :] = jnp.full((t, LANES), relb_ref[N_BUCKETS - 1, h], F32)
            bias_scr[1, rs, :] = _bias_of_dist(tq + PAGE_SIZE - col, relb_ref, h)
            bias_scr[2, rs, :] = _bias_of_dist(tq - col, relb_ref, h)

    @pl.when(c == 0)
    def _():
        q = qb_ref[0] * DH_B ** -0.5
        lane = lax.broadcasted_iota(I32, (t, D_B), 1)
        qbd = [jnp.where((lane >= h * DH_B) & (lane < (h + 1) * DH_B), q, 0.0) for h in range(H_B)]
        qbd_scr[...] = jnp.concatenate(qbd, axis=0).astype(BF16)
        m_scr[...] = jnp.full(m_scr.shape, NEG_INF, F32)
        l_scr[...] = jnp.zeros(l_scr.shape, F32)
        acc_scr[...] = jnp.zeros(acc_scr.shape, F32)

    def update(logit, sel, vmat, v_key_minor):
        selr = jnp.concatenate([sel] * H_B, axis=0) > 0.5
        lm = jnp.where(selr, logit, NEG_INF)
        m_old = m_scr[...]
        m_new = jnp.maximum(m_old, jnp.max(lm, axis=1, keepdims=True))
        m_safe = jnp.where(m_new == NEG_INF, 0.0, m_new)
        alpha = jnp.exp(m_old - m_safe)
        p = jnp.exp(lm - m_safe)
        l_scr[...] = alpha * l_scr[...] + jnp.sum(p, axis=1, keepdims=True)
        pv = _dot_nt(p.astype(BF16), vmat) if v_key_minor else _dot(p.astype(BF16), vmat)
        acc_scr[...] = alpha * acc_scr[...] + pv
        m_scr[...] = m_new

    for g in range(g_pages):
        kcat_scr[:, g * PAGE_SIZE:(g + 1) * PAGE_SIZE] = kpages[g][...].astype(BF16)
        vcat_scr[:, g * PAGE_SIZE:(g + 1) * PAGE_SIZE] = vpages[g][...].astype(BF16)
    logit = _dot(qbd_scr[...], kcat_scr[...])
    far = bias_scr[0]
    last = jnp.where(c == nsteps - 1, bias_scr[1], far)
    bias = jnp.concatenate([far] * (g_pages - 1) + [last], axis=1)
    sel = sel_ref[0, :, pl.ds(pl.multiple_of(c * w, LANES), w)]
    update(logit + bias, sel, vcat_scr[...], True)

    @pl.when(c == nsteps - 1)
    def _():
        pad = jnp.zeros((LANES - t, D_B), F32)
        knp = jnp.concatenate([knew_ref[0], pad], axis=0).astype(BF16)
        vnp = jnp.concatenate([vnew_ref[0], pad], axis=0).astype(BF16)
        ln = _dot_nt(qbd_scr[...], knp) + bias_scr[2]
        update(ln, sel_ref[0, :, past:], vnp, False)
        out = acc_scr[...] / l_scr[...]
        lane = lax.broadcasted_iota(I32, (t, D_B), 1)
        o = jnp.zeros((t, D_B), F32)
        for h in range(H_B):
            o = jnp.where((lane >= h * DH_B) & (lane < (h + 1) * DH_B), out[h * t:(h + 1) * t], o)
        o_ref[0] = o


def _dsa_sample(page_table, rel_bias, qb3, sel, k3, v3, cache_k, cache_v):
    db, t, _ = qb3.shape
    n_pages = page_table.shape[1]
    g = min(G_DSA, n_pages)
    assert n_pages % g == 0
    lp = sel.shape[2]
    rows = H_B * t
    seq = lambda w: pl.BlockSpec((1, t, w), lambda b, c, pt: (b, 0, 0))
    grid_spec = pltpu.PrefetchScalarGridSpec(
        num_scalar_prefetch=1,
        grid=(db, n_pages // g),
        in_specs=[pl.BlockSpec(memory_space=pltpu.SMEM), seq(D_B), seq(lp), seq(D_B), seq(D_B)]
                 + _page_specs(g, g, (D_B, PAGE_SIZE)) + _page_specs(g, g, (D_B, PAGE_SIZE)),
        out_specs=seq(D_B),
        scratch_shapes=[pltpu.VMEM((rows, D_B), BF16),
                        pltpu.VMEM((D_B, g * PAGE_SIZE), BF16), pltpu.VMEM((D_B, g * PAGE_SIZE), BF16),
                        pltpu.VMEM((3, rows, LANES), F32),
                        pltpu.VMEM((rows, 1), F32), pltpu.VMEM((rows, 1), F32), pltpu.VMEM((rows, D_B), F32)],
    )
    return pl.pallas_call(
        functools.partial(_dsa_sample_body, g),
        grid_spec=grid_spec,
        out_shape=jax.ShapeDtypeStruct((db, t, D_B), F32),
        compiler_params=_cparams(2),
        name="dsa_sample",
    )(page_table, rel_bias, qb3, sel, k3, v3, *([cache_k] * g), *([cache_v] * g))


def _rms(x, g):
    return x * lax.rsqrt(jnp.mean(x * x, axis=1, keepdims=True) + RMS_EPS) * g


def _odd_proj_body(x_ref, wa_ref, wb_ref, wbr_ref, wg_ref, wq_ref, wqr_ref, qn_ref, kvn_ref, cos_ref, sin_ref,
                   qcat_ref, kvpe_ref, kvpeb_ref, g_ref):
    xb = x_ref[...].astype(BF16)
    cos_k, sin_k = cos_ref[:, :LANES], sin_ref[:, :LANES]
    cos_q, sin_q = cos_ref[:, LANES:], sin_ref[:, LANES:]
    g_ref[...] = _dot(xb, wg_ref[...])
    kv = _dot(xb, wb_ref[...])
    kvr = _dot(xb, wbr_ref[...])
    ckvn = _rms(kv[:, :KV_LORA], kvn_ref[...])
    kpe = kv[:, KV_LORA:] * cos_k + kvr[:, KV_LORA:] * sin_k
    kvpe = jnp.concatenate([ckvn, kpe], axis=1)
    kvpe_ref[...] = kvpe
    kvpeb_ref[...] = kvpe.astype(BF16)
    cqn = _rms(_dot(xb, wa_ref[...]), qn_ref[...]).astype(BF16)
    qc = _dot(cqn, wq_ref[...])
    qr = _dot(cqn, wqr_ref[...])
    for h in range(H_C):
        sl = slice(h * LANES, (h + 1) * LANES)
        qcat_ref[:, sl] = (qc[:, sl] * cos_q + qr[:, sl] * sin_q).astype(qcat_ref.dtype)


def _odd_proj(x2d, wa, wb, wbr, wg, wq, wqr, qn, kvn, cos_t, sin_t, qcat_dtype):
    m = x2d.shape[0]
    tm = min(TM_PROJ, m)
    nt = cos_t.shape[0] // tm
    row = lambda w: pl.BlockSpec((tm, w), lambda i: (i, 0))
    full = lambda a: pl.BlockSpec(a.shape, lambda i: (0, 0))
    tab = pl.BlockSpec((tm, 2 * LANES), lambda i: (i % nt, 0))
    f = lambda w, dt: jax.ShapeDtypeStruct((m, w), dt)
    return pl.pallas_call(
        _odd_proj_body,
        grid=(m // tm,),
        in_specs=[row(D_MODEL)] + [full(a) for a in (wa, wb, wbr, wg, wq, wqr, qn, kvn)] + [tab, tab],
        out_specs=[row(QCAT_W), row(KVPE_W), row(KVPE_W), row(D_C)],
        out_shape=[f(QCAT_W, qcat_dtype), f(KVPE_W, F32), f(KVPE_W, BF16), f(D_C, F32)],
        compiler_params=_cparams(1),
        name="odd_proj",
    )(x2d, wa, wb, wbr, wg, wq, wqr, qn, kvn, cos_t, sin_t)


def _mla_expand_q(qcat, wexp_ref):
    rows = [_dot(qcat[:, h * LANES:(h + 1) * LANES].astype(BF16), wexp_ref[h]) for h in range(H_C)]
    return jnp.concatenate(rows, axis=0).astype(BF16)


def _mla_finish(acc_scr, l_scr, wuv_ref, t):
    o_lat = acc_scr[...] / l_scr[...]
    outs = []
    for j in range(H_C // 2):
        pair = jnp.concatenate([o_lat[(2 * j) * t:(2 * j + 1) * t], o_lat[(2 * j + 1) * t:(2 * j + 2) * t]], axis=1)
        outs.append(_dot(pair.astype(BF16), wuv_ref[j]))
    return jnp.concatenate(outs, axis=1)


def _mla_prompt_body(qcat_ref, k_ref, wexp_ref, wuv_ref, o_ref, qall_scr, m_scr, l_scr, acc_scr):
    i = pl.program_id(1)
    tq = qcat_ref.shape[1]
    qall_scr[...] = _mla_expand_q(qcat_ref[0], wexp_ref)
    m_scr[...] = jnp.full(m_scr.shape, NEG_INF, F32)
    l_scr[...] = jnp.zeros(l_scr.shape, F32)
    acc_scr[...] = jnp.zeros(acc_scr.shape, F32)
    rq = lax.broadcasted_iota(I32, (RG_MLA, TK_MLA), 0)
    qpos = i * tq + (rq & (tq - 1))
    col = lax.broadcasted_iota(I32, (RG_MLA, TK_MLA), 1)
    nchunks = (i * tq + tq + TK_MLA - 1) // TK_MLA

    def chunk(c, carry):
        k0 = pl.multiple_of(c * TK_MLA, TK_MLA)
        kc = k_ref[0, pl.ds(k0, TK_MLA), :]
        valid = (k0 + col) <= qpos
        for r in range(H_C * tq // RG_MLA):
            rs = slice(r * RG_MLA, (r + 1) * RG_MLA)
            s = _dot_nt(qall_scr[rs, :], kc) * MLA_SCALE
            s = jnp.where(valid, s, NEG_INF)
            m_old = m_scr[rs, :]
            m_new = jnp.maximum(m_old, jnp.max(s, axis=1, keepdims=True))
            alpha = jnp.exp(m_old - m_new)
            p = jnp.exp(s - m_new)
            l_scr[rs, :] = alpha * l_scr[rs, :] + jnp.sum(p, axis=1, keepdims=True)
            acc_scr[rs, :] = alpha * acc_scr[rs, :] + _dot(p.astype(BF16), kc[:, :KV_LORA])
            m_scr[rs, :] = m_new
        return carry

    lax.fori_loop(0, nchunks, chunk, 0)
    o_ref[0] = _mla_finish(acc_scr, l_scr, wuv_ref, tq)


def _mla_prompt(qcat3, kvpeb3, wexp, wuvp):
    b, s, _ = qcat3.shape
    assert s % TK_MLA == 0 and RG_MLA % TQ == 0 and TQ & (TQ - 1) == 0
    rows = H_C * TQ
    return pl.pallas_call(
        _mla_prompt_body,
        grid=(b, s // TQ),
        in_specs=[pl.BlockSpec((1, TQ, QCAT_W), lambda i, j: (i, j, 0)),
                  pl.BlockSpec((1, s, KVPE_W), lambda i, j: (i, 0, 0)),
                  pl.BlockSpec(wexp.shape, lambda i, j: (0, 0, 0)),
                  pl.BlockSpec(wuvp.shape, lambda i, j: (0, 0, 0))],
        out_specs=pl.BlockSpec((1, TQ, D_C), lambda i, j: (i, j, 0)),
        out_shape=jax.ShapeDtypeStruct((b, s, D_C), F32),
        scratch_shapes=[pltpu.VMEM((rows, KVPE_W), BF16), pltpu.VMEM((rows, 1), F32),
                        pltpu.VMEM((rows, 1), F32), pltpu.VMEM((rows, KV_LORA), F32)],
        compiler_params=_cparams(2),
        name="mla_prompt",
    )(qcat3, kvpeb3, wexp, wuvp)


def _mla_sample_body(g_pages, pt_ref, qcat_ref, knew_ref, wexp_ref, wuv_ref, *refs):
    cpages = refs[:g_pages]
    ppages = refs[g_pages:2 * g_pages]
    o_ref, qall_scr, ccat_scr, pcat_scr, m_scr, l_scr, acc_scr = refs[2 * g_pages:]
    b = pl.program_id(0)
    c = pl.program_id(1)
    nsteps = pl.num_programs(1)
    t = qcat_ref.shape[1]

    @pl.when((b == 0) & (c == 0))
    def _():
        pcat_scr[...] = jnp.zeros(pcat_scr.shape, BF16)

    @pl.when(c == 0)
    def _():
        qall_scr[...] = _mla_expand_q(qcat_ref[0], wexp_ref)
        m_scr[...] = jnp.full(m_scr.shape, NEG_INF, F32)
        l_scr[...] = jnp.zeros(l_scr.shape, F32)
        acc_scr[...] = jnp.zeros(acc_scr.shape, F32)

    def update(s, vmat):
        m_old = m_scr[...]
        m_new = jnp.maximum(m_old, jnp.max(s, axis=1, keepdims=True))
        alpha = jnp.exp(m_old - m_new)
        p = jnp.exp(s - m_new)
        l_scr[...] = alpha * l_scr[...] + jnp.sum(p, axis=1, keepdims=True)
        acc_scr[...] = alpha * acc_scr[...] + _dot(p.astype(BF16), vmat)
        m_scr[...] = m_new

    for g in range(g_pages):
        ccat_scr[g * PAGE_SIZE:(g + 1) * PAGE_SIZE, :] = cpages[g][...].astype(BF16)
        pcat_scr[0:ROPE_C, g * PAGE_SIZE:(g + 1) * PAGE_SIZE] = ppages[g][...].astype(BF16)
    q = qall_scr[...]
    s = (_dot_nt(q[:, :KV_LORA], ccat_scr[...]) + _dot(q[:, KV_LORA:], pcat_scr[...])) * MLA_SCALE
    update(s, ccat_scr[...])

    @pl.when(c == nsteps - 1)
    def _():
        knp = jnp.concatenate([knew_ref[0], jnp.zeros((LANES - t, KVPE_W), F32)], axis=0).astype(BF16)
        sn = _dot_nt(q, knp) * MLA_SCALE
        r = lax.broadcasted_iota(I32, (H_C * t, LANES), 0)
        col = lax.broadcasted_iota(I32, (H_C * t, LANES), 1)
        sn = jnp.where(col <= (r & (t - 1)), sn, NEG_INF)
        update(sn, knp[:, :KV_LORA])
        o_ref[0] = _mla_finish(acc_scr, l_scr, wuv_ref, t)


def _mla_sample(page_table, qcat3, kvpeb3, wexp, wuvp, cache_ckv, cache_kpe):
    db, t, _ = qcat3.shape
    n_pages = page_table.shape[1]
    g = min(G_MLA, n_pages)
    assert n_pages % g == 0
    rows = H_C * t
    grid_spec = pltpu.PrefetchScalarGridSpec(
        num_scalar_prefetch=1,
        grid=(db, n_pages // g),
        in_specs=[pl.BlockSpec((1, t, QCAT_W), lambda b, c, pt: (b, 0, 0)),
                  pl.BlockSpec((1, t, KVPE_W), lambda b, c, pt: (b, 0, 0)),
                  pl.BlockSpec(wexp.shape, lambda b, c, pt: (0, 0, 0)),
                  pl.BlockSpec(wuvp.shape, lambda b, c, pt: (0, 0, 0))]
                 + _page_specs(g, g, (PAGE_SIZE, KV_LORA)) + _page_specs(g, g, (ROPE_C, PAGE_SIZE)),
        out_specs=pl.BlockSpec((1, t, D_C), lambda b, c, pt: (b, 0, 0)),
        scratch_shapes=[pltpu.VMEM((rows, KVPE_W), BF16),
                        pltpu.VMEM((g * PAGE_SIZE, KV_LORA), BF16), pltpu.VMEM((LANES, g * PAGE_SIZE), BF16),
                        pltpu.VMEM((rows, 1), F32), pltpu.VMEM((rows, 1), F32), pltpu.VMEM((rows, KV_LORA), F32)],
    )
    return pl.pallas_call(
        functools.partial(_mla_sample_body, g),
        grid_spec=grid_spec,
        out_shape=jax.ShapeDtypeStruct((db, t, D_C), F32),
        compiler_params=_cparams(2),
        name="mla_sample",
    )(page_table, qcat3, kvpeb3, wexp, wuvp, *([cache_ckv] * g), *([cache_kpe] * g))


def _rope_cos_sin(pos):
    inv = ROPE_THETA ** (-jnp.arange(ROPE_HALF, dtype=F32) / ROPE_HALF)
    ang = pos.astype(F32)[:, None] * inv
    return jnp.cos(ang), jnp.sin(ang)


def _rope_lane_pattern(width, period, rot_start, limit):
    lane = np.arange(width)
    d = lane % period - rot_start
    inside = lane < limit
    x1 = inside & (d >= 0) & (d < ROPE_HALF)
    x2 = inside & (d >= ROPE_HALF) & (d < 2 * ROPE_HALF)
    return x1, x2


def _rope_tables(cos, sin, x1, x2, base):
    width = x1.shape[0]
    reps = width // ROPE_HALF
    cos_w = jnp.tile(cos, (1, reps))
    sin_w = jnp.tile(sin, (1, reps))
    cos_t = jnp.where(x1 | x2, cos_w, jnp.asarray(base, F32)[None, :])
    sin_t = jnp.where(x1, -sin_w, jnp.where(x2, sin_w, 0.0))
    return cos_t, sin_t


def _partner_columns(w, x1, x2):
    lane = np.arange(w.shape[1])
    src = lane + ROPE_HALF * x1 - ROPE_HALF * x2
    return jnp.where((x1 | x2)[None, :], w[:, src], 0.0)


def _tile_rows(tab, reps):
    return jnp.tile(tab, (reps, 1))


def kernel(x_prompt, x_sample, state_pool, cache_k_b, cache_v_b, cache_kidx_b, cache_ckv, cache_kpe, page_table, w_in_even, pool_w, pool_scale, w_out_even, rel_bias, w_in_odd, q_norm, w_q_b, kv_norm, w_uk, w_uv, w_out_odd, ln_g, ln_b):
    bsz, seq, _ = x_prompt.shape
    db, t, _ = x_sample.shape
    n_pages = page_table.shape[1]
    past = n_pages * PAGE_SIZE
    mp, ms = bsz * seq, db * t
    assert w_in_even.shape[0] == 1 and w_in_odd.shape[0] == 1 and t == 8

    cos_p, sin_p = _rope_cos_sin(jnp.arange(seq))
    cos_s, sin_s = _rope_cos_sin(past + jnp.arange(t))

    we = w_in_even[0]
    n_main = 6 * D_A
    wm = we[:, :n_main].astype(BF16)
    w_qi = we[:, n_main:n_main + H_IDX * D_IDX]
    w_ki = we[:, n_main + H_IDX * D_IDX:n_main + H_IDX * D_IDX + D_IDX]
    w_wi = we[:, n_main + H_IDX * D_IDX + D_IDX:]
    ws32 = jnp.concatenate([w_qi, w_ki, w_ki, w_wi, jnp.zeros((D_MODEL, IDX_W - IDX_WI - H_IDX), F32)], axis=1)
    x1e, x2e = _rope_lane_pattern(IDX_W, D_IDX, 0, IDX_WI)
    base_e = np.where(np.arange(IDX_W) < IDX_WI, 1.0, np.where(np.arange(IDX_W) < IDX_WI + H_IDX, H_IDX ** -0.5, 0.0))
    ws = ws32.astype(BF16)
    wr = _partner_columns(ws32, x1e, x2e).astype(BF16)
    pw = pool_w[0].astype(BF16)
    pscale = pool_scale[0][None, :]
    woe = w_out_even[0].astype(BF16)
    lng0, lnb0 = ln_g[0][None, :], ln_b[0][None, :]
    lng1, lnb1 = ln_g[1][None, :], ln_b[1][None, :]

    def even_tables(cos, sin, reps):
        ct, st = _rope_tables(cos, sin, x1e, x2e, base_e)
        return _tile_rows(ct, reps), _tile_rows(st, reps)

    xp2 = x_prompt.reshape(mp, D_MODEL)
    ct, st = even_tables(cos_p, sin_p, 1)
    u, ga, q, k, v, gb, idx, qb, kb, vb, kib = _even_proj(xp2, wm, ws, wr, ct, st)
    r3 = lambda a: a.reshape(bsz, seq, a.shape[-1])
    a_p = _pool_prompt(r3(u), pw, pscale).reshape(mp, D_A)
    o_p = _dsa_prompt(rel_bias, r3(qb), r3(idx), r3(kb), r3(vb), r3(kib)).reshape(mp, D_B)
    xp1 = _gate_out_ln(xp2, [(a_p, ga), (o_p, gb)], woe, lng0, lnb0, "even_out_prompt")
    pool_p = r3(u)[:, seq - POOL_STATE:][None]
    kb_p = k.reshape(1, bsz, seq, H_B, DH_B)
    vb_p = v.reshape(1, bsz, seq, H_B, DH_B)
    ki_p = r3(idx)[:, :, IDX_KI:IDX_KI + D_IDX][None]

    xs2 = x_sample.reshape(ms, D_MODEL)
    tm_s = min(TM_PROJ, ms)
    ct, st = even_tables(cos_s, sin_s, tm_s // t)
    u, ga, q, k, v, gb, idx, qb, kb, vb, kib = _even_proj(xs2, wm, ws, wr, ct, st)
    s3 = lambda a: a.reshape(db, t, a.shape[-1])
    u_s = s3(u)
    ext = jnp.concatenate([jnp.zeros((db, 16 - POOL_STATE, D_A), F32), state_pool[0], u_s], axis=1)
    a_s = _pool_sample(ext, pw, pscale)
    n_phys = cache_k_b.shape[1]
    kidx_t = jnp.transpose(cache_kidx_b, (0, 1, 3, 2))
    ck4 = jnp.transpose(cache_k_b, (0, 1, 3, 4, 2)).reshape(1, n_phys, D_B, PAGE_SIZE)
    cv4 = jnp.transpose(cache_v_b, (0, 1, 3, 4, 2)).reshape(1, n_phys, D_B, PAGE_SIZE)
    kpe_t = jnp.transpose(cache_kpe, (0, 1, 3, 2))
    sel = _dsa_select(page_table, s3(idx), kidx_t)
    o_s = _dsa_sample(page_table, rel_bias, s3(q), sel, s3(k), s3(v), ck4, cv4).reshape(ms, D_B)
    xs1 = _gate_out_ln(xs2, [(a_s, ga), (o_s, gb)], woe, lng0, lnb0, "even_out_sample")
    pool_s = ext[:, ext.shape[1] - POOL_STATE:][None]
    kb_s = k.reshape(1, db, t, H_B, DH_B)
    vb_s = v.reshape(1, db, t, H_B, DH_B)
    ki_s = s3(idx)[:, :, IDX_KI:IDX_KI + D_IDX][None]

    wo = w_in_odd[0]
    wa = wo[:, :Q_LORA].astype(BF16)
    wb32 = jnp.concatenate([wo[:, Q_LORA:Q_LORA + KV_LORA + ROPE_C],
                            jnp.zeros((D_MODEL, KVPE_W - KV_LORA - ROPE_C), F32)], axis=1)
    x1k, x2k = _rope_lane_pattern(KVPE_W, KVPE_W, KV_LORA, KVPE_W)
    wb = wb32.astype(BF16)
    wbr = _partner_columns(wb32, x1k, x2k).astype(BF16)
    wg = wo[:, Q_LORA + KV_LORA + ROPE_C:].astype(BF16)
    wqb = w_q_b[0]
    wq32 = jnp.concatenate([wqb, jnp.zeros((Q_LORA, H_C, LANES - NOPE_C - ROPE_C), F32)], axis=2)
    wq32 = wq32.reshape(Q_LORA, QCAT_W)
    x1q, x2q = _rope_lane_pattern(QCAT_W, LANES, NOPE_C, QCAT_W)
    wq = wq32.astype(BF16)
    wqr = _partner_columns(wq32, x1q, x2q).astype(BF16)
    qn = q_norm[0][None, :]
    kvn = kv_norm[0][None, :]
    base_k = np.zeros(LANES)
    base_q = np.where(np.arange(LANES) < NOPE_C, 1.0, 0.0)

    def odd_tables(cos, sin, reps):
        ck, sk = _rope_tables(cos, sin, x1k[KV_LORA:], x2k[KV_LORA:], base_k)
        cq, sq = _rope_tables(cos, sin, x1q[:LANES], x2q[:LANES], base_q)
        return (_tile_rows(jnp.concatenate([ck, cq], axis=1), reps),
                _tile_rows(jnp.concatenate([sk, sq], axis=1), reps))

    wuk = w_uk[0]
    wexp = jnp.zeros((H_C, LANES, KVPE_W), F32)
    wexp = wexp.at[:, :NOPE_C, :KV_LORA].set(jnp.transpose(wuk, (1, 2, 0)))
    wexp = wexp.at[:, NOPE_C:NOPE_C + ROPE_C, KV_LORA:KV_LORA + ROPE_C].set(jnp.eye(ROPE_C, dtype=F32)[None])
    wexp = wexp.astype(BF16)
    wuv = jnp.transpose(w_uv[0], (1, 0, 2))
    wuvp = jnp.zeros((H_C // 2, 2 * KV_LORA, 2 * DV_C), F32)
    wuvp = wuvp.at[:, :KV_LORA, :DV_C].set(wuv[0::2])
    wuvp = wuvp.at[:, KV_LORA:, DV_C:].set(wuv[1::2])
    wuvp = wuvp.astype(BF16)
    woo = w_out_odd[0].astype(BF16)

    ct, st = odd_tables(cos_p, sin_p, 1)
    qcat, kvpe, kvpeb, g1 = _odd_proj(xp1, wa, wb, wbr, wg, wq, wqr, qn, kvn, ct, st, BF16)
    o1 = _mla_prompt(r3(qcat), r3(kvpeb), wexp, wuvp).reshape(mp, D_C)
    y_p = _gate_out_ln(xp1, [(o1, g1)], woo, lng1, lnb1, "odd_out_prompt").reshape(bsz, seq, D_MODEL)
    ckv_p = r3(kvpe)[:, :, :KV_LORA][None]
    kpe_p = r3(kvpe)[:, :, KV_LORA:KV_LORA + ROPE_C][None]

    ct, st = odd_tables(cos_s, sin_s, tm_s // t)
    qcat, kvpe, kvpeb, g1 = _odd_proj(xs1, wa, wb, wbr, wg, wq, wqr, qn, kvn, ct, st, F32)
    o1 = _mla_sample(page_table, s3(qcat), s3(kvpe), wexp, wuvp, cache_ckv, kpe_t).reshape(ms, D_C)
    y_s = _gate_out_ln(xs1, [(o1, g1)], woo, lng1, lnb1, "odd_out_sample").reshape(db, t, D_MODEL)
    ckv_s = s3(kvpe)[:, :, :KV_LORA][None]
    kpe_s = s3(kvpe)[:, :, KV_LORA:KV_LORA + ROPE_C][None]

    return (y_p, y_s, pool_p, pool_s, kb_p, kb_s, vb_p, vb_s, ki_p, ki_s, ckv_p, ckv_s, kpe_p, kpe_s)
```

```python
import functools
import math

import numpy as np
import jax
import jax.numpy as jnp
from jax import lax
from jax.experimental import pallas as pl
from jax.experimental.pallas import tpu as pltpu

F32 = jnp.float32
BF16 = jnp.bfloat16
I32 = jnp.int32

D_MODEL = 1024
DEPTH = 2
PAGE_SIZE = 128
D_A = D_MODEL // 2
POOL_WINDOWS = (2, 4, 8, 16)
G_A = D_A // len(POOL_WINDOWS)
POOL_STATE = max(POOL_WINDOWS) - 1
H_B = 8
DH_B = 64
D_B = H_B * DH_B
H_IDX = 4
D_IDX = 64
TOPK_MAX = 256
N_BUCKETS = 32
MAX_DISTANCE = 128
H_C = 16
Q_LORA = 384
KV_LORA = 256
NOPE_C = 64
ROPE_C = 32
DV_C = 64
D_C = H_C * DV_C
MLA_SCALE = (NOPE_C + ROPE_C) ** -0.5
MLA_EXP2_SCALE = MLA_SCALE * math.log2(math.e)
ROPE_THETA = 10000.0
ROPE_HALF = 16
LN_EPS = 1e-5
RMS_EPS = 1e-6
ALPHA = (2 * DEPTH) ** 0.25

LANES = 128
IDX_W = 512
IDX_KI = 256
IDX_WI = 384
KVPE_W = 384
MLA_DEN_LANE = 320
QCAT_W = H_C * LANES

VMEM_LIMIT = 48 * 1024 * 1024

TM_PROJ = 256
TS_POOL = 512
G_POOL = 32
TM_OUT = 256
TQ = 128
TQ_MLA = 256
DSA_CHUNKS_PER_VARIANT = 4
G_SEL = 32
R_PICK = 64
G_DSA = 16
G_MLA = 16

NEG_INF = float("-inf")
INT_MIN = -(2 ** 31)


def _cparams(n_axes):
    return pltpu.CompilerParams(dimension_semantics=("arbitrary",) * n_axes,
                                vmem_limit_bytes=VMEM_LIMIT)


def _dot(a, b):
    return jnp.dot(a, b, preferred_element_type=F32)


def _dot_nt(a, b):
    return lax.dot_general(a, b, (((1,), (1,)), ((), ())), preferred_element_type=F32)


def _t5_breaks():
    max_d = 2 * MAX_DISTANCE
    d = np.arange(max_d + 1)
    me = N_BUCKETS // 2
    large = me + (np.log(np.maximum(d, 1) / me) / math.log(MAX_DISTANCE / me) * (N_BUCKETS - me)).astype(np.int64)
    large = np.minimum(large, N_BUCKETS - 1)
    bucket = np.where(d < me, d, large)
    assert np.all(bucket[MAX_DISTANCE:] == N_BUCKETS - 1)
    return int(bucket[0]), [(int(i), int(bucket[i])) for i in range(1, max_d + 1) if bucket[i] != bucket[i - 1]]


_T5_FIRST, _T5_BREAKS = _t5_breaks()


def _bias_of_dist(dist, relb_ref, h):
    val = jnp.full(dist.shape, relb_ref[_T5_FIRST, h], F32)
    for p, bk in _T5_BREAKS:
        val = jnp.where(dist >= p, relb_ref[bk, h], val)
    return val


def _sortable_key(score):
    score = jnp.where(score == 0.0, 0.0, score)
    bits = lax.bitcast_convert_type(score, I32)
    return jnp.where(bits < 0, bits ^ jnp.int32(0x7FFFFFFF), bits)


def _kth_largest_key(keys_ref, nc, rows, k, splits):
    rs = rows // splits

    def count_ge(part, cand):
        blk = keys_ref[0:nc, part * rs:(part + 1) * rs, :]
        hit = jnp.where(blk >= cand[None], 1.0, 0.0)
        return jnp.sum(jnp.sum(hit, axis=0), axis=1, keepdims=True)

    def body(it, ts):
        bit = jnp.left_shift(jnp.int32(1), jnp.int32(31) - it)
        out = []
        for part, t in enumerate(ts):
            cand = t + bit
            out.append(jnp.where(count_ge(part, cand) >= k, cand, t))
        return tuple(out)

    init = tuple(jnp.full((rs, 1), INT_MIN, I32) for _ in range(splits))
    ts = lax.fori_loop(0, 32, body, init)
    return ts[0] if splits == 1 else jnp.concatenate(ts, axis=0)


def _select_topk(keys_ref, nc, rows, topk, splits):
    t = _kth_largest_key(keys_ref, nc, rows, topk, splits)[None]
    keys = keys_ref[0:nc]
    gt = keys > t
    eqm = keys == t
    cnt_gt = jnp.sum(jnp.sum(jnp.where(gt, 1.0, 0.0), axis=0), axis=1, keepdims=True)
    need = (topk - cnt_gt)[None]
    r = lax.broadcasted_iota(I32, (LANES, 2 * LANES), 0)
    c = lax.broadcasted_iota(I32, (LANES, 2 * LANES), 1)
    su_ones = jnp.where((r < c) | (c >= LANES), 1.0, 0.0).astype(BF16)
    eq = jnp.where(eqm, 1.0, 0.0).reshape(nc * rows, LANES).astype(BF16)
    pt = _dot(eq, su_ones)
    pre = pt[:, :LANES].reshape(nc, rows, LANES)
    tot = pt[:, LANES:].reshape(nc, rows, LANES)
    offs = []
    off = jnp.zeros((rows, LANES), F32)
    for cc in range(nc):
        offs.append(off)
        off = off + tot[cc]
    before = pre + jnp.stack(offs, axis=0)
    return gt | (eqm & (before < need))


def _silu_gate(v, g):
    return v * (g * (1.0 / (1.0 + jnp.exp(-g))))


def _even_proj_body(x_ref, wm_ref, ws_ref, wr_ref, cos_ref, sin_ref,
                    u_ref, ga_ref, q_ref, k_ref, v_ref, gb_ref, idx_ref, qb_ref, kb_ref, vx_ref, kib_ref):
    xb = x_ref[...].astype(BF16)

    def mm(n):
        return _dot(xb, wm_ref[:, n * D_A:(n + 1) * D_A])

    u_ref[...] = mm(0)
    ga_ref[...] = mm(1)
    q = mm(2)
    q_ref[...] = q
    qb_ref[...] = (q * DH_B ** -0.5).astype(BF16)
    k = mm(3)
    k_ref[...] = k
    kb_ref[...] = k.astype(BF16)
    v = mm(4)
    v_ref[...] = v
    ones = jnp.ones((v.shape[0], LANES), BF16)
    for j in range(H_B // 2):
        vx_ref[:, 2 * j * LANES:(2 * j + 1) * LANES] = v[:, j * LANES:(j + 1) * LANES].astype(BF16)
        vx_ref[:, (2 * j + 1) * LANES:(2 * j + 2) * LANES] = ones
    gb_ref[...] = mm(5)
    idx = _dot(xb, ws_ref[...]) * cos_ref[...] + _dot(xb, wr_ref[...]) * sin_ref[...]
    idx_ref[...] = idx
    kib_ref[...] = idx[:, IDX_KI:IDX_KI + LANES].astype(BF16)


def _even_proj(x2d, wm, ws, wr, cos_t, sin_t):
    m = x2d.shape[0]
    tm = min(TM_PROJ, m)
    nt = cos_t.shape[0] // tm
    row = lambda w: pl.BlockSpec((tm, w), lambda i: (i, 0))
    full = lambda a: pl.BlockSpec(a.shape, lambda i: (0, 0))
    tab = pl.BlockSpec((tm, IDX_W), lambda i: (i % nt, 0))
    f = lambda w, dt: jax.ShapeDtypeStruct((m, w), dt)
    return pl.pallas_call(
        _even_proj_body,
        grid=(m // tm,),
        in_specs=[row(D_MODEL), full(wm), full(ws), full(wr), tab, tab],
        out_specs=[row(D_A)] * 6 + [row(IDX_W)] + [row(D_B)] * 2 + [row(2 * D_B), row(LANES)],
        out_shape=[f(D_A, F32)] * 6 + [f(IDX_W, F32)] + [f(D_B, BF16)] * 2 + [f(2 * D_B, BF16), f(LANES, BF16)],
        compiler_params=_cparams(1),
        name="even_proj",
    )(x2d, wm, ws, wr, cos_t, sin_t)


def _pool_mix_group(win_sum, cur, inv_cnt, pw_ref, scale_ref, g):
    pooled = win_sum * inv_cnt - cur
    mixed = _dot(pooled.astype(BF16), pw_ref[g])
    return mixed * scale_ref[:, g * G_A:(g + 1) * G_A]


def _pool_prompt_body(u_ref, halo_ref, pw_ref, scale_ref, a_ref, ext_ref):
    s = pl.program_id(1)
    ts = u_ref.shape[1]
    halo = halo_ref[0]
    ext_ref[0:16, :] = jnp.where(s == 0, 0.0, halo)
    ext_ref[16:, :] = u_ref[0]
    pos = s * ts + lax.broadcasted_iota(I32, (ts, 1), 0)
    for g, w in enumerate(POOL_WINDOWS):
        sl = pl.ds(g * G_A, G_A)
        acc = ext_ref[pl.ds(16, ts), sl]
        for kk in range(1, w):
            acc = acc + ext_ref[pl.ds(16 - kk, ts), sl]
        cnt = jnp.minimum(pos + 1, w).astype(F32)
        a_ref[0, :, g * G_A:(g + 1) * G_A] = _pool_mix_group(
            acc, ext_ref[pl.ds(16, ts), sl], 1.0 / cnt, pw_ref, scale_ref, g)


def _pool_prompt(u3, pw, scale):
    b, s, _ = u3.shape
    ts = min(TS_POOL, s)
    hb = ts // 16
    return pl.pallas_call(
        _pool_prompt_body,
        grid=(b, s // ts),
        in_specs=[pl.BlockSpec((1, ts, D_A), lambda i, j: (i, j, 0)),
                  pl.BlockSpec((1, 16, D_A), lambda i, j: (i, jnp.maximum(j * hb - 1, 0), 0)),
                  pl.BlockSpec(pw.shape, lambda i, j: (0, 0, 0)),
                  pl.BlockSpec(scale.shape, lambda i, j: (0, 0))],
        out_specs=pl.BlockSpec((1, ts, D_A), lambda i, j: (i, j, 0)),
        out_shape=jax.ShapeDtypeStruct((b, s, D_A), F32),
        scratch_shapes=[pltpu.VMEM((ts + 16, D_A), F32)],
        compiler_params=_cparams(2),
        name="pool_prompt",
    )(u3, u3, pw, scale)


def _pool_sample_body(ext_ref, pw_ref, scale_ref, a_ref):
    gs = ext_ref.shape[0]
    t = ext_ref.shape[1] - 16
    for g, w in enumerate(POOL_WINDOWS):
        sl = pl.ds(g * G_A, G_A)
        cur = ext_ref[:, pl.ds(16, t), sl]
        acc = cur
        for kk in range(1, w):
            acc = acc + ext_ref[:, pl.ds(16 - kk, t), sl]
        acc = acc.reshape(gs * t, G_A)
        cur = cur.reshape(gs * t, G_A)
        a_ref[:, g * G_A:(g + 1) * G_A] = _pool_mix_group(acc, cur, 1.0 / w, pw_ref, scale_ref, g)


def _pool_sample(ext, pw, scale):
    db, e, _ = ext.shape
    t = e - 16
    gs = min(G_POOL, db)
    return pl.pallas_call(
        _pool_sample_body,
        grid=(db // gs,),
        in_specs=[pl.BlockSpec((gs, e, D_A), lambda i: (i, 0, 0)),
                  pl.BlockSpec(pw.shape, lambda i: (0, 0, 0)),
                  pl.BlockSpec(scale.shape, lambda i: (0, 0))],
        out_specs=pl.BlockSpec((gs * t, D_A), lambda i: (i, 0)),
        out_shape=jax.ShapeDtypeStruct((db * t, D_A), F32),
        compiler_params=_cparams(1),
        name="pool_sample",
    )(ext, pw, scale)


def _gate_out_ln_body(nparts, x_ref, *refs):
    parts = refs[:2 * nparts]
    w_ref, g_ref, b_ref, y_ref = refs[2 * nparts:]
    hs = [_silu_gate(parts[2 * p][...], parts[2 * p + 1][...]).astype(BF16) for p in range(nparts)]
    h = hs[0] if nparts == 1 else jnp.concatenate(hs, axis=1)
    z = ALPHA * x_ref[...] + _dot(h, w_ref[...])
    mu = jnp.mean(z, axis=1, keepdims=True)
    zc = z - mu
    var = jnp.mean(zc * zc, axis=1, keepdims=True)
    y_ref[...] = zc * lax.rsqrt(var + LN_EPS) * g_ref[...] + b_ref[...]


def _gate_out_ln(x2d, parts, w, ln_g, ln_b, name):
    m = x2d.shape[0]
    tm = min(TM_OUT, m)
    row = lambda a: pl.BlockSpec((tm, a.shape[1]), lambda i: (i, 0))
    full = lambda a: pl.BlockSpec(a.shape, lambda i: (0, 0))
    flat = [a for pair in parts for a in pair]
    return pl.pallas_call(
        functools.partial(_gate_out_ln_body, len(parts)),
        grid=(m // tm,),
        in_specs=[row(x2d)] + [row(a) for a in flat] + [full(w), full(ln_g), full(ln_b)],
        out_specs=row(x2d),
        out_shape=jax.ShapeDtypeStruct(x2d.shape, F32),
        compiler_params=_cparams(1),
        name=name,
    )(x2d, *flat, w, ln_g, ln_b)


def _dsa_prompt_tile(nc, topk, i, qb_ref, idxq_ref, kb_ref, vx_ref, kib_ref, o_ref,
                     keys_scr, madd_scr, bias_scr):
    tq = qb_ref.shape[1]
    w = nc * LANES
    lane = lax.broadcasted_iota(I32, (tq, LANES), 1)
    lo = lane < DH_B
    idxq = idxq_ref[0]
    kib = kib_ref[0, 0:w, :]

    score = None
    for j in range(H_IDX // 2):
        pair = idxq[:, j * LANES:(j + 1) * LANES]
        for half in range(2):
            h = 2 * j + half
            lhs = jnp.where(lo if half == 0 else ~lo, pair, 0.0).astype(BF16)
            wh = idxq[:, IDX_WI + h:IDX_WI + h + 1] * D_IDX ** -0.5
            term = wh * jnp.maximum(_dot_nt(lhs, kib), 0.0)
            score = term if score is None else score + term
    qrow = i * tq + lax.broadcasted_iota(I32, (tq, LANES), 0)
    valid = [(c * LANES + lane) <= qrow for c in range(nc)]
    for c in range(nc):
        keys_scr[c] = _sortable_key(jnp.where(valid[c], score[:, c * LANES:(c + 1) * LANES], NEG_INF))
    sel = _select_topk(keys_scr, nc, tq, topk, 2)
    for c in range(nc):
        madd_scr[:, c * LANES:(c + 1) * LANES] = jnp.where(sel[c] & valid[c], 0.0, NEG_INF)

    for j in range(H_B // 2):
        qpair = qb_ref[0, :, j * LANES:(j + 1) * LANES]
        kpair = kb_ref[0, 0:w, j * LANES:(j + 1) * LANES]
        vext = vx_ref[0, 0:w, 2 * j * LANES:(2 * j + 2) * LANES]
        opair = None
        for half in range(2):
            h = 2 * j + half
            hm = lo if half == 0 else ~lo
            lhs = jnp.where(hm, qpair, jnp.zeros_like(qpair))
            logit = _dot_nt(lhs, kpair)
            pieces = []
            for c in range(nc):
                lc = logit[:, c * LANES:(c + 1) * LANES] + madd_scr[:, c * LANES:(c + 1) * LANES]
                if c >= nc - DSA_CHUNKS_PER_VARIANT - 1:
                    lc = lc + bias_scr[h, jnp.clip(i - c, 0, 2)]
                pieces.append(lc)
            logit = jnp.concatenate(pieces, axis=1)
            p = jnp.exp(logit - jnp.max(logit, axis=1, keepdims=True))
            pv = _dot(p.astype(BF16), vext)
            ov = pv[:, :LANES] / pv[:, LANES:]
            opair = ov if opair is None else jnp.where(hm, ov, opair)
        o_ref[0, :, j * LANES:(j + 1) * LANES] = opair


def _dsa_prompt_body(topk, relb_ref, qb_ref, idxq_ref, kb_ref, vx_ref, kib_ref, o_ref,
                     keys_scr, madd_scr, bias_scr):
    b = pl.program_id(0)
    i = pl.program_id(1)
    ncs = kb_ref.shape[1] // LANES

    @pl.when((b == 0) & (i == 0))
    def _():
        r = lax.broadcasted_iota(I32, (LANES, LANES), 0)
        c = lax.broadcasted_iota(I32, (LANES, LANES), 1)
        for h in range(H_B):
            far = relb_ref[N_BUCKETS - 1, h]
            bias_scr[h, 0] = _bias_of_dist(r - c, relb_ref, h) - far
            bias_scr[h, 1] = _bias_of_dist(LANES + r - c, relb_ref, h) - far
            bias_scr[h, 2] = jnp.zeros((LANES, LANES), F32)

    nvar = -(-ncs // DSA_CHUNKS_PER_VARIANT)
    for var in range(nvar):
        nc = min((var + 1) * DSA_CHUNKS_PER_VARIANT, ncs)

        @pl.when(i // DSA_CHUNKS_PER_VARIANT == var)
        def _(nc=nc):
            _dsa_prompt_tile(nc, topk, i, qb_ref, idxq_ref, kb_ref, vx_ref, kib_ref, o_ref,
                             keys_scr, madd_scr, bias_scr)


def _dsa_prompt(rel_bias, qb3, idx3, kb3, vx3, kib3):
    b, s, _ = qb3.shape
    assert s % TQ == 0 and TQ == LANES
    topk = min(TOPK_MAX, s // 4)
    qspec = lambda w: pl.BlockSpec((1, TQ, w), lambda i, j: (i, j, 0))
    kspec = lambda w: pl.BlockSpec((1, s, w), lambda i, j: (i, 0, 0))
    return pl.pallas_call(
        functools.partial(_dsa_prompt_body, topk),
        grid=(b, s // TQ),
        in_specs=[pl.BlockSpec(memory_space=pltpu.SMEM),
                  qspec(D_B), qspec(IDX_W), kspec(D_B), kspec(2 * D_B), kspec(LANES)],
        out_specs=qspec(D_B),
        out_shape=jax.ShapeDtypeStruct((b, s, D_B), F32),
        scratch_shapes=[pltpu.VMEM((s // LANES, TQ, LANES), I32), pltpu.VMEM((TQ, s), F32),
                        pltpu.VMEM((H_B, 3, LANES, LANES), F32)],
        compiler_params=_cparams(2),
        name="dsa_prompt",
    )(rel_bias, qb3, idx3, kb3, vx3, kib3)


def _dsa_score_body(g_pages, pt_ref, idx_ref, *refs):
    pages = refs[:g_pages]
    keys_ref, knew_ref, kcat_scr = refs[g_pages:]
    c = pl.program_id(1)
    nsteps = pl.num_programs(1)
    t = idx_ref.shape[1]
    idx = idx_ref[0]
    qrows = jnp.concatenate([idx[:, h * D_IDX:(h + 1) * D_IDX] for h in range(H_IDX)], axis=0).astype(BF16)

    def score_of(qk):
        sh = jnp.maximum(qk * D_IDX ** -0.5, 0.0)
        out = None
        for h in range(H_IDX):
            term = idx[:, IDX_WI + h:IDX_WI + h + 1] * sh[h * t:(h + 1) * t]
            out = term if out is None else out + term
        return out

    for g in range(g_pages):
        kcat_scr[:, g * PAGE_SIZE:(g + 1) * PAGE_SIZE] = pages[g][...].astype(BF16)
    score = score_of(_dot(qrows, kcat_scr[...]))
    for g in range(g_pages):
        keys_ref[g] = _sortable_key(score[:, g * PAGE_SIZE:(g + 1) * PAGE_SIZE])

    @pl.when(c == nsteps - 1)
    def _():
        knew = jnp.concatenate([idx[:, IDX_KI:IDX_KI + D_IDX], jnp.zeros((LANES - t, D_IDX), F32)], axis=0)
        sn = score_of(_dot_nt(qrows, knew.astype(BF16)))
        tq = lax.broadcasted_iota(I32, (t, LANES), 0)
        tk = lax.broadcasted_iota(I32, (t, LANES), 1)
        knew_ref[...] = _sortable_key(jnp.where(tk <= tq, sn, NEG_INF))


def _dsa_pick_body(topk, t, keys_ref, knew_ref, sel_ref, keys_scr):
    nc = keys_ref.shape[0]
    rows = keys_ref.shape[1]
    keys_scr[0:nc] = keys_ref[...]
    keys_scr[nc] = knew_ref[...]
    sel = _select_topk(keys_scr, nc + 1, rows, topk, 1)
    sel_ref[0:nc] = jnp.where(sel[0:nc], 1.0, 0.0)
    tq = lax.broadcasted_iota(I32, (rows, LANES), 0) & (t - 1)
    tk = lax.broadcasted_iota(I32, (rows, LANES), 1)
    sel_ref[nc] = jnp.where(sel[nc] & (tk <= tq), 1.0, 0.0)


def _page_specs(n, g_pages, shape_tail):
    def spec(k):
        return pl.BlockSpec((None, None) + shape_tail,
                            lambda b, c, pt: (0, pt[b, c * g_pages + k]) + (0,) * len(shape_tail))
    return [spec(k) for k in range(n)]


def _dsa_select(page_table, idx3, cache_kidx):
    db, t, _ = idx3.shape
    n_pages = page_table.shape[1]
    g = min(G_SEL, n_pages)
    assert n_pages % g == 0 and t & (t - 1) == 0
    topk = min(TOPK_MAX, (n_pages * PAGE_SIZE + t) // 4)
    grid_spec = pltpu.PrefetchScalarGridSpec(
        num_scalar_prefetch=1,
        grid=(db, n_pages // g),
        in_specs=[pl.BlockSpec((1, t, IDX_W), lambda b, c, pt: (b, 0, 0))]
                 + _page_specs(g, g, (D_IDX, PAGE_SIZE)),
        out_specs=[pl.BlockSpec((g, t, LANES), lambda b, c, pt: (c, b, 0)),
                   pl.BlockSpec((t, LANES), lambda b, c, pt: (b, 0))],
        scratch_shapes=[pltpu.VMEM((D_IDX, g * PAGE_SIZE), BF16)],
    )
    keys, knew = pl.pallas_call(
        functools.partial(_dsa_score_body, g),
        grid_spec=grid_spec,
        out_shape=[jax.ShapeDtypeStruct((n_pages, db * t, LANES), I32),
                   jax.ShapeDtypeStruct((db * t, LANES), I32)],
        compiler_params=_cparams(2),
        name="dsa_score",
    )(page_table, idx3, *([cache_kidx] * g))
    rows = min(R_PICK, db * t)
    return pl.pallas_call(
        functools.partial(_dsa_pick_body, topk, t),
        grid=(db * t // rows,),
        in_specs=[pl.BlockSpec((n_pages, rows, LANES), lambda r: (0, r, 0)),
                  pl.BlockSpec((rows, LANES), lambda r: (r, 0))],
        out_specs=pl.BlockSpec((n_pages + 1, rows, LANES), lambda r: (0, r, 0)),
        out_shape=jax.ShapeDtypeStruct((n_pages + 1, db * t, LANES), F32),
        scratch_shapes=[pltpu.VMEM((n_pages + 1, rows, LANES), I32)],
        compiler_params=_cparams(1),
        name="dsa_pick",
    )(keys, knew)


def _dsa_sample_body(g_pages, pt_ref, relb_ref, qb_ref, sel_ref, knew_ref, vnew_ref, *refs):
    kpages = refs[:g_pages]
    vpages = refs[g_pages:2 * g_pages]
    o_ref, qbd_scr, kcat_scr, vcat_scr, bias_scr, m_scr, l_scr, acc_scr = refs[2 * g_pages:]
    b = pl.program_id(0)
    c = pl.program_id(1)
    nsteps = pl.num_programs(1)
    t = qb_ref.shape[1]

    @pl.when((b == 0) & (c == 0))
    def _():
        tq = lax.broadcasted_iota(I32, (t, LANES), 0)
        col = lax.broadcasted_iota(I32, (t, LANES), 1)
        for h in range(H_B):
            rs = slice(h * t, (h + 1) * t)
            bias_scr[0, rs, :] = jnp.full((t, LANES), relb_ref[N_BUCKETS - 1, h], F32)
            bias_scr[1, rs, :] = _bias_of_dist(tq + PAGE_SIZE - col, relb_ref, h)
            bias_scr[2, rs, :] = _bias_of_dist(tq - col, relb_ref, h)

    @pl.when(c == 0)
    def _():
        q = qb_ref[0] * DH_B ** -0.5
        lane = lax.broadcasted_iota(I32, (t, D_B), 1)
        qbd = [jnp.where((lane >= h * DH_B) & (lane < (h + 1) * DH_B), q, 0.0) for h in range(H_B)]
        qbd_scr[...] = jnp.concatenate(qbd, axis=0).astype(BF16)
        m_scr[...] = jnp.full(m_scr.shape, NEG_INF, F32)
        l_scr[...] = jnp.zeros(l_scr.shape, F32)
        acc_scr[...] = jnp.zeros(acc_scr.shape, F32)

    def update(logit, sel, vmat, v_key_minor):
        selr = jnp.concatenate([sel] * H_B, axis=0) > 0.5
        lm = jnp.where(selr, logit, NEG_INF)
        m_old = m_scr[...]
        m_new = jnp.maximum(m_old, jnp.max(lm, axis=1, keepdims=True))
        m_safe = jnp.where(m_new == NEG_INF, 0.0, m_new)
        alpha = jnp.exp(m_old - m_safe)
        p = jnp.exp(lm - m_safe)
        l_scr[...] = alpha * l_scr[...] + jnp.sum(p, axis=1, keepdims=True)
        pv = _dot_nt(p.astype(BF16), vmat) if v_key_minor else _dot(p.astype(BF16), vmat)
        acc_scr[...] = alpha * acc_scr[...] + pv
        m_scr[...] = m_new

    for g in range(g_pages):
        kcat_scr[:, g * PAGE_SIZE:(g + 1) * PAGE_SIZE] = kpages[g][...].astype(BF16)
        vcat_scr[:, g * PAGE_SIZE:(g + 1) * PAGE_SIZE] = vpages[g][...].astype(BF16)
    logit = _dot(qbd_scr[...], kcat_scr[...])
    far = bias_scr[0]
    last = jnp.where(c == nsteps - 1, bias_scr[1], far)
    bias = jnp.concatenate([far] * (g_pages - 1) + [last], axis=1)
    sel = jnp.concatenate([sel_ref[c * g_pages + g] for g in range(g_pages)], axis=1)
    update(logit + bias, sel, vcat_scr[...], True)

    @pl.when(c == nsteps - 1)
    def _():
        pad = jnp.zeros((LANES - t, D_B), F32)
        knp = jnp.concatenate([knew_ref[0], pad], axis=0).astype(BF16)
        vnp = jnp.concatenate([vnew_ref[0], pad], axis=0).astype(BF16)
        ln = _dot_nt(qbd_scr[...], knp) + bias_scr[2]
        update(ln, sel_ref[sel_ref.shape[0] - 1], vnp, False)
        out = acc_scr[...] / l_scr[...]
        lane = lax.broadcasted_iota(I32, (t, D_B), 1)
        o = jnp.zeros((t, D_B), F32)
        for h in range(H_B):
            o = jnp.where((lane >= h * DH_B) & (lane < (h + 1) * DH_B), out[h * t:(h + 1) * t], o)
        o_ref[0] = o


def _dsa_sample(page_table, rel_bias, qb3, sel, k3, v3, cache_k, cache_v):
    db, t, _ = qb3.shape
    n_pages = page_table.shape[1]
    g = min(G_DSA, n_pages)
    assert n_pages % g == 0
    rows = H_B * t
    seq = lambda w: pl.BlockSpec((1, t, w), lambda b, c, pt: (b, 0, 0))
    sel_spec = pl.BlockSpec((sel.shape[0], t, LANES), lambda b, c, pt: (0, b, 0))
    grid_spec = pltpu.PrefetchScalarGridSpec(
        num_scalar_prefetch=1,
        grid=(db, n_pages // g),
        in_specs=[pl.BlockSpec(memory_space=pltpu.SMEM), seq(D_B), sel_spec, seq(D_B), seq(D_B)]
                 + _page_specs(g, g, (D_B, PAGE_SIZE)) + _page_specs(g, g, (D_B, PAGE_SIZE)),
        out_specs=seq(D_B),
        scratch_shapes=[pltpu.VMEM((rows, D_B), BF16),
                        pltpu.VMEM((D_B, g * PAGE_SIZE), BF16), pltpu.VMEM((D_B, g * PAGE_SIZE), BF16),
                        pltpu.VMEM((3, rows, LANES), F32),
                        pltpu.VMEM((rows, 1), F32), pltpu.VMEM((rows, 1), F32), pltpu.VMEM((rows, D_B), F32)],
    )
    return pl.pallas_call(
        functools.partial(_dsa_sample_body, g),
        grid_spec=grid_spec,
        out_shape=jax.ShapeDtypeStruct((db, t, D_B), F32),
        compiler_params=_cparams(2),
        name="dsa_sample",
    )(page_table, rel_bias, qb3, sel, k3, v3, *([cache_k] * g), *([cache_v] * g))


def _ones_beyond_rope(kpe_slab):
    lane = lax.broadcasted_iota(I32, kpe_slab.shape, 1)
    return jnp.where(lane < ROPE_C, kpe_slab, 1.0)


def _rms(x, g):
    return x * lax.rsqrt(jnp.mean(x * x, axis=1, keepdims=True) + RMS_EPS) * g


def _odd_proj_body(x_ref, wa_ref, wb_ref, wbr_ref, wg_ref, wq_ref, wqr_ref, qn_ref, kvn_ref, cos_ref, sin_ref,
                   qcat_ref, kvpe_ref, kvpeb_ref, g_ref):
    xb = x_ref[...].astype(BF16)
    cos_k, sin_k = cos_ref[:, :LANES], sin_ref[:, :LANES]
    cos_q, sin_q = cos_ref[:, LANES:], sin_ref[:, LANES:]
    g_ref[...] = _dot(xb, wg_ref[...])
    kv = _dot(xb, wb_ref[...])
    kvr = _dot(xb, wbr_ref[...])
    ckvn = _rms(kv[:, :KV_LORA], kvn_ref[...])
    kpe = kv[:, KV_LORA:] * cos_k + kvr[:, KV_LORA:] * sin_k
    kvpe_ref[...] = jnp.concatenate([ckvn, kpe], axis=1)
    kvpeb_ref[...] = jnp.concatenate([ckvn, _ones_beyond_rope(kpe)], axis=1).astype(BF16)
    cqn = _rms(_dot(xb, wa_ref[...]), qn_ref[...]).astype(BF16)
    qc = _dot(cqn, wq_ref[...])
    qr = _dot(cqn, wqr_ref[...])
    for h in range(H_C):
        sl = slice(h * LANES, (h + 1) * LANES)
        qcat_ref[:, sl] = (qc[:, sl] * cos_q + qr[:, sl] * sin_q).astype(qcat_ref.dtype)


def _odd_proj(x2d, wa, wb, wbr, wg, wq, wqr, qn, kvn, cos_t, sin_t, qcat_dtype):
    m = x2d.shape[0]
    tm = min(TM_PROJ, m)
    nt = cos_t.shape[0] // tm
    row = lambda w: pl.BlockSpec((tm, w), lambda i: (i, 0))
    full = lambda a: pl.BlockSpec(a.shape, lambda i: (0, 0))
    tab = pl.BlockSpec((tm, 2 * LANES), lambda i: (i % nt, 0))
    f = lambda w, dt: jax.ShapeDtypeStruct((m, w), dt)
    return pl.pallas_call(
        _odd_proj_body,
        grid=(m // tm,),
        in_specs=[row(D_MODEL)] + [full(a) for a in (wa, wb, wbr, wg, wq, wqr, qn, kvn)] + [tab, tab],
        out_specs=[row(QCAT_W), row(KVPE_W), row(KVPE_W), row(D_C)],
        out_shape=[f(QCAT_W, qcat_dtype), f(KVPE_W, F32), f(KVPE_W, BF16), f(D_C, F32)],
        compiler_params=_cparams(1),
        name="odd_proj",
    )(x2d, wa, wb, wbr, wg, wq, wqr, qn, kvn, cos_t, sin_t)


def _mla_expand_q(qcat, wexp_ref):
    rows = [_dot(qcat[:, h * LANES:(h + 1) * LANES].astype(BF16), wexp_ref[h]) for h in range(H_C)]
    return jnp.concatenate(rows, axis=0).astype(BF16)


def _mla_softmax_step(s, m_old, vmat):
    m_new = jnp.maximum(m_old, jnp.max(s, axis=1, keepdims=True))
    alpha = jnp.exp2((m_old - m_new) * MLA_EXP2_SCALE)
    m_wide = jnp.concatenate([m_new] * (s.shape[1] // LANES), axis=1)
    p = jnp.exp2((s - m_wide) * MLA_EXP2_SCALE)
    pv = _dot(p.astype(BF16), vmat)
    return m_new, jnp.concatenate([alpha] * (KVPE_W // LANES), axis=1), pv


def _mla_head_pair_out(acc_a, acc_b, wuv_pair):
    lat = [a[:, :KV_LORA] / a[:, MLA_DEN_LANE:MLA_DEN_LANE + 1] for a in (acc_a, acc_b)]
    return _dot(jnp.concatenate(lat, axis=1).astype(BF16), wuv_pair)


def _mla_prompt_body(qcat_ref, k_ref, wexp_ref, wuv_ref, o_ref, qall_scr, m_scr, acc_scr):
    i = pl.program_id(1)
    tq = qcat_ref.shape[1]
    qall_scr[...] = _mla_expand_q(qcat_ref[0], wexp_ref)
    m_scr[...] = jnp.full(m_scr.shape, NEG_INF, F32)
    acc_scr[...] = jnp.zeros(acc_scr.shape, F32)

    def chunk(c, causal):
        k0 = pl.multiple_of(c * tq, tq)
        kc = k_ref[0, pl.ds(k0, tq), :]
        for h in range(H_C):
            rs = slice(h * tq, (h + 1) * tq)
            s = _dot_nt(qall_scr[rs, :], kc)
            if causal:
                row = lax.broadcasted_iota(I32, (tq, tq), 0)
                col = lax.broadcasted_iota(I32, (tq, tq), 1)
                s = jnp.where(col <= row, s, NEG_INF)
            m_new, alpha, pv = _mla_softmax_step(s, m_scr[rs, :], kc)
            acc_scr[rs, :] = alpha * acc_scr[rs, :] + pv
            m_scr[rs, :] = m_new

    def full_chunk(c, carry):
        chunk(c, False)
        return carry

    lax.fori_loop(0, i, full_chunk, 0)
    chunk(i, True)
    for j in range(H_C // 2):
        a = acc_scr[2 * j * tq:(2 * j + 1) * tq, :]
        b = acc_scr[(2 * j + 1) * tq:(2 * j + 2) * tq, :]
        o_ref[0, :, 2 * j * DV_C:(2 * j + 2) * DV_C] = _mla_head_pair_out(a, b, wuv_ref[j])


def _mla_prompt(qcat3, kvpeb3, wexp, wuvp):
    b, s, _ = qcat3.shape
    tq = min(TQ_MLA, s)
    assert s % tq == 0
    rows = H_C * tq
    return pl.pallas_call(
        _mla_prompt_body,
        grid=(b, s // tq),
        in_specs=[pl.BlockSpec((1, tq, QCAT_W), lambda i, j: (i, j, 0)),
                  pl.BlockSpec((1, s, KVPE_W), lambda i, j: (i, 0, 0)),
                  pl.BlockSpec(wexp.shape, lambda i, j: (0, 0, 0)),
                  pl.BlockSpec(wuvp.shape, lambda i, j: (0, 0, 0))],
        out_specs=pl.BlockSpec((1, tq, D_C), lambda i, j: (i, j, 0)),
        out_shape=jax.ShapeDtypeStruct((b, s, D_C), F32),
        scratch_shapes=[pltpu.VMEM((rows, KVPE_W), BF16), pltpu.VMEM((rows, LANES), F32),
                        pltpu.VMEM((rows, KVPE_W), F32)],
        compiler_params=_cparams(2),
        name="mla_prompt",
    )(qcat3, kvpeb3, wexp, wuvp)


def _mla_sample_body(g_pages, pt_ref, qcat_ref, knew_ref, wexp_ref, wuv_ref, *refs):
    cpages = refs[:g_pages]
    ppages = refs[g_pages:2 * g_pages]
    o_ref, qall_scr, ccat_scr, pcat_scr, m_scr, acc_scr = refs[2 * g_pages:]
    b = pl.program_id(0)
    c = pl.program_id(1)
    nsteps = pl.num_programs(1)
    t = qcat_ref.shape[1]

    @pl.when((b == 0) & (c == 0))
    def _():
        pcat_scr[...] = jnp.zeros(pcat_scr.shape, BF16)
        ccat_scr[:, KV_LORA:] = jnp.ones((ccat_scr.shape[0], KVPE_W - KV_LORA), BF16)

    @pl.when(c == 0)
    def _():
        qall_scr[...] = _mla_expand_q(qcat_ref[0], wexp_ref)
        m_scr[...] = jnp.full(m_scr.shape, NEG_INF, F32)
        acc_scr[...] = jnp.zeros(acc_scr.shape, F32)

    def update(s, vmat):
        m_new, alpha, pv = _mla_softmax_step(s, m_scr[...], vmat)
        acc_scr[...] = alpha * acc_scr[...] + pv
        m_scr[...] = m_new

    for g in range(g_pages):
        ccat_scr[g * PAGE_SIZE:(g + 1) * PAGE_SIZE, 0:KV_LORA] = cpages[g][...].astype(BF16)
        pcat_scr[0:ROPE_C, g * PAGE_SIZE:(g + 1) * PAGE_SIZE] = ppages[g][...].astype(BF16)
    q = qall_scr[...]
    s = _dot_nt(q[:, :KV_LORA], ccat_scr[:, 0:KV_LORA]) + _dot(q[:, KV_LORA:], pcat_scr[...])
    update(s, ccat_scr[...])

    @pl.when(c == nsteps - 1)
    def _():
        kn = knew_ref[0]
        kn = jnp.concatenate([kn[:, :KV_LORA], _ones_beyond_rope(kn[:, KV_LORA:])], axis=1)
        knp = jnp.concatenate([kn, jnp.zeros((LANES - t, KVPE_W), F32)], axis=0).astype(BF16)
        sn = _dot_nt(q, knp)
        r = lax.broadcasted_iota(I32, (H_C * t, LANES), 0)
        col = lax.broadcasted_iota(I32, (H_C * t, LANES), 1)
        sn = jnp.where(col <= (r & (t - 1)), sn, NEG_INF)
        update(sn, knp)
        acc = acc_scr[...]
        outs = [_mla_head_pair_out(acc[2 * j * t:(2 * j + 1) * t], acc[(2 * j + 1) * t:(2 * j + 2) * t], wuv_ref[j])
                for j in range(H_C // 2)]
        o_ref[0] = jnp.concatenate(outs, axis=1)


def _mla_sample(page_table, qcat3, kvpeb3, wexp, wuvp, cache_ckv, cache_kpe):
    db, t, _ = qcat3.shape
    n_pages = page_table.shape[1]
    g = min(G_MLA, n_pages)
    assert n_pages % g == 0
    rows = H_C * t
    grid_spec = pltpu.PrefetchScalarGridSpec(
        num_scalar_prefetch=1,
        grid=(db, n_pages // g),
        in_specs=[pl.BlockSpec((1, t, QCAT_W), lambda b, c, pt: (b, 0, 0)),
                  pl.BlockSpec((1, t, KVPE_W), lambda b, c, pt: (b, 0, 0)),
                  pl.BlockSpec(wexp.shape, lambda b, c, pt: (0, 0, 0)),
                  pl.BlockSpec(wuvp.shape, lambda b, c, pt: (0, 0, 0))]
                 + _page_specs(g, g, (PAGE_SIZE, KV_LORA)) + _page_specs(g, g, (ROPE_C, PAGE_SIZE)),
        out_specs=pl.BlockSpec((1, t, D_C), lambda b, c, pt: (b, 0, 0)),
        scratch_shapes=[pltpu.VMEM((rows, KVPE_W), BF16),
                        pltpu.VMEM((g * PAGE_SIZE, KVPE_W), BF16), pltpu.VMEM((LANES, g * PAGE_SIZE), BF16),
                        pltpu.VMEM((rows, LANES), F32), pltpu.VMEM((rows, KVPE_W), F32)],
    )
    return pl.pallas_call(
        functools.partial(_mla_sample_body, g),
        grid_spec=grid_spec,
        out_shape=jax.ShapeDtypeStruct((db, t, D_C), F32),
        compiler_params=_cparams(2),
        name="mla_sample",
    )(page_table, qcat3, kvpeb3, wexp, wuvp, *([cache_ckv] * g), *([cache_kpe] * g))


def _rope_cos_sin(pos):
    inv = ROPE_THETA ** (-jnp.arange(ROPE_HALF, dtype=F32) / ROPE_HALF)
    ang = pos.astype(F32)[:, None] * inv
    return jnp.cos(ang), jnp.sin(ang)


def _rope_lane_pattern(width, period, rot_start, limit):
    lane = np.arange(width)
    d = lane % period - rot_start
    inside = lane < limit
    x1 = inside & (d >= 0) & (d < ROPE_HALF)
    x2 = inside & (d >= ROPE_HALF) & (d < 2 * ROPE_HALF)
    return x1, x2


def _rope_tables(cos, sin, x1, x2, base):
    width = x1.shape[0]
    reps = width // ROPE_HALF
    cos_w = jnp.tile(cos, (1, reps))
    sin_w = jnp.tile(sin, (1, reps))
    cos_t = jnp.where(x1 | x2, cos_w, jnp.asarray(base, F32)[None, :])
    sin_t = jnp.where(x1, -sin_w, jnp.where(x2, sin_w, 0.0))
    return cos_t, sin_t


def _partner_columns(w, x1, x2):
    lane = np.arange(w.shape[1])
    src = lane + ROPE_HALF * x1 - ROPE_HALF * x2
    return jnp.where((x1 | x2)[None, :], w[:, src], 0.0)


def _tile_rows(tab, reps):
    return jnp.tile(tab, (reps, 1))


def kernel(x_prompt, x_sample, state_pool, cache_k_b, cache_v_b, cache_kidx_b, cache_ckv, cache_kpe, page_table, w_in_even, pool_w, pool_scale, w_out_even, rel_bias, w_in_odd, q_norm, w_q_b, kv_norm, w_uk, w_uv, w_out_odd, ln_g, ln_b):
    bsz, seq, _ = x_prompt.shape
    db, t, _ = x_sample.shape
    n_pages = page_table.shape[1]
    past = n_pages * PAGE_SIZE
    mp, ms = bsz * seq, db * t
    assert w_in_even.shape[0] == 1 and w_in_odd.shape[0] == 1 and t == 8

    cos_p, sin_p = _rope_cos_sin(jnp.arange(seq))
    cos_s, sin_s = _rope_cos_sin(past + jnp.arange(t))

    we = w_in_even[0]
    n_main = 6 * D_A
    wm = we[:, :n_main].astype(BF16)
    w_qi = we[:, n_main:n_main + H_IDX * D_IDX]
    w_ki = we[:, n_main + H_IDX * D_IDX:n_main + H_IDX * D_IDX + D_IDX]
    w_wi = we[:, n_main + H_IDX * D_IDX + D_IDX:]
    ws32 = jnp.concatenate([w_qi, w_ki, w_ki, w_wi, jnp.zeros((D_MODEL, IDX_W - IDX_WI - H_IDX), F32)], axis=1)
    x1e, x2e = _rope_lane_pattern(IDX_W, D_IDX, 0, IDX_WI)
    base_e = np.where(np.arange(IDX_W) < IDX_WI, 1.0, np.where(np.arange(IDX_W) < IDX_WI + H_IDX, H_IDX ** -0.5, 0.0))
    ws = ws32.astype(BF16)
    wr = _partner_columns(ws32, x1e, x2e).astype(BF16)
    pw = pool_w[0].astype(BF16)
    pscale = pool_scale[0][None, :]
    woe = w_out_even[0].astype(BF16)
    lng0, lnb0 = ln_g[0][None, :], ln_b[0][None, :]
    lng1, lnb1 = ln_g[1][None, :], ln_b[1][None, :]

    def even_tables(cos, sin, reps):
        ct, st = _rope_tables(cos, sin, x1e, x2e, base_e)
        return _tile_rows(ct, reps), _tile_rows(st, reps)

    xp2 = x_prompt.reshape(mp, D_MODEL)
    ct, st = even_tables(cos_p, sin_p, 1)
    u, ga, q, k, v, gb, idx, qb, kb, vx, kib = _even_proj(xp2, wm, ws, wr, ct, st)
    r3 = lambda a: a.reshape(bsz, seq, a.shape[-1])
    a_p = _pool_prompt(r3(u), pw, pscale).reshape(mp, D_A)
    o_p = _dsa_prompt(rel_bias, r3(qb), r3(idx), r3(kb), r3(vx), r3(kib)).reshape(mp, D_B)
    xp1 = _gate_out_ln(xp2, [(a_p, ga), (o_p, gb)], woe, lng0, lnb0, "even_out_prompt")
    pool_p = r3(u)[:, seq - POOL_STATE:][None]
    kb_p = k.reshape(1, bsz, seq, H_B, DH_B)
    vb_p = v.reshape(1, bsz, seq, H_B, DH_B)
    ki_p = r3(idx)[:, :, IDX_KI:IDX_KI + D_IDX][None]

    xs2 = x_sample.reshape(ms, D_MODEL)
    tm_s = min(TM_PROJ, ms)
    ct, st = even_tables(cos_s, sin_s, tm_s // t)
    u, ga, q, k, v, gb, idx, qb, kb, vx, kib = _even_proj(xs2, wm, ws, wr, ct, st)
    s3 = lambda a: a.reshape(db, t, a.shape[-1])
    u_s = s3(u)
    ext = jnp.concatenate([jnp.zeros((db, 16 - POOL_STATE, D_A), F32), state_pool[0], u_s], axis=1)
    a_s = _pool_sample(ext, pw, pscale)
    n_phys = cache_k_b.shape[1]
    kidx_t = jnp.transpose(cache_kidx_b, (0, 1, 3, 2))
    ck4 = jnp.transpose(cache_k_b, (0, 1, 3, 4, 2)).reshape(1, n_phys, D_B, PAGE_SIZE)
    cv4 = jnp.transpose(cache_v_b, (0, 1, 3, 4, 2)).reshape(1, n_phys, D_B, PAGE_SIZE)
    kpe_t = jnp.transpose(cache_kpe, (0, 1, 3, 2))
    sel = _dsa_select(page_table, s3(idx), kidx_t)
    o_s = _dsa_sample(page_table, rel_bias, s3(q), sel, s3(k), s3(v), ck4, cv4).reshape(ms, D_B)
    xs1 = _gate_out_ln(xs2, [(a_s, ga), (o_s, gb)], woe, lng0, lnb0, "even_out_sample")
    pool_s = ext[:, ext.shape[1] - POOL_STATE:][None]
    kb_s = k.reshape(1, db, t, H_B, DH_B)
    vb_s = v.reshape(1, db, t, H_B, DH_B)
    ki_s = s3(idx)[:, :, IDX_KI:IDX_KI + D_IDX][None]

    wo = w_in_odd[0]
    wa = wo[:, :Q_LORA].astype(BF16)
    wb32 = jnp.concatenate([wo[:, Q_LORA:Q_LORA + KV_LORA + ROPE_C],
                            jnp.zeros((D_MODEL, KVPE_W - KV_LORA - ROPE_C), F32)], axis=1)
    x1k, x2k = _rope_lane_pattern(KVPE_W, KVPE_W, KV_LORA, KVPE_W)
    wb = wb32.astype(BF16)
    wbr = _partner_columns(wb32, x1k, x2k).astype(BF16)
    wg = wo[:, Q_LORA + KV_LORA + ROPE_C:].astype(BF16)
    wqb = w_q_b[0]
    wq32 = jnp.concatenate([wqb, jnp.zeros((Q_LORA, H_C, LANES - NOPE_C - ROPE_C), F32)], axis=2)
    wq32 = wq32.reshape(Q_LORA, QCAT_W)
    x1q, x2q = _rope_lane_pattern(QCAT_W, LANES, NOPE_C, QCAT_W)
    wq = wq32.astype(BF16)
    wqr = _partner_columns(wq32, x1q, x2q).astype(BF16)
    qn = q_norm[0][None, :]
    kvn = kv_norm[0][None, :]
    base_k = np.zeros(LANES)
    base_q = np.where(np.arange(LANES) < NOPE_C, 1.0, 0.0)

    def odd_tables(cos, sin, reps):
        ck, sk = _rope_tables(cos, sin, x1k[KV_LORA:], x2k[KV_LORA:], base_k)
        cq, sq = _rope_tables(cos, sin, x1q[:LANES], x2q[:LANES], base_q)
        return (_tile_rows(jnp.concatenate([ck, cq], axis=1), reps),
                _tile_rows(jnp.concatenate([sk, sq], axis=1), reps))

    wuk = w_uk[0]
    wexp = jnp.zeros((H_C, LANES, KVPE_W), F32)
    wexp = wexp.at[:, :NOPE_C, :KV_LORA].set(jnp.transpose(wuk, (1, 2, 0)))
    wexp = wexp.at[:, NOPE_C:NOPE_C + ROPE_C, KV_LORA:KV_LORA + ROPE_C].set(jnp.eye(ROPE_C, dtype=F32)[None])
    wexp = wexp.astype(BF16)
    wuv = jnp.transpose(w_uv[0], (1, 0, 2))
    wuvp = jnp.zeros((H_C // 2, 2 * KV_LORA, 2 * DV_C), F32)
    wuvp = wuvp.at[:, :KV_LORA, :DV_C].set(wuv[0::2])
    wuvp = wuvp.at[:, KV_LORA:, DV_C:].set(wuv[1::2])
    wuvp = wuvp.astype(BF16)
    woo = w_out_odd[0].astype(BF16)

    ct, st = odd_tables(cos_p, sin_p, 1)
    qcat, kvpe, kvpeb, g1 = _odd_proj(xp1, wa, wb, wbr, wg, wq, wqr, qn, kvn, ct, st, BF16)
    o1 = _mla_prompt(r3(qcat), r3(kvpeb), wexp, wuvp).reshape(mp, D_C)
    y_p = _gate_out_ln(xp1, [(o1, g1)], woo, lng1, lnb1, "odd_out_prompt").reshape(bsz, seq, D_MODEL)
    ckv_p = r3(kvpe)[:, :, :KV_LORA][None]
    kpe_p = r3(kvpe)[:, :, KV_LORA:KV_LORA + ROPE_C][None]

    ct, st = odd_tables(cos_s, sin_s, tm_s // t)
    qcat, kvpe, kvpeb, g1 = _odd_proj(xs1, wa, wb, wbr, wg, wq, wqr, qn, kvn, ct, st, F32)
    o1 = _mla_sample(page_table, s3(qcat), s3(kvpe), wexp, wuvp, cache_ckv, kpe_t).reshape(ms, D_C)
    y_s = _gate_out_ln(xs1, [(o1, g1)], woo, lng1, lnb1, "odd_out_sample").reshape(db, t, D_MODEL)
    ckv_s = s3(kvpe)[:, :, :KV_LORA][None]
    kpe_s = s3(kvpe)[:, :, KV_LORA:KV_LORA + ROPE_C][None]

    return (y_p, y_s, pool_p, pool_s, kb_p, kb_s, vb_p, vb_s, ki_p, ki_s, ckv_p, ckv_s, kpe_p, kpe_s)
```

```python
import functools
import math

import numpy as np
import jax
import jax.numpy as jnp
from jax import lax
from jax.experimental import pallas as pl
from jax.experimental.pallas import tpu as pltpu

F32 = jnp.float32
BF16 = jnp.bfloat16
I32 = jnp.int32

D_MODEL = 1024
DEPTH = 2
PAGE_SIZE = 128
D_A = D_MODEL // 2
POOL_WINDOWS = (2, 4, 8, 16)
G_A = D_A // len(POOL_WINDOWS)
POOL_STATE = max(POOL_WINDOWS) - 1
H_B = 8
DH_B = 64
D_B = H_B * DH_B
H_IDX = 4
D_IDX = 64
TOPK_MAX = 256
N_BUCKETS = 32
MAX_DISTANCE = 128
H_C = 16
Q_LORA = 384
KV_LORA = 256
NOPE_C = 64
ROPE_C = 32
DV_C = 64
D_C = H_C * DV_C
MLA_SCALE = (NOPE_C + ROPE_C) ** -0.5
MLA_EXP2_SCALE = MLA_SCALE * math.log2(math.e)
ROPE_THETA = 10000.0
ROPE_HALF = 16
LN_EPS = 1e-5
RMS_EPS = 1e-6
ALPHA = (2 * DEPTH) ** 0.25

LANES = 128
IDX_W = 512
IDX_KI = 256
IDX_WI = 384
KVPE_W = 384
MLA_DEN_LANE = 320
QCAT_W = H_C * LANES

VMEM_LIMIT = 48 * 1024 * 1024

TM_PROJ = 256
TS_POOL = 512
G_POOL = 32
TM_OUT = 256
TQ = 128
TQ_MLA = 256
DSA_CHUNKS_PER_VARIANT = 4
G_SEL = 32
R_PICK = 64
G_DSA = 16
G_MLA = 32
MLA_SUB_PAGES = 8

NEG_INF = float("-inf")
INT_MIN = -(2 ** 31)


def _cparams(n_axes):
    return pltpu.CompilerParams(dimension_semantics=("arbitrary",) * n_axes,
                                vmem_limit_bytes=VMEM_LIMIT)


def _dot(a, b):
    return jnp.dot(a, b, preferred_element_type=F32)


def _dot_nt(a, b):
    return lax.dot_general(a, b, (((1,), (1,)), ((), ())), preferred_element_type=F32)


def _t5_breaks():
    max_d = 2 * MAX_DISTANCE
    d = np.arange(max_d + 1)
    me = N_BUCKETS // 2
    large = me + (np.log(np.maximum(d, 1) / me) / math.log(MAX_DISTANCE / me) * (N_BUCKETS - me)).astype(np.int64)
    large = np.minimum(large, N_BUCKETS - 1)
    bucket = np.where(d < me, d, large)
    assert np.all(bucket[MAX_DISTANCE:] == N_BUCKETS - 1)
    return int(bucket[0]), [(int(i), int(bucket[i])) for i in range(1, max_d + 1) if bucket[i] != bucket[i - 1]]


_T5_FIRST, _T5_BREAKS = _t5_breaks()


def _bias_of_dist(dist, relb_ref, h):
    val = jnp.full(dist.shape, relb_ref[_T5_FIRST, h], F32)
    for p, bk in _T5_BREAKS:
        val = jnp.where(dist >= p, relb_ref[bk, h], val)
    return val


def _sortable_key(score):
    score = jnp.where(score == 0.0, 0.0, score)
    bits = lax.bitcast_convert_type(score, I32)
    return jnp.where(bits < 0, bits ^ jnp.int32(0x7FFFFFFF), bits)


def _kth_largest_key(keys_ref, nc, rows, k, splits):
    rs = rows // splits

    def count_ge(part, cand):
        blk = keys_ref[0:nc, part * rs:(part + 1) * rs, :]
        hit = jnp.where(blk >= cand[None], 1.0, 0.0)
        return jnp.sum(jnp.sum(hit, axis=0), axis=1, keepdims=True)

    def body(it, ts):
        bit = jnp.left_shift(jnp.int32(1), jnp.int32(31) - it)
        out = []
        for part, t in enumerate(ts):
            cand = t + bit
            out.append(jnp.where(count_ge(part, cand) >= k, cand, t))
        return tuple(out)

    init = tuple(jnp.full((rs, 1), INT_MIN, I32) for _ in range(splits))
    ts = lax.fori_loop(0, 32, body, init)
    return ts[0] if splits == 1 else jnp.concatenate(ts, axis=0)


def _select_topk(keys_ref, nc, rows, topk, splits):
    t = _kth_largest_key(keys_ref, nc, rows, topk, splits)[None]
    keys = keys_ref[0:nc]
    gt = keys > t
    eqm = keys == t
    cnt_gt = jnp.sum(jnp.sum(jnp.where(gt, 1.0, 0.0), axis=0), axis=1, keepdims=True)
    need = (topk - cnt_gt)[None]
    r = lax.broadcasted_iota(I32, (LANES, 2 * LANES), 0)
    c = lax.broadcasted_iota(I32, (LANES, 2 * LANES), 1)
    su_ones = jnp.where((r < c) | (c >= LANES), 1.0, 0.0).astype(BF16)
    eq = jnp.where(eqm, 1.0, 0.0).reshape(nc * rows, LANES).astype(BF16)
    pt = _dot(eq, su_ones)
    pre = pt[:, :LANES].reshape(nc, rows, LANES)
    tot = pt[:, LANES:].reshape(nc, rows, LANES)
    offs = []
    off = jnp.zeros((rows, LANES), F32)
    for cc in range(nc):
        offs.append(off)
        off = off + tot[cc]
    before = pre + jnp.stack(offs, axis=0)
    return gt | (eqm & (before < need))


def _silu_gate(v, g):
    return v * (g * (1.0 / (1.0 + jnp.exp(-g))))


def _even_proj_body(kv_feature_major, x_ref, wm_ref, ws_ref, wr_ref, cos_ref, sin_ref,
                    u_ref, ga_ref, q_ref, k_ref, v_ref, gb_ref, idx_ref, qb_ref, kb_ref, vx_ref, kib_ref):
    xb = x_ref[...].astype(BF16)

    def mm(n):
        return _dot(xb, wm_ref[:, n * D_A:(n + 1) * D_A])

    u_ref[...] = mm(0)
    ga_ref[...] = mm(1)
    q = mm(2)
    q_ref[...] = q
    qb_ref[...] = (q * DH_B ** -0.5).astype(BF16)
    k = mm(3)
    kb_ref[...] = k.astype(BF16)
    v = mm(4)
    if kv_feature_major:
        k_ref[0] = k.T
        v_ref[0] = v.T
    else:
        k_ref[...] = k
        v_ref[...] = v
    ones = jnp.ones((v.shape[0], LANES), BF16)
    for j in range(H_B // 2):
        vx_ref[:, 2 * j * LANES:(2 * j + 1) * LANES] = v[:, j * LANES:(j + 1) * LANES].astype(BF16)
        vx_ref[:, (2 * j + 1) * LANES:(2 * j + 2) * LANES] = ones
    gb_ref[...] = mm(5)
    idx = _dot(xb, ws_ref[...]) * cos_ref[...] + _dot(xb, wr_ref[...]) * sin_ref[...]
    idx_ref[...] = idx
    kib_ref[...] = idx[:, IDX_KI:IDX_KI + LANES].astype(BF16)


def _even_proj(x2d, wm, ws, wr, cos_t, sin_t, seq_len=None):
    m = x2d.shape[0]
    tm = min(TM_PROJ, m)
    nt = cos_t.shape[0] // tm
    row = lambda w: pl.BlockSpec((tm, w), lambda i: (i, 0))
    full = lambda a: pl.BlockSpec(a.shape, lambda i: (0, 0))
    tab = pl.BlockSpec((tm, IDX_W), lambda i: (i % nt, 0))
    f = lambda w, dt: jax.ShapeDtypeStruct((m, w), dt)
    if seq_len is None:
        kv_spec, kv_shape = row(D_B), f(D_B, F32)
    else:
        spt = seq_len // tm
        kv_spec = pl.BlockSpec((1, D_B, tm), lambda i: (i // spt, 0, i % spt))
        kv_shape = jax.ShapeDtypeStruct((m // seq_len, D_B, seq_len), F32)
    return pl.pallas_call(
        functools.partial(_even_proj_body, seq_len is not None),
        grid=(m // tm,),
        in_specs=[row(D_MODEL), full(wm), full(ws), full(wr), tab, tab],
        out_specs=[row(D_A)] * 3 + [kv_spec] * 2 + [row(D_A), row(IDX_W), row(D_B), row(D_B), row(2 * D_B), row(LANES)],
        out_shape=[f(D_A, F32)] * 3 + [kv_shape] * 2 + [f(D_A, F32), f(IDX_W, F32), f(D_B, BF16), f(D_B, BF16),
                                                         f(2 * D_B, BF16), f(LANES, BF16)],
        compiler_params=_cparams(1),
        name="even_proj",
    )(x2d, wm, ws, wr, cos_t, sin_t)


def _pool_mix_group(win_sum, cur, inv_cnt, pw_ref, scale_ref, g):
    pooled = win_sum * inv_cnt - cur
    mixed = _dot(pooled.astype(BF16), pw_ref[g])
    return mixed * scale_ref[:, g * G_A:(g + 1) * G_A]


def _pool_prompt_body(u_ref, halo_ref, pw_ref, scale_ref, a_ref, ext_ref):
    s = pl.program_id(1)
    ts = u_ref.shape[1]
    halo = halo_ref[0]
    ext_ref[0:16, :] = jnp.where(s == 0, 0.0, halo)
    ext_ref[16:, :] = u_ref[0]
    pos = s * ts + lax.broadcasted_iota(I32, (ts, 1), 0)
    for g, w in enumerate(POOL_WINDOWS):
        sl = pl.ds(g * G_A, G_A)
        acc = ext_ref[pl.ds(16, ts), sl]
        for kk in range(1, w):
            acc = acc + ext_ref[pl.ds(16 - kk, ts), sl]
        cnt = jnp.minimum(pos + 1, w).astype(F32)
        a_ref[0, :, g * G_A:(g + 1) * G_A] = _pool_mix_group(
            acc, ext_ref[pl.ds(16, ts), sl], 1.0 / cnt, pw_ref, scale_ref, g)


def _pool_prompt(u3, pw, scale):
    b, s, _ = u3.shape
    ts = min(TS_POOL, s)
    hb = ts // 16
    return pl.pallas_call(
        _pool_prompt_body,
        grid=(b, s // ts),
        in_specs=[pl.BlockSpec((1, ts, D_A), lambda i, j: (i, j, 0)),
                  pl.BlockSpec((1, 16, D_A), lambda i, j: (i, jnp.maximum(j * hb - 1, 0), 0)),
                  pl.BlockSpec(pw.shape, lambda i, j: (0, 0, 0)),
                  pl.BlockSpec(scale.shape, lambda i, j: (0, 0))],
        out_specs=pl.BlockSpec((1, ts, D_A), lambda i, j: (i, j, 0)),
        out_shape=jax.ShapeDtypeStruct((b, s, D_A), F32),
        scratch_shapes=[pltpu.VMEM((ts + 16, D_A), F32)],
        compiler_params=_cparams(2),
        name="pool_prompt",
    )(u3, u3, pw, scale)


def _pool_sample_body(ext_ref, pw_ref, scale_ref, a_ref):
    gs = ext_ref.shape[0]
    t = ext_ref.shape[1] - 16
    for g, w in enumerate(POOL_WINDOWS):
        sl = pl.ds(g * G_A, G_A)
        cur = ext_ref[:, pl.ds(16, t), sl]
        acc = cur
        for kk in range(1, w):
            acc = acc + ext_ref[:, pl.ds(16 - kk, t), sl]
        acc = acc.reshape(gs * t, G_A)
        cur = cur.reshape(gs * t, G_A)
        a_ref[:, g * G_A:(g + 1) * G_A] = _pool_mix_group(acc, cur, 1.0 / w, pw_ref, scale_ref, g)


def _pool_sample(ext, pw, scale):
    db, e, _ = ext.shape
    t = e - 16
    gs = min(G_POOL, db)
    return pl.pallas_call(
        _pool_sample_body,
        grid=(db // gs,),
        in_specs=[pl.BlockSpec((gs, e, D_A), lambda i: (i, 0, 0)),
                  pl.BlockSpec(pw.shape, lambda i: (0, 0, 0)),
                  pl.BlockSpec(scale.shape, lambda i: (0, 0))],
        out_specs=pl.BlockSpec((gs * t, D_A), lambda i: (i, 0)),
        out_shape=jax.ShapeDtypeStruct((db * t, D_A), F32),
        compiler_params=_cparams(1),
        name="pool_sample",
    )(ext, pw, scale)


def _gate_out_ln_body(nparts, x_ref, *refs):
    parts = refs[:2 * nparts]
    w_ref, g_ref, b_ref, y_ref = refs[2 * nparts:]
    hs = [_silu_gate(parts[2 * p][...], parts[2 * p + 1][...]).astype(BF16) for p in range(nparts)]
    h = hs[0] if nparts == 1 else jnp.concatenate(hs, axis=1)
    z = ALPHA * x_ref[...] + _dot(h, w_ref[...])
    mu = jnp.mean(z, axis=1, keepdims=True)
    zc = z - mu
    var = jnp.mean(zc * zc, axis=1, keepdims=True)
    y_ref[...] = zc * lax.rsqrt(var + LN_EPS) * g_ref[...] + b_ref[...]


def _gate_out_ln(x2d, parts, w, ln_g, ln_b, name):
    m = x2d.shape[0]
    tm = min(TM_OUT, m)
    row = lambda a: pl.BlockSpec((tm, a.shape[1]), lambda i: (i, 0))
    full = lambda a: pl.BlockSpec(a.shape, lambda i: (0, 0))
    flat = [a for pair in parts for a in pair]
    return pl.pallas_call(
        functools.partial(_gate_out_ln_body, len(parts)),
        grid=(m // tm,),
        in_specs=[row(x2d)] + [row(a) for a in flat] + [full(w), full(ln_g), full(ln_b)],
        out_specs=row(x2d),
        out_shape=jax.ShapeDtypeStruct(x2d.shape, F32),
        compiler_params=_cparams(1),
        name=name,
    )(x2d, *flat, w, ln_g, ln_b)


def _dsa_prompt_tile(nc, topk, i, qb_ref, idxq_ref, kb_ref, vx_ref, kib_ref, o_ref,
                     keys_scr, madd_scr, bias_scr):
    tq = qb_ref.shape[1]
    w = nc * LANES
    lane = lax.broadcasted_iota(I32, (tq, LANES), 1)
    lo = lane < DH_B
    idxq = idxq_ref[0]
    kib = kib_ref[0, 0:w, :]

    score = None
    for j in range(H_IDX // 2):
        pair = idxq[:, j * LANES:(j + 1) * LANES]
        for half in range(2):
            h = 2 * j + half
            lhs = jnp.where(lo if half == 0 else ~lo, pair, 0.0).astype(BF16)
            wh = idxq[:, IDX_WI + h:IDX_WI + h + 1] * D_IDX ** -0.5
            term = wh * jnp.maximum(_dot_nt(lhs, kib), 0.0)
            score = term if score is None else score + term
    qrow = i * tq + lax.broadcasted_iota(I32, (tq, LANES), 0)
    valid = [(c * LANES + lane) <= qrow for c in range(nc)]
    for c in range(nc):
        keys_scr[c] = _sortable_key(jnp.where(valid[c], score[:, c * LANES:(c + 1) * LANES], NEG_INF))
    sel = _select_topk(keys_scr, nc, tq, topk, 2)
    for c in range(nc):
        madd_scr[:, c * LANES:(c + 1) * LANES] = jnp.where(sel[c] & valid[c], 0.0, NEG_INF)

    for j in range(H_B // 2):
        qpair = qb_ref[0, :, j * LANES:(j + 1) * LANES]
        kpair = kb_ref[0, 0:w, j * LANES:(j + 1) * LANES]
        vext = vx_ref[0, 0:w, 2 * j * LANES:(2 * j + 2) * LANES]
        opair = None
        for half in range(2):
            h = 2 * j + half
            hm = lo if half == 0 else ~lo
            lhs = jnp.where(hm, qpair, jnp.zeros_like(qpair))
            logit = _dot_nt(lhs, kpair)
            pieces = []
            for c in range(nc):
                lc = logit[:, c * LANES:(c + 1) * LANES] + madd_scr[:, c * LANES:(c + 1) * LANES]
                if c >= nc - DSA_CHUNKS_PER_VARIANT - 1:
                    lc = lc + bias_scr[h, jnp.clip(i - c, 0, 2)]
                pieces.append(lc)
            logit = jnp.concatenate(pieces, axis=1)
            p = jnp.exp(logit - jnp.max(logit, axis=1, keepdims=True))
            pv = _dot(p.astype(BF16), vext)
            ov = pv[:, :LANES] / pv[:, LANES:]
            opair = ov if opair is None else jnp.where(hm, ov, opair)
        o_ref[0, :, j * LANES:(j + 1) * LANES] = opair


def _dsa_prompt_body(topk, relb_ref, qb_ref, idxq_ref, kb_ref, vx_ref, kib_ref, o_ref,
                     keys_scr, madd_scr, bias_scr):
    b = pl.program_id(0)
    i = pl.program_id(1)
    ncs = kb_ref.shape[1] // LANES

    @pl.when((b == 0) & (i == 0))
    def _():
        r = lax.broadcasted_iota(I32, (LANES, LANES), 0)
        c = lax.broadcasted_iota(I32, (LANES, LANES), 1)
        for h in range(H_B):
            far = relb_ref[N_BUCKETS - 1, h]
            bias_scr[h, 0] = _bias_of_dist(r - c, relb_ref, h) - far
            bias_scr[h, 1] = _bias_of_dist(LANES + r - c, relb_ref, h) - far
            bias_scr[h, 2] = jnp.zeros((LANES, LANES), F32)

    nvar = -(-ncs // DSA_CHUNKS_PER_VARIANT)
    for var in range(nvar):
        nc = min((var + 1) * DSA_CHUNKS_PER_VARIANT, ncs)

        @pl.when(i // DSA_CHUNKS_PER_VARIANT == var)
        def _(nc=nc):
            _dsa_prompt_tile(nc, topk, i, qb_ref, idxq_ref, kb_ref, vx_ref, kib_ref, o_ref,
                             keys_scr, madd_scr, bias_scr)


def _dsa_prompt(rel_bias, qb3, idx3, kb3, vx3, kib3):
    b, s, _ = qb3.shape
    assert s % TQ == 0 and TQ == LANES
    topk = min(TOPK_MAX, s // 4)
    qspec = lambda w: pl.BlockSpec((1, TQ, w), lambda i, j: (i, j, 0))
    kspec = lambda w: pl.BlockSpec((1, s, w), lambda i, j: (i, 0, 0))
    return pl.pallas_call(
        functools.partial(_dsa_prompt_body, topk),
        grid=(b, s // TQ),
        in_specs=[pl.BlockSpec(memory_space=pltpu.SMEM),
                  qspec(D_B), qspec(IDX_W), kspec(D_B), kspec(2 * D_B), kspec(LANES)],
        out_specs=qspec(D_B),
        out_shape=jax.ShapeDtypeStruct((b, s, D_B), F32),
        scratch_shapes=[pltpu.VMEM((s // LANES, TQ, LANES), I32), pltpu.VMEM((TQ, s), F32),
                        pltpu.VMEM((H_B, 3, LANES, LANES), F32)],
        compiler_params=_cparams(2),
        name="dsa_prompt",
    )(rel_bias, qb3, idx3, kb3, vx3, kib3)


def _dsa_score_body(g_pages, pt_ref, idx_ref, *refs):
    pages = refs[:g_pages]
    keys_ref, knew_ref, kcat_scr = refs[g_pages:]
    c = pl.program_id(1)
    nsteps = pl.num_programs(1)
    t = idx_ref.shape[1]
    idx = idx_ref[0]
    qrows = jnp.concatenate([idx[:, h * D_IDX:(h + 1) * D_IDX] for h in range(H_IDX)], axis=0).astype(BF16)

    def score_of(qk):
        sh = jnp.maximum(qk * D_IDX ** -0.5, 0.0)
        out = None
        for h in range(H_IDX):
            term = idx[:, IDX_WI + h:IDX_WI + h + 1] * sh[h * t:(h + 1) * t]
            out = term if out is None else out + term
        return out

    for g in range(g_pages):
        kcat_scr[:, g * PAGE_SIZE:(g + 1) * PAGE_SIZE] = pages[g][...].astype(BF16)
    score = score_of(_dot(qrows, kcat_scr[...]))
    for g in range(g_pages):
        keys_ref[g] = _sortable_key(score[:, g * PAGE_SIZE:(g + 1) * PAGE_SIZE])

    @pl.when(c == nsteps - 1)
    def _():
        knew = jnp.concatenate([idx[:, IDX_KI:IDX_KI + D_IDX], jnp.zeros((LANES - t, D_IDX), F32)], axis=0)
        sn = score_of(_dot_nt(qrows, knew.astype(BF16)))
        tq = lax.broadcasted_iota(I32, (t, LANES), 0)
        tk = lax.broadcasted_iota(I32, (t, LANES), 1)
        knew_ref[...] = _sortable_key(jnp.where(tk <= tq, sn, NEG_INF))


def _dsa_pick_body(topk, t, keys_ref, knew_ref, sel_ref, keys_scr):
    nc = keys_ref.shape[0]
    rows = keys_ref.shape[1]
    keys_scr[0:nc] = keys_ref[...]
    keys_scr[nc] = knew_ref[...]
    sel = _select_topk(keys_scr, nc + 1, rows, topk, 1)
    sel_ref[0:nc] = jnp.where(sel[0:nc], 1.0, 0.0)
    tq = lax.broadcasted_iota(I32, (rows, LANES), 0) & (t - 1)
    tk = lax.broadcasted_iota(I32, (rows, LANES), 1)
    sel_ref[nc] = jnp.where(sel[nc] & (tk <= tq), 1.0, 0.0)


def _page_specs(n, g_pages, shape_tail):
    def spec(k):
        return pl.BlockSpec((None, None) + shape_tail,
                            lambda b, c, pt: (0, pt[b, c * g_pages + k]) + (0,) * len(shape_tail))
    return [spec(k) for k in range(n)]


def _dsa_select(page_table, idx3, cache_kidx):
    db, t, _ = idx3.shape
    n_pages = page_table.shape[1]
    g = min(G_SEL, n_pages)
    assert n_pages % g == 0 and t & (t - 1) == 0
    topk = min(TOPK_MAX, (n_pages * PAGE_SIZE + t) // 4)
    grid_spec = pltpu.PrefetchScalarGridSpec(
        num_scalar_prefetch=1,
        grid=(db, n_pages // g),
        in_specs=[pl.BlockSpec((1, t, IDX_W), lambda b, c, pt: (b, 0, 0))]
                 + _page_specs(g, g, (D_IDX, PAGE_SIZE)),
        out_specs=[pl.BlockSpec((g, t, LANES), lambda b, c, pt: (c, b, 0)),
                   pl.BlockSpec((t, LANES), lambda b, c, pt: (b, 0))],
        scratch_shapes=[pltpu.VMEM((D_IDX, g * PAGE_SIZE), BF16)],
    )
    keys, knew = pl.pallas_call(
        functools.partial(_dsa_score_body, g),
        grid_spec=grid_spec,
        out_shape=[jax.ShapeDtypeStruct((n_pages, db * t, LANES), I32),
                   jax.ShapeDtypeStruct((db * t, LANES), I32)],
        compiler_params=_cparams(2),
        name="dsa_score",
    )(page_table, idx3, *([cache_kidx] * g))
    rows = min(R_PICK, db * t)
    return pl.pallas_call(
        functools.partial(_dsa_pick_body, topk, t),
        grid=(db * t // rows,),
        in_specs=[pl.BlockSpec((n_pages, rows, LANES), lambda r: (0, r, 0)),
                  pl.BlockSpec((rows, LANES), lambda r: (r, 0))],
        out_specs=pl.BlockSpec((n_pages + 1, rows, LANES), lambda r: (0, r, 0)),
        out_shape=jax.ShapeDtypeStruct((n_pages + 1, db * t, LANES), F32),
        scratch_shapes=[pltpu.VMEM((n_pages + 1, rows, LANES), I32)],
        compiler_params=_cparams(1),
        name="dsa_pick",
    )(keys, knew)


def _dsa_sample_body(g_pages, pt_ref, relb_ref, qb_ref, sel_ref, knew_ref, vnew_ref, *refs):
    kpages = refs[:g_pages]
    vpages = refs[g_pages:2 * g_pages]
    o_ref, qbd_scr, kcat_scr, vcat_scr, bias_scr, m_scr, l_scr, acc_scr = refs[2 * g_pages:]
    b = pl.program_id(0)
    c = pl.program_id(1)
    nsteps = pl.num_programs(1)
    t = qb_ref.shape[1]

    @pl.when((b == 0) & (c == 0))
    def _():
        tq = lax.broadcasted_iota(I32, (t, LANES), 0)
        col = lax.broadcasted_iota(I32, (t, LANES), 1)
        for h in range(H_B):
            rs = slice(h * t, (h + 1) * t)
            bias_scr[0, rs, :] = jnp.full((t, LANES), relb_ref[N_BUCKETS - 1, h], F32)
            bias_scr[1, rs, :] = _bias_of_dist(tq + PAGE_SIZE - col, relb_ref, h)
            bias_scr[2, rs, :] = _bias_of_dist(tq - col, relb_ref, h)

    @pl.when(c == 0)
    def _():
        q = qb_ref[0] * DH_B ** -0.5
        lane = lax.broadcasted_iota(I32, (t, D_B), 1)
        qbd = [jnp.where((lane >= h * DH_B) & (lane < (h + 1) * DH_B), q, 0.0) for h in range(H_B)]
        qbd_scr[...] = jnp.concatenate(qbd, axis=0).astype(BF16)
        m_scr[...] = jnp.full(m_scr.shape, NEG_INF, F32)
        l_scr[...] = jnp.zeros(l_scr.shape, F32)
        acc_scr[...] = jnp.zeros(acc_scr.shape, F32)

    def update(logit, sel, vmat, v_key_minor):
        selr = jnp.concatenate([sel] * H_B, axis=0) > 0.5
        lm = jnp.where(selr, logit, NEG_INF)
        m_old = m_scr[...]
        m_new = jnp.maximum(m_old, jnp.max(lm, axis=1, keepdims=True))
        m_safe = jnp.where(m_new == NEG_INF, 0.0, m_new)
        alpha = jnp.exp(m_old - m_safe)
        p = jnp.exp(lm - m_safe)
        l_scr[...] = alpha * l_scr[...] + jnp.sum(p, axis=1, keepdims=True)
        pv = _dot_nt(p.astype(BF16), vmat) if v_key_minor else _dot(p.astype(BF16), vmat)
        acc_scr[...] = alpha * acc_scr[...] + pv
        m_scr[...] = m_new

    for g in range(g_pages):
        kcat_scr[:, g * PAGE_SIZE:(g + 1) * PAGE_SIZE] = kpages[g][...].astype(BF16)
        vcat_scr[:, g * PAGE_SIZE:(g + 1) * PAGE_SIZE] = vpages[g][...].astype(BF16)
    logit = _dot(qbd_scr[...], kcat_scr[...])
    far = bias_scr[0]
    last = jnp.where(c == nsteps - 1, bias_scr[1], far)
    bias = jnp.concatenate([far] * (g_pages - 1) + [last], axis=1)
    sel = jnp.concatenate([sel_ref[c * g_pages + g] for g in range(g_pages)], axis=1)
    update(logit + bias, sel, vcat_scr[...], True)

    @pl.when(c == nsteps - 1)
    def _():
        pad = jnp.zeros((LANES - t, D_B), F32)
        knp = jnp.concatenate([knew_ref[0], pad], axis=0).astype(BF16)
        vnp = jnp.concatenate([vnew_ref[0], pad], axis=0).astype(BF16)
        ln = _dot_nt(qbd_scr[...], knp) + bias_scr[2]
        update(ln, sel_ref[sel_ref.shape[0] - 1], vnp, False)
        out = acc_scr[...] / l_scr[...]
        lane = lax.broadcasted_iota(I32, (t, D_B), 1)
        o = jnp.zeros((t, D_B), F32)
        for h in range(H_B):
            o = jnp.where((lane >= h * DH_B) & (lane < (h + 1) * DH_B), out[h * t:(h + 1) * t], o)
        o_ref[0] = o


def _dsa_sample(page_table, rel_bias, qb3, sel, k3, v3, cache_k, cache_v):
    db, t, _ = qb3.shape
    n_pages = page_table.shape[1]
    g = min(G_DSA, n_pages)
    assert n_pages % g == 0
    rows = H_B * t
    seq = lambda w: pl.BlockSpec((1, t, w), lambda b, c, pt: (b, 0, 0))
    sel_spec = pl.BlockSpec((sel.shape[0], t, LANES), lambda b, c, pt: (0, b, 0))
    grid_spec = pltpu.PrefetchScalarGridSpec(
        num_scalar_prefetch=1,
        grid=(db, n_pages // g),
        in_specs=[pl.BlockSpec(memory_space=pltpu.SMEM), seq(D_B), sel_spec, seq(D_B), seq(D_B)]
                 + _page_specs(g, g, (D_B, PAGE_SIZE)) + _page_specs(g, g, (D_B, PAGE_SIZE)),
        out_specs=seq(D_B),
        scratch_shapes=[pltpu.VMEM((rows, D_B), BF16),
                        pltpu.VMEM((D_B, g * PAGE_SIZE), BF16), pltpu.VMEM((D_B, g * PAGE_SIZE), BF16),
                        pltpu.VMEM((3, rows, LANES), F32),
                        pltpu.VMEM((rows, 1), F32), pltpu.VMEM((rows, 1), F32), pltpu.VMEM((rows, D_B), F32)],
    )
    return pl.pallas_call(
        functools.partial(_dsa_sample_body, g),
        grid_spec=grid_spec,
        out_shape=jax.ShapeDtypeStruct((db, t, D_B), F32),
        compiler_params=_cparams(2),
        name="dsa_sample",
    )(page_table, rel_bias, qb3, sel, k3, v3, *([cache_k] * g), *([cache_v] * g))


def _ones_beyond_rope(kpe_slab):
    lane = lax.broadcasted_iota(I32, kpe_slab.shape, 1)
    return jnp.where(lane < ROPE_C, kpe_slab, 1.0)


def _rms(x, g):
    return x * lax.rsqrt(jnp.mean(x * x, axis=1, keepdims=True) + RMS_EPS) * g


def _odd_proj_body(x_ref, wa_ref, wb_ref, wbr_ref, wg_ref, wq_ref, wqr_ref, qn_ref, kvn_ref, cos_ref, sin_ref,
                   qcat_ref, kvpe_ref, kvpeb_ref, g_ref):
    xb = x_ref[...].astype(BF16)
    cos_k, sin_k = cos_ref[:, :LANES], sin_ref[:, :LANES]
    cos_q, sin_q = cos_ref[:, LANES:], sin_ref[:, LANES:]
    g_ref[...] = _dot(xb, wg_ref[...])
    kv = _dot(xb, wb_ref[...])
    kvr = _dot(xb, wbr_ref[...])
    ckvn = _rms(kv[:, :KV_LORA], kvn_ref[...])
    kpe = kv[:, KV_LORA:] * cos_k + kvr[:, KV_LORA:] * sin_k
    kvpe = jnp.concatenate([ckvn, kpe], axis=1)
    kvpe_ref[...] = kvpe
    kvpeb_ref[...] = kvpe.astype(BF16)
    cqn = _rms(_dot(xb, wa_ref[...]), qn_ref[...]).astype(BF16)
    qc = _dot(cqn, wq_ref[...])
    qr = _dot(cqn, wqr_ref[...])
    for h in range(H_C):
        sl = slice(h * LANES, (h + 1) * LANES)
        qcat_ref[:, sl] = (qc[:, sl] * cos_q + qr[:, sl] * sin_q).astype(qcat_ref.dtype)


def _odd_proj(x2d, wa, wb, wbr, wg, wq, wqr, qn, kvn, cos_t, sin_t, qcat_dtype):
    m = x2d.shape[0]
    tm = min(TM_PROJ, m)
    nt = cos_t.shape[0] // tm
    row = lambda w: pl.BlockSpec((tm, w), lambda i: (i, 0))
    full = lambda a: pl.BlockSpec(a.shape, lambda i: (0, 0))
    tab = pl.BlockSpec((tm, 2 * LANES), lambda i: (i % nt, 0))
    f = lambda w, dt: jax.ShapeDtypeStruct((m, w), dt)
    return pl.pallas_call(
        _odd_proj_body,
        grid=(m // tm,),
        in_specs=[row(D_MODEL)] + [full(a) for a in (wa, wb, wbr, wg, wq, wqr, qn, kvn)] + [tab, tab],
        out_specs=[row(QCAT_W), row(KVPE_W), row(KVPE_W), row(D_C)],
        out_shape=[f(QCAT_W, qcat_dtype), f(KVPE_W, F32), f(KVPE_W, BF16), f(D_C, F32)],
        compiler_params=_cparams(1),
        name="odd_proj",
    )(x2d, wa, wb, wbr, wg, wq, wqr, qn, kvn, cos_t, sin_t)


def _mla_expand_q(qcat, wexp_ref):
    rows = [_dot(qcat[:, h * LANES:(h + 1) * LANES].astype(BF16), wexp_ref[h]) for h in range(H_C)]
    return jnp.concatenate(rows, axis=0).astype(BF16)


def _mla_softmax_step(s, m_old, vmat):
    m_new = jnp.maximum(m_old, jnp.max(s, axis=1, keepdims=True))
    alpha = jnp.exp2((m_old - m_new) * MLA_EXP2_SCALE)
    m_wide = jnp.concatenate([m_new] * (s.shape[1] // LANES), axis=1)
    p = jnp.exp2((s - m_wide) * MLA_EXP2_SCALE)
    pv = _dot(p.astype(BF16), vmat)
    return m_new, jnp.concatenate([alpha] * (KVPE_W // LANES), axis=1), pv


def _mla_head_pair_out(acc_a, acc_b, wuv_pair):
    lat = [a[:, :KV_LORA] / a[:, MLA_DEN_LANE:MLA_DEN_LANE + 1] for a in (acc_a, acc_b)]
    return _dot(jnp.concatenate(lat, axis=1).astype(BF16), wuv_pair)


def _mla_prompt_body(qcat_ref, kv_ref, wk_ref, wv_ref, o_ref, kx_scr, vx_scr, m_scr, acc_scr):
    i = pl.program_id(1)
    tq = qcat_ref.shape[1]
    s = kv_ref.shape[1]

    @pl.when(i == 0)
    def _():
        lane = lax.broadcasted_iota(I32, (tq, QCAT_W), 1)
        ones = (((lane >> 7) ^ (lane >> 6)) & 1) == 1
        for r in range(s // tq):
            rs = slice(r * tq, (r + 1) * tq)
            kv = kv_ref[0, rs, :]
            kx_scr[rs, :] = _dot(kv, wk_ref[...]).astype(BF16)
            vx_scr[rs, :] = jnp.where(ones, 1.0, _dot(kv[:, :KV_LORA], wv_ref[...])).astype(BF16)

    m_scr[...] = jnp.full(m_scr.shape, NEG_INF, F32)
    acc_scr[...] = jnp.zeros(acc_scr.shape, F32)

    def chunk(c, causal):
        k0 = pl.multiple_of(c * tq, tq)
        for h in range(H_C):
            hs = slice(h * LANES, (h + 1) * LANES)
            rs = slice(h * tq, (h + 1) * tq)
            sc = _dot_nt(qcat_ref[0, :, hs], kx_scr[pl.ds(k0, tq), hs])
            if causal:
                row = lax.broadcasted_iota(I32, (tq, tq), 0)
                col = lax.broadcasted_iota(I32, (tq, tq), 1)
                sc = jnp.where(col <= row, sc, NEG_INF)
            m_old = m_scr[rs, :]
            m_new = jnp.maximum(m_old, jnp.max(sc, axis=1, keepdims=True))
            alpha = jnp.exp2((m_old - m_new) * MLA_EXP2_SCALE)
            p = jnp.exp2((sc - jnp.concatenate([m_new] * (tq // LANES), axis=1)) * MLA_EXP2_SCALE)
            acc_scr[rs, :] = alpha * acc_scr[rs, :] + _dot(p.astype(BF16), vx_scr[pl.ds(k0, tq), hs])
            m_scr[rs, :] = m_new

    def full_chunk(c, carry):
        chunk(c, False)
        return carry

    lax.fori_loop(0, i, full_chunk, 0)
    chunk(i, True)
    lo = lax.broadcasted_iota(I32, (tq, LANES), 1) < DV_C
    for j in range(H_C // 2):
        ae = acc_scr[2 * j * tq:(2 * j + 1) * tq, :]
        ao = acc_scr[(2 * j + 1) * tq:(2 * j + 2) * tq, :]
        o_ref[0, :, j * LANES:(j + 1) * LANES] = jnp.where(
            lo, ae / pltpu.roll(ae, DV_C, 1), ao / pltpu.roll(ao, DV_C, 1))


def _mla_prompt(qcat3, kvpeb3, wk2, wv2):
    b, s, _ = qcat3.shape
    tq = min(TQ_MLA, s)
    assert s % tq == 0 and tq % LANES == 0
    rows = H_C * tq
    return pl.pallas_call(
        _mla_prompt_body,
        grid=(b, s // tq),
        in_specs=[pl.BlockSpec((1, tq, QCAT_W), lambda i, j: (i, j, 0)),
                  pl.BlockSpec((1, s, KVPE_W), lambda i, j: (i, 0, 0)),
                  pl.BlockSpec(wk2.shape, lambda i, j: (0, 0)),
                  pl.BlockSpec(wv2.shape, lambda i, j: (0, 0))],
        out_specs=pl.BlockSpec((1, tq, D_C), lambda i, j: (i, j, 0)),
        out_shape=jax.ShapeDtypeStruct((b, s, D_C), F32),
        scratch_shapes=[pltpu.VMEM((s, QCAT_W), BF16), pltpu.VMEM((s, QCAT_W), BF16),
                        pltpu.VMEM((rows, LANES), F32), pltpu.VMEM((rows, LANES), F32)],
        compiler_params=_cparams(2),
        name="mla_prompt",
    )(qcat3, kvpeb3, wk2, wv2)


def _mla_sample_body(g_pages, pt_ref, qcat_ref, knew_ref, wexp_ref, wuv_ref, *refs):
    cpages = refs[:g_pages]
    ppages = refs[g_pages:2 * g_pages]
    o_ref, qall_scr, ccat_scr, pcat_scr, m_scr, acc_scr = refs[2 * g_pages:]
    b = pl.program_id(0)
    c = pl.program_id(1)
    nsteps = pl.num_programs(1)
    t = qcat_ref.shape[1]

    @pl.when((b == 0) & (c == 0))
    def _():
        pcat_scr[...] = jnp.zeros(pcat_scr.shape, BF16)
        ccat_scr[:, KV_LORA:] = jnp.ones((ccat_scr.shape[0], KVPE_W - KV_LORA), BF16)

    @pl.when(c == 0)
    def _():
        qall_scr[...] = _mla_expand_q(qcat_ref[0], wexp_ref)
        m_scr[...] = jnp.full(m_scr.shape, NEG_INF, F32)
        acc_scr[...] = jnp.zeros(acc_scr.shape, F32)

    def update(s, vmat):
        m_new, alpha, pv = _mla_softmax_step(s, m_scr[...], vmat)
        acc_scr[...] = alpha * acc_scr[...] + pv
        m_scr[...] = m_new

    q = qall_scr[...]
    sub = min(MLA_SUB_PAGES, g_pages)
    m_run = m_scr[...]
    acc_run = acc_scr[...]
    for blk in range(g_pages // sub):
        rs = slice(blk * sub * PAGE_SIZE, (blk + 1) * sub * PAGE_SIZE)
        for g in range(blk * sub, (blk + 1) * sub):
            ccat_scr[g * PAGE_SIZE:(g + 1) * PAGE_SIZE, 0:KV_LORA] = cpages[g][...].astype(BF16)
            pcat_scr[0:ROPE_C, g * PAGE_SIZE:(g + 1) * PAGE_SIZE] = ppages[g][...].astype(BF16)
        s = _dot_nt(q[:, :KV_LORA], ccat_scr[rs, 0:KV_LORA]) + _dot(q[:, KV_LORA:], pcat_scr[:, rs])
        m_run, alpha, pv = _mla_softmax_step(s, m_run, ccat_scr[rs, :])
        acc_run = alpha * acc_run + pv
    m_scr[...] = m_run
    acc_scr[...] = acc_run

    @pl.when(c == nsteps - 1)
    def _():
        kn = knew_ref[0]
        kn = jnp.concatenate([kn[:, :KV_LORA], _ones_beyond_rope(kn[:, KV_LORA:])], axis=1)
        knp = jnp.concatenate([kn, jnp.zeros((LANES - t, KVPE_W), F32)], axis=0).astype(BF16)
        sn = _dot_nt(q, knp)
        r = lax.broadcasted_iota(I32, (H_C * t, LANES), 0)
        col = lax.broadcasted_iota(I32, (H_C * t, LANES), 1)
        sn = jnp.where(col <= (r & (t - 1)), sn, NEG_INF)
        update(sn, knp)
        acc = acc_scr[...]
        outs = [_mla_head_pair_out(acc[2 * j * t:(2 * j + 1) * t], acc[(2 * j + 1) * t:(2 * j + 2) * t], wuv_ref[j])
                for j in range(H_C // 2)]
        o_ref[0] = jnp.concatenate(outs, axis=1)


def _mla_sample(page_table, qcat3, kvpeb3, wexp, wuvp, cache_ckv, cache_kpe):
    db, t, _ = qcat3.shape
    n_pages = page_table.shape[1]
    g = min(G_MLA, n_pages)
    assert n_pages % g == 0
    rows = H_C * t
    grid_spec = pltpu.PrefetchScalarGridSpec(
        num_scalar_prefetch=1,
        grid=(db, n_pages // g),
        in_specs=[pl.BlockSpec((1, t, QCAT_W), lambda b, c, pt: (b, 0, 0)),
                  pl.BlockSpec((1, t, KVPE_W), lambda b, c, pt: (b, 0, 0)),
                  pl.BlockSpec(wexp.shape, lambda b, c, pt: (0, 0, 0)),
                  pl.BlockSpec(wuvp.shape, lambda b, c, pt: (0, 0, 0))]
                 + _page_specs(g, g, (PAGE_SIZE, KV_LORA)) + _page_specs(g, g, (ROPE_C, PAGE_SIZE)),
        out_specs=pl.BlockSpec((1, t, D_C), lambda b, c, pt: (b, 0, 0)),
        scratch_shapes=[pltpu.VMEM((rows, KVPE_W), BF16),
                        pltpu.VMEM((g * PAGE_SIZE, KVPE_W), BF16), pltpu.VMEM((LANES, g * PAGE_SIZE), BF16),
                        pltpu.VMEM((rows, LANES), F32), pltpu.VMEM((rows, KVPE_W), F32)],
    )
    return pl.pallas_call(
        functools.partial(_mla_sample_body, g),
        grid_spec=grid_spec,
        out_shape=jax.ShapeDtypeStruct((db, t, D_C), F32),
        compiler_params=_cparams(2),
        name="mla_sample",
    )(page_table, qcat3, kvpeb3, wexp, wuvp, *([cache_ckv] * g), *([cache_kpe] * g))


def _rope_cos_sin(pos):
    inv = ROPE_THETA ** (-jnp.arange(ROPE_HALF, dtype=F32) / ROPE_HALF)
    ang = pos.astype(F32)[:, None] * inv
    return jnp.cos(ang), jnp.sin(ang)


def _rope_lane_pattern(width, period, rot_start, limit):
    lane = np.arange(width)
    d = lane % period - rot_start
    inside = lane < limit
    x1 = inside & (d >= 0) & (d < ROPE_HALF)
    x2 = inside & (d >= ROPE_HALF) & (d < 2 * ROPE_HALF)
    return x1, x2


def _rope_tables(cos, sin, x1, x2, base):
    width = x1.shape[0]
    reps = width // ROPE_HALF
    cos_w = jnp.tile(cos, (1, reps))
    sin_w = jnp.tile(sin, (1, reps))
    cos_t = jnp.where(x1 | x2, cos_w, jnp.asarray(base, F32)[None, :])
    sin_t = jnp.where(x1, -sin_w, jnp.where(x2, sin_w, 0.0))
    return cos_t, sin_t


def _partner_columns(w, x1, x2):
    lane = np.arange(w.shape[1])
    src = lane + ROPE_HALF * x1 - ROPE_HALF * x2
    return jnp.where((x1 | x2)[None, :], w[:, src], 0.0)


def _tile_rows(tab, reps):
    return jnp.tile(tab, (reps, 1))


def kernel(x_prompt, x_sample, state_pool, cache_k_b, cache_v_b, cache_kidx_b, cache_ckv, cache_kpe, page_table, w_in_even, pool_w, pool_scale, w_out_even, rel_bias, w_in_odd, q_norm, w_q_b, kv_norm, w_uk, w_uv, w_out_odd, ln_g, ln_b):
    bsz, seq, _ = x_prompt.shape
    db, t, _ = x_sample.shape
    n_pages = page_table.shape[1]
    past = n_pages * PAGE_SIZE
    mp, ms = bsz * seq, db * t
    assert w_in_even.shape[0] == 1 and w_in_odd.shape[0] == 1 and t == 8

    cos_p, sin_p = _rope_cos_sin(jnp.arange(seq))
    cos_s, sin_s = _rope_cos_sin(past + jnp.arange(t))

    we = w_in_even[0]
    n_main = 6 * D_A
    wm = we[:, :n_main].astype(BF16)
    w_qi = we[:, n_main:n_main + H_IDX * D_IDX]
    w_ki = we[:, n_main + H_IDX * D_IDX:n_main + H_IDX * D_IDX + D_IDX]
    w_wi = we[:, n_main + H_IDX * D_IDX + D_IDX:]
    ws32 = jnp.concatenate([w_qi, w_ki, w_ki, w_wi, jnp.zeros((D_MODEL, IDX_W - IDX_WI - H_IDX), F32)], axis=1)
    x1e, x2e = _rope_lane_pattern(IDX_W, D_IDX, 0, IDX_WI)
    base_e = np.where(np.arange(IDX_W) < IDX_WI, 1.0, np.where(np.arange(IDX_W) < IDX_WI + H_IDX, H_IDX ** -0.5, 0.0))
    ws = ws32.astype(BF16)
    wr = _partner_columns(ws32, x1e, x2e).astype(BF16)
    pw = pool_w[0].astype(BF16)
    pscale = pool_scale[0][None, :]
    woe = w_out_even[0].astype(BF16)
    lng0, lnb0 = ln_g[0][None, :], ln_b[0][None, :]
    lng1, lnb1 = ln_g[1][None, :], ln_b[1][None, :]

    def even_tables(cos, sin, reps):
        ct, st = _rope_tables(cos, sin, x1e, x2e, base_e)
        return _tile_rows(ct, reps), _tile_rows(st, reps)

    xp2 = x_prompt.reshape(mp, D_MODEL)
    ct, st = even_tables(cos_p, sin_p, 1)
    u, ga, q, k_fm, v_fm, gb, idx, qb, kb, vx, kib = _even_proj(xp2, wm, ws, wr, ct, st, seq_len=seq)
    r3 = lambda a: a.reshape(bsz, seq, a.shape[-1])
    a_p = _pool_prompt(r3(u), pw, pscale).reshape(mp, D_A)
    o_p = _dsa_prompt(rel_bias, r3(qb), r3(idx), r3(kb), r3(vx), r3(kib)).reshape(mp, D_B)
    xp1 = _gate_out_ln(xp2, [(a_p, ga), (o_p, gb)], woe, lng0, lnb0, "even_out_prompt")
    pool_p = r3(u)[:, seq - POOL_STATE:][None]
    kb_p = jnp.transpose(k_fm.reshape(bsz, H_B, DH_B, seq), (0, 3, 1, 2))[None]
    vb_p = jnp.transpose(v_fm.reshape(bsz, H_B, DH_B, seq), (0, 3, 1, 2))[None]
    ki_p = r3(idx)[:, :, IDX_KI:IDX_KI + D_IDX][None]

    xs2 = x_sample.reshape(ms, D_MODEL)
    tm_s = min(TM_PROJ, ms)
    ct, st = even_tables(cos_s, sin_s, tm_s // t)
    u, ga, q, k, v, gb, idx, qb, kb, vx, kib = _even_proj(xs2, wm, ws, wr, ct, st)
    s3 = lambda a: a.reshape(db, t, a.shape[-1])
    u_s = s3(u)
    ext = jnp.concatenate([jnp.zeros((db, 16 - POOL_STATE, D_A), F32), state_pool[0], u_s], axis=1)
    a_s = _pool_sample(ext, pw, pscale)
    n_phys = cache_k_b.shape[1]
    kidx_t = jnp.transpose(cache_kidx_b, (0, 1, 3, 2))
    ck4 = jnp.transpose(cache_k_b, (0, 1, 3, 4, 2)).reshape(1, n_phys, D_B, PAGE_SIZE)
    cv4 = jnp.transpose(cache_v_b, (0, 1, 3, 4, 2)).reshape(1, n_phys, D_B, PAGE_SIZE)
    kpe_t = jnp.transpose(cache_kpe, (0, 1, 3, 2))
    sel = _dsa_select(page_table, s3(idx), kidx_t)
    o_s = _dsa_sample(page_table, rel_bias, s3(q), sel, s3(k), s3(v), ck4, cv4).reshape(ms, D_B)
    xs1 = _gate_out_ln(xs2, [(a_s, ga), (o_s, gb)], woe, lng0, lnb0, "even_out_sample")
    pool_s = ext[:, ext.shape[1] - POOL_STATE:][None]
    kb_s = k.reshape(1, db, t, H_B, DH_B)
    vb_s = v.reshape(1, db, t, H_B, DH_B)
    ki_s = s3(idx)[:, :, IDX_KI:IDX_KI + D_IDX][None]

    wo = w_in_odd[0]
    wa = wo[:, :Q_LORA].astype(BF16)
    wb32 = jnp.concatenate([wo[:, Q_LORA:Q_LORA + KV_LORA + ROPE_C],
                            jnp.zeros((D_MODEL, KVPE_W - KV_LORA - ROPE_C), F32)], axis=1)
    x1k, x2k = _rope_lane_pattern(KVPE_W, KVPE_W, KV_LORA, KVPE_W)
    wb = wb32.astype(BF16)
    wbr = _partner_columns(wb32, x1k, x2k).astype(BF16)
    wg = wo[:, Q_LORA + KV_LORA + ROPE_C:].astype(BF16)
    wqb = w_q_b[0]
    wq32 = jnp.concatenate([wqb, jnp.zeros((Q_LORA, H_C, LANES - NOPE_C - ROPE_C), F32)], axis=2)
    wq32 = wq32.reshape(Q_LORA, QCAT_W)
    x1q, x2q = _rope_lane_pattern(QCAT_W, LANES, NOPE_C, QCAT_W)
    wq = wq32.astype(BF16)
    wqr = _partner_columns(wq32, x1q, x2q).astype(BF16)
    qn = q_norm[0][None, :]
    kvn = kv_norm[0][None, :]
    base_k = np.zeros(LANES)
    base_q = np.where(np.arange(LANES) < NOPE_C, 1.0, 0.0)

    def odd_tables(cos, sin, reps):
        ck, sk = _rope_tables(cos, sin, x1k[KV_LORA:], x2k[KV_LORA:], base_k)
        cq, sq = _rope_tables(cos, sin, x1q[:LANES], x2q[:LANES], base_q)
        return (_tile_rows(jnp.concatenate([ck, cq], axis=1), reps),
                _tile_rows(jnp.concatenate([sk, sq], axis=1), reps))

    wuk = w_uk[0]
    wexp = jnp.zeros((H_C, LANES, KVPE_W), F32)
    wexp = wexp.at[:, :NOPE_C, :KV_LORA].set(jnp.transpose(wuk, (1, 2, 0)))
    wexp = wexp.at[:, NOPE_C:NOPE_C + ROPE_C, KV_LORA:KV_LORA + ROPE_C].set(jnp.eye(ROPE_C, dtype=F32)[None])
    wexp = wexp.astype(BF16)
    wuv = jnp.transpose(w_uv[0], (1, 0, 2))
    wuvp = jnp.zeros((H_C // 2, 2 * KV_LORA, 2 * DV_C), F32)
    wuvp = wuvp.at[:, :KV_LORA, :DV_C].set(wuv[0::2])
    wuvp = wuvp.at[:, KV_LORA:, DV_C:].set(wuv[1::2])
    wuvp = wuvp.astype(BF16)
    wk2 = jnp.zeros((KVPE_W, H_C, LANES), F32)
    wk2 = wk2.at[:KV_LORA, :, :NOPE_C].set(wuk)
    wk2 = wk2.at[KV_LORA:KV_LORA + ROPE_C, :, NOPE_C:NOPE_C + ROPE_C].set(
        jnp.broadcast_to(jnp.eye(ROPE_C, dtype=F32)[:, None, :], (ROPE_C, H_C, ROPE_C)))
    wk2 = wk2.reshape(KVPE_W, QCAT_W).astype(BF16)
    wv2 = jnp.zeros((KV_LORA, H_C // 2, 2, LANES), F32)
    wv2 = wv2.at[:, :, 0, :DV_C].set(w_uv[0][:, 0::2])
    wv2 = wv2.at[:, :, 1, DV_C:].set(w_uv[0][:, 1::2])
    wv2 = wv2.reshape(KV_LORA, QCAT_W).astype(BF16)
    woo = w_out_odd[0].astype(BF16)

    ct, st = odd_tables(cos_p, sin_p, 1)
    qcat, kvpe, kvpeb, g1 = _odd_proj(xp1, wa, wb, wbr, wg, wq, wqr, qn, kvn, ct, st, BF16)
    o1 = _mla_prompt(r3(qcat), r3(kvpeb), wk2, wv2).reshape(mp, D_C)
    y_p = _gate_out_ln(xp1, [(o1, g1)], woo, lng1, lnb1, "odd_out_prompt").reshape(bsz, seq, D_MODEL)
    ckv_p = r3(kvpe)[:, :, :KV_LORA][None]
    kpe_p = r3(kvpe)[:, :, KV_LORA:KV_LORA + ROPE_C][None]

    ct, st = odd_tables(cos_s, sin_s, tm_s // t)
    qcat, kvpe, kvpeb, g1 = _odd_proj(xs1, wa, wb, wbr, wg, wq, wqr, qn, kvn, ct, st, F32)
    o1 = _mla_sample(page_table, s3(qcat), s3(kvpe), wexp, wuvp, cache_ckv, kpe_t).reshape(ms, D_C)
    y_s = _gate_out_ln(xs1, [(o1, g1)], woo, lng1, lnb1, "odd_out_sample").reshape(db, t, D_MODEL)
    ckv_s = s3(kvpe)[:, :, :KV_LORA][None]
    kpe_s = s3(kvpe)[:, :, KV_LORA:KV_LORA + ROPE_C][None]

    return (y_p, y_s, pool_p, pool_s, kb_p, kb_s, vb_p, vb_s, ki_p, ki_s, ckv_p, ckv_s, kpe_p, kpe_s)
```

```python
import functools
import math

import numpy as np
import jax
import jax.numpy as jnp
from jax import lax
from jax.experimental import pallas as pl
from jax.experimental.pallas import tpu as pltpu

F32 = jnp.float32
BF16 = jnp.bfloat16
I32 = jnp.int32

D_MODEL = 1024
DEPTH = 2
PAGE_SIZE = 128
D_A = D_MODEL // 2
POOL_WINDOWS = (2, 4, 8, 16)
G_A = D_A // len(POOL_WINDOWS)
POOL_STATE = max(POOL_WINDOWS) - 1
H_B = 8
DH_B = 64
D_B = H_B * DH_B
H_IDX = 4
D_IDX = 64
TOPK_MAX = 256
N_BUCKETS = 32
MAX_DISTANCE = 128
H_C = 16
Q_LORA = 384
KV_LORA = 256
NOPE_C = 64
ROPE_C = 32
DV_C = 64
D_C = H_C * DV_C
MLA_SCALE = (NOPE_C + ROPE_C) ** -0.5
MLA_EXP2_SCALE = MLA_SCALE * math.log2(math.e)
ROPE_THETA = 10000.0
ROPE_HALF = 16
LN_EPS = 1e-5
RMS_EPS = 1e-6
ALPHA = (2 * DEPTH) ** 0.25

LANES = 128
IDX_W = 512
IDX_KI = 256
IDX_WI = 384
KVPE_W = 384
MLA_DEN_LANE = 320
QCAT_W = H_C * LANES

VMEM_LIMIT = 48 * 1024 * 1024

TM_PROJ = 256
TS_POOL = 512
G_POOL = 32
TM_OUT = 256
TQ = 128
TQ_MLA = 256
DSA_CHUNKS_PER_VARIANT = 4
R_PICK = 64
G_DSA = 16
G_MLA = 32
MLA_SUB_PAGES = 8

NEG_INF = float("-inf")
INT_MIN = -(2 ** 31)


def _cparams(n_axes):
    return pltpu.CompilerParams(dimension_semantics=("arbitrary",) * n_axes,
                                vmem_limit_bytes=VMEM_LIMIT)


def _dot(a, b):
    return jnp.dot(a, b, preferred_element_type=F32)


def _dot_nt(a, b):
    return lax.dot_general(a, b, (((1,), (1,)), ((), ())), preferred_element_type=F32)


def _t5_breaks():
    max_d = 2 * MAX_DISTANCE
    d = np.arange(max_d + 1)
    me = N_BUCKETS // 2
    large = me + (np.log(np.maximum(d, 1) / me) / math.log(MAX_DISTANCE / me) * (N_BUCKETS - me)).astype(np.int64)
    large = np.minimum(large, N_BUCKETS - 1)
    bucket = np.where(d < me, d, large)
    assert np.all(bucket[MAX_DISTANCE:] == N_BUCKETS - 1)
    return int(bucket[0]), [(int(i), int(bucket[i])) for i in range(1, max_d + 1) if bucket[i] != bucket[i - 1]]


_T5_FIRST, _T5_BREAKS = _t5_breaks()


def _bias_of_dist(dist, relb_ref, h):
    val = jnp.full(dist.shape, relb_ref[_T5_FIRST, h], F32)
    for p, bk in _T5_BREAKS:
        val = jnp.where(dist >= p, relb_ref[bk, h], val)
    return val


def _sortable_key(score):
    score = jnp.where(score == 0.0, 0.0, score)
    bits = lax.bitcast_convert_type(score, I32)
    return jnp.where(bits < 0, bits ^ jnp.int32(0x7FFFFFFF), bits)


def _kth_largest_key(keys_ref, nc, rows, k, splits):
    rs = rows // splits

    def count_ge(part, cand):
        blk = keys_ref[0:nc, part * rs:(part + 1) * rs, :]
        hit = jnp.where(blk >= cand[None], 1.0, 0.0)
        return jnp.sum(jnp.sum(hit, axis=0), axis=1, keepdims=True)

    def body(it, ts):
        bit = jnp.left_shift(jnp.int32(1), jnp.int32(31) - it)
        out = []
        for part, t in enumerate(ts):
            cand = t + bit
            out.append(jnp.where(count_ge(part, cand) >= k, cand, t))
        return tuple(out)

    init = tuple(jnp.full((rs, 1), INT_MIN, I32) for _ in range(splits))
    ts = lax.fori_loop(0, 32, body, init)
    return ts[0] if splits == 1 else jnp.concatenate(ts, axis=0)


def _select_topk(keys_ref, nc, rows, topk, splits):
    t = _kth_largest_key(keys_ref, nc, rows, topk, splits)[None]
    keys = keys_ref[0:nc]
    gt = keys > t
    eqm = keys == t
    cnt_gt = jnp.sum(jnp.sum(jnp.where(gt, 1.0, 0.0), axis=0), axis=1, keepdims=True)
    need = (topk - cnt_gt)[None]
    r = lax.broadcasted_iota(I32, (LANES, 2 * LANES), 0)
    c = lax.broadcasted_iota(I32, (LANES, 2 * LANES), 1)
    su_ones = jnp.where((r < c) | (c >= LANES), 1.0, 0.0).astype(BF16)
    eq = jnp.where(eqm, 1.0, 0.0).reshape(nc * rows, LANES).astype(BF16)
    pt = _dot(eq, su_ones)
    pre = pt[:, :LANES].reshape(nc, rows, LANES)
    tot = pt[:, LANES:].reshape(nc, rows, LANES)
    offs = []
    off = jnp.zeros((rows, LANES), F32)
    for cc in range(nc):
        offs.append(off)
        off = off + tot[cc]
    before = pre + jnp.stack(offs, axis=0)
    return gt | (eqm & (before < need))


def _silu_gate(v, g):
    return v * (g * (1.0 / (1.0 + jnp.exp(-g))))


def _even_proj_body(kv_feature_major, x_ref, wm_ref, ws_ref, wr_ref, cos_ref, sin_ref,
                    u_ref, ga_ref, q_ref, k_ref, v_ref, gb_ref, idx_ref, qb_ref, kb_ref, vx_ref, kib_ref):
    xb = x_ref[...].astype(BF16)

    def mm(n):
        return _dot(xb, wm_ref[:, n * D_A:(n + 1) * D_A])

    u_ref[...] = mm(0)
    ga_ref[...] = mm(1)
    q = mm(2)
    q_ref[...] = q
    qb_ref[...] = (q * DH_B ** -0.5).astype(BF16)
    k = mm(3)
    kb_ref[...] = k.astype(BF16)
    v = mm(4)
    if kv_feature_major:
        k_ref[0] = k.T
        v_ref[0] = v.T
    else:
        k_ref[...] = k
        v_ref[...] = v
    ones = jnp.ones((v.shape[0], LANES), BF16)
    for j in range(H_B // 2):
        vx_ref[:, 2 * j * LANES:(2 * j + 1) * LANES] = v[:, j * LANES:(j + 1) * LANES].astype(BF16)
        vx_ref[:, (2 * j + 1) * LANES:(2 * j + 2) * LANES] = ones
    gb_ref[...] = mm(5)
    idx = _dot(xb, ws_ref[...]) * cos_ref[...] + _dot(xb, wr_ref[...]) * sin_ref[...]
    idx_ref[...] = idx
    kib_ref[...] = idx[:, IDX_KI:IDX_KI + LANES].astype(BF16)


def _even_proj(x2d, wm, ws, wr, cos_t, sin_t, seq_len=None):
    m = x2d.shape[0]
    tm = min(TM_PROJ, m)
    nt = cos_t.shape[0] // tm
    row = lambda w: pl.BlockSpec((tm, w), lambda i: (i, 0))
    full = lambda a: pl.BlockSpec(a.shape, lambda i: (0, 0))
    tab = pl.BlockSpec((tm, IDX_W), lambda i: (i % nt, 0))
    f = lambda w, dt: jax.ShapeDtypeStruct((m, w), dt)
    if seq_len is None:
        kv_spec, kv_shape = row(D_B), f(D_B, F32)
    else:
        spt = seq_len // tm
        kv_spec = pl.BlockSpec((1, D_B, tm), lambda i: (i // spt, 0, i % spt))
        kv_shape = jax.ShapeDtypeStruct((m // seq_len, D_B, seq_len), F32)
    return pl.pallas_call(
        functools.partial(_even_proj_body, seq_len is not None),
        grid=(m // tm,),
        in_specs=[row(D_MODEL), full(wm), full(ws), full(wr), tab, tab],
        out_specs=[row(D_A)] * 3 + [kv_spec] * 2 + [row(D_A), row(IDX_W), row(D_B), row(D_B), row(2 * D_B), row(LANES)],
        out_shape=[f(D_A, F32)] * 3 + [kv_shape] * 2 + [f(D_A, F32), f(IDX_W, F32), f(D_B, BF16), f(D_B, BF16),
                                                         f(2 * D_B, BF16), f(LANES, BF16)],
        compiler_params=_cparams(1),
        name="even_proj",
    )(x2d, wm, ws, wr, cos_t, sin_t)


def _pool_mix_group(win_sum, cur, inv_cnt, pw_ref, scale_ref, g):
    pooled = win_sum * inv_cnt - cur
    mixed = _dot(pooled.astype(BF16), pw_ref[g])
    return mixed * scale_ref[:, g * G_A:(g + 1) * G_A]


def _pool_prompt_body(u_ref, halo_ref, pw_ref, scale_ref, a_ref, ext_ref):
    s = pl.program_id(1)
    ts = u_ref.shape[1]
    halo = halo_ref[0]
    ext_ref[0:16, :] = jnp.where(s == 0, 0.0, halo)
    ext_ref[16:, :] = u_ref[0]
    pos = s * ts + lax.broadcasted_iota(I32, (ts, 1), 0)
    for g, w in enumerate(POOL_WINDOWS):
        sl = pl.ds(g * G_A, G_A)
        acc = ext_ref[pl.ds(16, ts), sl]
        for kk in range(1, w):
            acc = acc + ext_ref[pl.ds(16 - kk, ts), sl]
        cnt = jnp.minimum(pos + 1, w).astype(F32)
        a_ref[0, :, g * G_A:(g + 1) * G_A] = _pool_mix_group(
            acc, ext_ref[pl.ds(16, ts), sl], 1.0 / cnt, pw_ref, scale_ref, g)


def _pool_prompt(u3, pw, scale):
    b, s, _ = u3.shape
    ts = min(TS_POOL, s)
    hb = ts // 16
    return pl.pallas_call(
        _pool_prompt_body,
        grid=(b, s // ts),
        in_specs=[pl.BlockSpec((1, ts, D_A), lambda i, j: (i, j, 0)),
                  pl.BlockSpec((1, 16, D_A), lambda i, j: (i, jnp.maximum(j * hb - 1, 0), 0)),
                  pl.BlockSpec(pw.shape, lambda i, j: (0, 0, 0)),
                  pl.BlockSpec(scale.shape, lambda i, j: (0, 0))],
        out_specs=pl.BlockSpec((1, ts, D_A), lambda i, j: (i, j, 0)),
        out_shape=jax.ShapeDtypeStruct((b, s, D_A), F32),
        scratch_shapes=[pltpu.VMEM((ts + 16, D_A), F32)],
        compiler_params=_cparams(2),
        name="pool_prompt",
    )(u3, u3, pw, scale)


def _pool_sample_body(ext_ref, pw_ref, scale_ref, a_ref):
    gs = ext_ref.shape[0]
    t = ext_ref.shape[1] - 16
    for g, w in enumerate(POOL_WINDOWS):
        sl = pl.ds(g * G_A, G_A)
        cur = ext_ref[:, pl.ds(16, t), sl]
        acc = cur
        for kk in range(1, w):
            acc = acc + ext_ref[:, pl.ds(16 - kk, t), sl]
        acc = acc.reshape(gs * t, G_A)
        cur = cur.reshape(gs * t, G_A)
        a_ref[:, g * G_A:(g + 1) * G_A] = _pool_mix_group(acc, cur, 1.0 / w, pw_ref, scale_ref, g)


def _pool_sample(ext, pw, scale):
    db, e, _ = ext.shape
    t = e - 16
    gs = min(G_POOL, db)
    return pl.pallas_call(
        _pool_sample_body,
        grid=(db // gs,),
        in_specs=[pl.BlockSpec((gs, e, D_A), lambda i: (i, 0, 0)),
                  pl.BlockSpec(pw.shape, lambda i: (0, 0, 0)),
                  pl.BlockSpec(scale.shape, lambda i: (0, 0))],
        out_specs=pl.BlockSpec((gs * t, D_A), lambda i: (i, 0)),
        out_shape=jax.ShapeDtypeStruct((db * t, D_A), F32),
        compiler_params=_cparams(1),
        name="pool_sample",
    )(ext, pw, scale)


def _gate_out_ln_body(nparts, x_ref, *refs):
    parts = refs[:2 * nparts]
    w_ref, g_ref, b_ref, y_ref = refs[2 * nparts:]
    hs = [_silu_gate(parts[2 * p][...], parts[2 * p + 1][...]).astype(BF16) for p in range(nparts)]
    h = hs[0] if nparts == 1 else jnp.concatenate(hs, axis=1)
    z = ALPHA * x_ref[...] + _dot(h, w_ref[...])
    mu = jnp.mean(z, axis=1, keepdims=True)
    zc = z - mu
    var = jnp.mean(zc * zc, axis=1, keepdims=True)
    y_ref[...] = zc * lax.rsqrt(var + LN_EPS) * g_ref[...] + b_ref[...]


def _gate_out_ln(x2d, parts, w, ln_g, ln_b, name):
    m = x2d.shape[0]
    tm = min(TM_OUT, m)
    row = lambda a: pl.BlockSpec((tm, a.shape[1]), lambda i: (i, 0))
    full = lambda a: pl.BlockSpec(a.shape, lambda i: (0, 0))
    flat = [a for pair in parts for a in pair]
    return pl.pallas_call(
        functools.partial(_gate_out_ln_body, len(parts)),
        grid=(m // tm,),
        in_specs=[row(x2d)] + [row(a) for a in flat] + [full(w), full(ln_g), full(ln_b)],
        out_specs=row(x2d),
        out_shape=jax.ShapeDtypeStruct(x2d.shape, F32),
        compiler_params=_cparams(1),
        name=name,
    )(x2d, *flat, w, ln_g, ln_b)


def _dsa_prompt_tile(nc, topk, i, qb_ref, idxq_ref, kb_ref, vx_ref, kib_ref, o_ref,
                     keys_scr, madd_scr, bias_scr):
    tq = qb_ref.shape[1]
    w = nc * LANES
    lane = lax.broadcasted_iota(I32, (tq, LANES), 1)
    lo = lane < DH_B
    idxq = idxq_ref[0]
    kib = kib_ref[0, 0:w, :]

    score = None
    for j in range(H_IDX // 2):
        pair = idxq[:, j * LANES:(j + 1) * LANES]
        for half in range(2):
            h = 2 * j + half
            lhs = jnp.where(lo if half == 0 else ~lo, pair, 0.0).astype(BF16)
            wh = idxq[:, IDX_WI + h:IDX_WI + h + 1] * D_IDX ** -0.5
            term = wh * jnp.maximum(_dot_nt(lhs, kib), 0.0)
            score = term if score is None else score + term
    qrow = i * tq + lax.broadcasted_iota(I32, (tq, LANES), 0)
    valid = [(c * LANES + lane) <= qrow for c in range(nc)]
    for c in range(nc):
        keys_scr[c] = _sortable_key(jnp.where(valid[c], score[:, c * LANES:(c + 1) * LANES], NEG_INF))
    sel = _select_topk(keys_scr, nc, tq, topk, 2)
    for c in range(nc):
        madd_scr[:, c * LANES:(c + 1) * LANES] = jnp.where(sel[c] & valid[c], 0.0, NEG_INF)

    for j in range(H_B // 2):
        qpair = qb_ref[0, :, j * LANES:(j + 1) * LANES]
        kpair = kb_ref[0, 0:w, j * LANES:(j + 1) * LANES]
        vext = vx_ref[0, 0:w, 2 * j * LANES:(2 * j + 2) * LANES]
        opair = None
        for half in range(2):
            h = 2 * j + half
            hm = lo if half == 0 else ~lo
            lhs = jnp.where(hm, qpair, jnp.zeros_like(qpair))
            logit = _dot_nt(lhs, kpair)
            pieces = []
            for c in range(nc):
                lc = logit[:, c * LANES:(c + 1) * LANES] + madd_scr[:, c * LANES:(c + 1) * LANES]
                if c >= nc - DSA_CHUNKS_PER_VARIANT - 1:
                    lc = lc + bias_scr[h, jnp.clip(i - c, 0, 2)]
                pieces.append(lc)
            logit = jnp.concatenate(pieces, axis=1)
            p = jnp.exp(logit - jnp.max(logit, axis=1, keepdims=True))
            pv = _dot(p.astype(BF16), vext)
            ov = pv[:, :LANES] / pv[:, LANES:]
            opair = ov if opair is None else jnp.where(hm, ov, opair)
        o_ref[0, :, j * LANES:(j + 1) * LANES] = opair


def _dsa_prompt_body(topk, relb_ref, qb_ref, idxq_ref, kb_ref, vx_ref, kib_ref, o_ref,
                     keys_scr, madd_scr, bias_scr):
    b = pl.program_id(0)
    i = pl.program_id(1)
    ncs = kb_ref.shape[1] // LANES

    @pl.when((b == 0) & (i == 0))
    def _():
        r = lax.broadcasted_iota(I32, (LANES, LANES), 0)
        c = lax.broadcasted_iota(I32, (LANES, LANES), 1)
        for h in range(H_B):
            far = relb_ref[N_BUCKETS - 1, h]
            bias_scr[h, 0] = _bias_of_dist(r - c, relb_ref, h) - far
            bias_scr[h, 1] = _bias_of_dist(LANES + r - c, relb_ref, h) - far
            bias_scr[h, 2] = jnp.zeros((LANES, LANES), F32)

    nvar = -(-ncs // DSA_CHUNKS_PER_VARIANT)
    for var in range(nvar):
        nc = min((var + 1) * DSA_CHUNKS_PER_VARIANT, ncs)

        @pl.when(i // DSA_CHUNKS_PER_VARIANT == var)
        def _(nc=nc):
            _dsa_prompt_tile(nc, topk, i, qb_ref, idxq_ref, kb_ref, vx_ref, kib_ref, o_ref,
                             keys_scr, madd_scr, bias_scr)


def _dsa_prompt(rel_bias, qb3, idx3, kb3, vx3, kib3):
    b, s, _ = qb3.shape
    assert s % TQ == 0 and TQ == LANES
    topk = min(TOPK_MAX, s // 4)
    qspec = lambda w: pl.BlockSpec((1, TQ, w), lambda i, j: (i, j, 0))
    kspec = lambda w: pl.BlockSpec((1, s, w), lambda i, j: (i, 0, 0))
    return pl.pallas_call(
        functools.partial(_dsa_prompt_body, topk),
        grid=(b, s // TQ),
        in_specs=[pl.BlockSpec(memory_space=pltpu.SMEM),
                  qspec(D_B), qspec(IDX_W), kspec(D_B), kspec(2 * D_B), kspec(LANES)],
        out_specs=qspec(D_B),
        out_shape=jax.ShapeDtypeStruct((b, s, D_B), F32),
        scratch_shapes=[pltpu.VMEM((s // LANES, TQ, LANES), I32), pltpu.VMEM((TQ, s), F32),
                        pltpu.VMEM((H_B, 3, LANES, LANES), F32)],
        compiler_params=_cparams(2),
        name="dsa_prompt",
    )(rel_bias, qb3, idx3, kb3, vx3, kib3)


class _PagedPrefetch:
    def __init__(self, pt_ref, caches, pages_per_step, steps_per_seq):
        self.pt_ref, self.caches = pt_ref, caches
        self.n, self.steps = pages_per_step, steps_per_seq

    def _copies(self, seq, group, slot):
        out = []
        for p in range(self.n):
            page = self.pt_ref[seq, group * self.n + p]
            for hbm, buf, sem in self.caches:
                out.append(pltpu.make_async_copy(hbm.at[0, page], buf.at[slot, p], sem.at[slot]))
        return out

    def advance(self, seq, group, n_seq):
        step = seq * self.steps + group
        slot = step & 1

        @pl.when(step == 0)
        def _():
            for cp in self._copies(seq, group, slot):
                cp.start()

        wrap = group + 1 == self.steps
        nseq = jnp.where(wrap, seq + 1, seq)
        ngroup = jnp.where(wrap, 0, group + 1)

        @pl.when(nseq < n_seq)
        def _():
            for cp in self._copies(nseq, ngroup, 1 - slot):
                cp.start()

        for cp in self._copies(seq, group, slot):
            cp.wait()
        return slot


def _dsa_score_body(n_pages, pt_ref, idx_ref, kidx_hbm, keys_ref, knew_ref, kbuf, ksem, kcat_scr):
    b = pl.program_id(0)
    slot = _PagedPrefetch(pt_ref, [(kidx_hbm, kbuf, ksem)], n_pages, 1).advance(b, 0, pl.num_programs(0))
    t = idx_ref.shape[1]
    idx = idx_ref[0]
    qrows = jnp.concatenate([idx[:, h * D_IDX:(h + 1) * D_IDX] for h in range(H_IDX)], axis=0).astype(BF16)

    def score_of(qk):
        sh = jnp.maximum(qk * D_IDX ** -0.5, 0.0)
        out = None
        for h in range(H_IDX):
            term = idx[:, IDX_WI + h:IDX_WI + h + 1] * sh[h * t:(h + 1) * t]
            out = term if out is None else out + term
        return out

    for g in range(n_pages):
        kcat_scr[:, g * PAGE_SIZE:(g + 1) * PAGE_SIZE] = kbuf[slot, g].astype(BF16)
    score = score_of(_dot(qrows, kcat_scr[...]))
    for g in range(n_pages):
        keys_ref[g] = _sortable_key(score[:, g * PAGE_SIZE:(g + 1) * PAGE_SIZE])

    knew = jnp.concatenate([idx[:, IDX_KI:IDX_KI + D_IDX], jnp.zeros((LANES - t, D_IDX), F32)], axis=0)
    sn = score_of(_dot_nt(qrows, knew.astype(BF16)))
    tq = lax.broadcasted_iota(I32, (t, LANES), 0)
    tk = lax.broadcasted_iota(I32, (t, LANES), 1)
    knew_ref[...] = _sortable_key(jnp.where(tk <= tq, sn, NEG_INF))


def _dsa_pick_body(topk, t, keys_ref, knew_ref, sel_ref, keys_scr):
    nc = keys_ref.shape[0]
    rows = keys_ref.shape[1]
    keys_scr[0:nc] = keys_ref[...]
    keys_scr[nc] = knew_ref[...]
    sel = _select_topk(keys_scr, nc + 1, rows, topk, 1)
    sel_ref[0:nc] = jnp.where(sel[0:nc], 1.0, 0.0)
    tq = lax.broadcasted_iota(I32, (rows, LANES), 0) & (t - 1)
    tk = lax.broadcasted_iota(I32, (rows, LANES), 1)
    sel_ref[nc] = jnp.where(sel[nc] & (tk <= tq), 1.0, 0.0)


def _dsa_select(page_table, idx3, cache_kidx):
    db, t, _ = idx3.shape
    n_pages = page_table.shape[1]
    assert t & (t - 1) == 0
    topk = min(TOPK_MAX, (n_pages * PAGE_SIZE + t) // 4)
    grid_spec = pltpu.PrefetchScalarGridSpec(
        num_scalar_prefetch=1,
        grid=(db,),
        in_specs=[pl.BlockSpec((1, t, IDX_W), lambda b, pt: (b, 0, 0)), pl.BlockSpec(memory_space=pl.ANY)],
        out_specs=[pl.BlockSpec((n_pages, t, LANES), lambda b, pt: (0, b, 0)),
                   pl.BlockSpec((t, LANES), lambda b, pt: (b, 0))],
        scratch_shapes=[pltpu.VMEM((2, n_pages, D_IDX, PAGE_SIZE), F32), pltpu.SemaphoreType.DMA((2,)),
                        pltpu.VMEM((D_IDX, n_pages * PAGE_SIZE), BF16)],
    )
    keys, knew = pl.pallas_call(
        functools.partial(_dsa_score_body, n_pages),
        grid_spec=grid_spec,
        out_shape=[jax.ShapeDtypeStruct((n_pages, db * t, LANES), I32),
                   jax.ShapeDtypeStruct((db * t, LANES), I32)],
        compiler_params=_cparams(1),
        name="dsa_score",
    )(page_table, idx3, cache_kidx)
    rows = min(R_PICK, db * t)
    return pl.pallas_call(
        functools.partial(_dsa_pick_body, topk, t),
        grid=(db * t // rows,),
        in_specs=[pl.BlockSpec((n_pages, rows, LANES), lambda r: (0, r, 0)),
                  pl.BlockSpec((rows, LANES), lambda r: (r, 0))],
        out_specs=pl.BlockSpec((n_pages + 1, rows, LANES), lambda r: (0, r, 0)),
        out_shape=jax.ShapeDtypeStruct((n_pages + 1, db * t, LANES), F32),
        scratch_shapes=[pltpu.VMEM((n_pages + 1, rows, LANES), I32)],
        compiler_params=_cparams(1),
        name="dsa_pick",
    )(keys, knew)


def _dsa_sample_body(g_pages, pt_ref, relb_ref, qb_ref, sel_ref, knew_ref, vnew_ref, k_hbm, v_hbm, o_ref,
                     kbuf, vbuf, ksem, vsem, qbd_scr, kcat_scr, vcat_scr, bias_scr, m_scr, l_scr, acc_scr):
    b = pl.program_id(0)
    c = pl.program_id(1)
    nsteps = pl.num_programs(1)
    t = qb_ref.shape[1]
    slot = _PagedPrefetch(pt_ref, [(k_hbm, kbuf, ksem), (v_hbm, vbuf, vsem)], g_pages, nsteps).advance(
        b, c, pl.num_programs(0))

    @pl.when((b == 0) & (c == 0))
    def _():
        tq = lax.broadcasted_iota(I32, (t, LANES), 0)
        col = lax.broadcasted_iota(I32, (t, LANES), 1)
        for h in range(H_B):
            rs = slice(h * t, (h + 1) * t)
            bias_scr[0, rs, :] = jnp.full((t, LANES), relb_ref[N_BUCKETS - 1, h], F32)
            bias_scr[1, rs, :] = _bias_of_dist(tq + PAGE_SIZE - col, relb_ref, h)
            bias_scr[2, rs, :] = _bias_of_dist(tq - col, relb_ref, h)

    @pl.when(c == 0)
    def _():
        q = qb_ref[0] * DH_B ** -0.5
        lane = lax.broadcasted_iota(I32, (t, D_B), 1)
        qbd = [jnp.where((lane >= h * DH_B) & (lane < (h + 1) * DH_B), q, 0.0) for h in range(H_B)]
        qbd_scr[...] = jnp.concatenate(qbd, axis=0).astype(BF16)
        m_scr[...] = jnp.full(m_scr.shape, NEG_INF, F32)
        l_scr[...] = jnp.zeros(l_scr.shape, F32)
        acc_scr[...] = jnp.zeros(acc_scr.shape, F32)

    def update(logit, sel, vmat, v_key_minor):
        selr = jnp.concatenate([sel] * H_B, axis=0) > 0.5
        lm = jnp.where(selr, logit, NEG_INF)
        m_old = m_scr[...]
        m_new = jnp.maximum(m_old, jnp.max(lm, axis=1, keepdims=True))
        m_safe = jnp.where(m_new == NEG_INF, 0.0, m_new)
        alpha = jnp.exp(m_old - m_safe)
        p = jnp.exp(lm - m_safe)
        l_scr[...] = alpha * l_scr[...] + jnp.sum(p, axis=1, keepdims=True)
        pv = _dot_nt(p.astype(BF16), vmat) if v_key_minor else _dot(p.astype(BF16), vmat)
        acc_scr[...] = alpha * acc_scr[...] + pv
        m_scr[...] = m_new

    for g in range(g_pages):
        kcat_scr[:, g * PAGE_SIZE:(g + 1) * PAGE_SIZE] = kbuf[slot, g].astype(BF16)
        vcat_scr[:, g * PAGE_SIZE:(g + 1) * PAGE_SIZE] = vbuf[slot, g].astype(BF16)
    logit = _dot(qbd_scr[...], kcat_scr[...])
    far = bias_scr[0]
    last = jnp.where(c == nsteps - 1, bias_scr[1], far)
    bias = jnp.concatenate([far] * (g_pages - 1) + [last], axis=1)
    sel = jnp.concatenate([sel_ref[c * g_pages + g] for g in range(g_pages)], axis=1)
    update(logit + bias, sel, vcat_scr[...], True)

    @pl.when(c == nsteps - 1)
    def _():
        pad = jnp.zeros((LANES - t, D_B), F32)
        knp = jnp.concatenate([knew_ref[0], pad], axis=0).astype(BF16)
        vnp = jnp.concatenate([vnew_ref[0], pad], axis=0).astype(BF16)
        ln = _dot_nt(qbd_scr[...], knp) + bias_scr[2]
        update(ln, sel_ref[sel_ref.shape[0] - 1], vnp, False)
        out = acc_scr[...] / l_scr[...]
        lane = lax.broadcasted_iota(I32, (t, D_B), 1)
        o = jnp.zeros((t, D_B), F32)
        for h in range(H_B):
            o = jnp.where((lane >= h * DH_B) & (lane < (h + 1) * DH_B), out[h * t:(h + 1) * t], o)
        o_ref[0] = o


def _dsa_sample(page_table, rel_bias, qb3, sel, k3, v3, cache_k, cache_v):
    db, t, _ = qb3.shape
    n_pages = page_table.shape[1]
    g = min(G_DSA, n_pages)
    assert n_pages % g == 0
    rows = H_B * t
    seq = lambda w: pl.BlockSpec((1, t, w), lambda b, c, pt: (b, 0, 0))
    sel_spec = pl.BlockSpec((sel.shape[0], t, LANES), lambda b, c, pt: (0, b, 0))
    grid_spec = pltpu.PrefetchScalarGridSpec(
        num_scalar_prefetch=1,
        grid=(db, n_pages // g),
        in_specs=[pl.BlockSpec(memory_space=pltpu.SMEM), seq(D_B), sel_spec, seq(D_B), seq(D_B),
                  pl.BlockSpec(memory_space=pl.ANY), pl.BlockSpec(memory_space=pl.ANY)],
        out_specs=seq(D_B),
        scratch_shapes=[pltpu.VMEM((2, g, D_B, PAGE_SIZE), F32), pltpu.VMEM((2, g, D_B, PAGE_SIZE), F32),
                        pltpu.SemaphoreType.DMA((2,)), pltpu.SemaphoreType.DMA((2,)),
                        pltpu.VMEM((rows, D_B), BF16),
                        pltpu.VMEM((D_B, g * PAGE_SIZE), BF16), pltpu.VMEM((D_B, g * PAGE_SIZE), BF16),
                        pltpu.VMEM((3, rows, LANES), F32),
                        pltpu.VMEM((rows, 1), F32), pltpu.VMEM((rows, 1), F32), pltpu.VMEM((rows, D_B), F32)],
    )
    return pl.pallas_call(
        functools.partial(_dsa_sample_body, g),
        grid_spec=grid_spec,
        out_shape=jax.ShapeDtypeStruct((db, t, D_B), F32),
        compiler_params=_cparams(2),
        name="dsa_sample",
    )(page_table, rel_bias, qb3, sel, k3, v3, cache_k, cache_v)


def _ones_beyond_rope(kpe_slab):
    lane = lax.broadcasted_iota(I32, kpe_slab.shape, 1)
    return jnp.where(lane < ROPE_C, kpe_slab, 1.0)


def _rms(x, g):
    return x * lax.rsqrt(jnp.mean(x * x, axis=1, keepdims=True) + RMS_EPS) * g


def _odd_proj_body(x_ref, wa_ref, wb_ref, wbr_ref, wg_ref, wq_ref, wqr_ref, qn_ref, kvn_ref, cos_ref, sin_ref,
                   qcat_ref, kvpe_ref, kvpeb_ref, g_ref):
    xb = x_ref[...].astype(BF16)
    cos_k, sin_k = cos_ref[:, :LANES], sin_ref[:, :LANES]
    cos_q, sin_q = cos_ref[:, LANES:], sin_ref[:, LANES:]
    g_ref[...] = _dot(xb, wg_ref[...])
    kv = _dot(xb, wb_ref[...])
    kvr = _dot(xb, wbr_ref[...])
    ckvn = _rms(kv[:, :KV_LORA], kvn_ref[...])
    kpe = kv[:, KV_LORA:] * cos_k + kvr[:, KV_LORA:] * sin_k
    kvpe = jnp.concatenate([ckvn, kpe], axis=1)
    kvpe_ref[...] = kvpe
    kvpeb_ref[...] = kvpe.astype(BF16)
    cqn = _rms(_dot(xb, wa_ref[...]), qn_ref[...]).astype(BF16)
    qc = _dot(cqn, wq_ref[...])
    qr = _dot(cqn, wqr_ref[...])
    for h in range(H_C):
        sl = slice(h * LANES, (h + 1) * LANES)
        qcat_ref[:, sl] = (qc[:, sl] * cos_q + qr[:, sl] * sin_q).astype(qcat_ref.dtype)


def _odd_proj(x2d, wa, wb, wbr, wg, wq, wqr, qn, kvn, cos_t, sin_t, qcat_dtype):
    m = x2d.shape[0]
    tm = min(TM_PROJ, m)
    nt = cos_t.shape[0] // tm
    row = lambda w: pl.BlockSpec((tm, w), lambda i: (i, 0))
    full = lambda a: pl.BlockSpec(a.shape, lambda i: (0, 0))
    tab = pl.BlockSpec((tm, 2 * LANES), lambda i: (i % nt, 0))
    f = lambda w, dt: jax.ShapeDtypeStruct((m, w), dt)
    return pl.pallas_call(
        _odd_proj_body,
        grid=(m // tm,),
        in_specs=[row(D_MODEL)] + [full(a) for a in (wa, wb, wbr, wg, wq, wqr, qn, kvn)] + [tab, tab],
        out_specs=[row(QCAT_W), row(KVPE_W), row(KVPE_W), row(D_C)],
        out_shape=[f(QCAT_W, qcat_dtype), f(KVPE_W, F32), f(KVPE_W, BF16), f(D_C, F32)],
        compiler_params=_cparams(1),
        name="odd_proj",
    )(x2d, wa, wb, wbr, wg, wq, wqr, qn, kvn, cos_t, sin_t)


def _mla_expand_q(qcat, wexp_ref):
    rows = [_dot(qcat[:, h * LANES:(h + 1) * LANES].astype(BF16), wexp_ref[h]) for h in range(H_C)]
    return jnp.concatenate(rows, axis=0).astype(BF16)


def _mla_softmax_step(s, m_old, vmat):
    m_new = jnp.maximum(m_old, jnp.max(s, axis=1, keepdims=True))
    alpha = jnp.exp2((m_old - m_new) * MLA_EXP2_SCALE)
    m_wide = jnp.concatenate([m_new] * (s.shape[1] // LANES), axis=1)
    p = jnp.exp2((s - m_wide) * MLA_EXP2_SCALE)
    pv = _dot(p.astype(BF16), vmat)
    return m_new, jnp.concatenate([alpha] * (KVPE_W // LANES), axis=1), pv


def _mla_head_pair_out(acc_a, acc_b, wuv_pair):
    lat = [a[:, :KV_LORA] / a[:, MLA_DEN_LANE:MLA_DEN_LANE + 1] for a in (acc_a, acc_b)]
    return _dot(jnp.concatenate(lat, axis=1).astype(BF16), wuv_pair)


def _mla_prompt_body(qcat_ref, kv_ref, wk_ref, wv_ref, o_ref, kx_scr, vx_scr, m_scr, acc_scr):
    i = pl.program_id(1)
    tq = qcat_ref.shape[1]
    s = kv_ref.shape[1]

    @pl.when(i == 0)
    def _():
        lane = lax.broadcasted_iota(I32, (tq, QCAT_W), 1)
        ones = (((lane >> 7) ^ (lane >> 6)) & 1) == 1
        for r in range(s // tq):
            rs = slice(r * tq, (r + 1) * tq)
            kv = kv_ref[0, rs, :]
            kx_scr[rs, :] = _dot(kv, wk_ref[...]).astype(BF16)
            vx_scr[rs, :] = jnp.where(ones, 1.0, _dot(kv[:, :KV_LORA], wv_ref[...])).astype(BF16)

    m_scr[...] = jnp.full(m_scr.shape, NEG_INF, F32)
    acc_scr[...] = jnp.zeros(acc_scr.shape, F32)

    def chunk(c, causal):
        k0 = pl.multiple_of(c * tq, tq)
        for h in range(H_C):
            hs = slice(h * LANES, (h + 1) * LANES)
            rs = slice(h * tq, (h + 1) * tq)
            sc = _dot_nt(qcat_ref[0, :, hs], kx_scr[pl.ds(k0, tq), hs])
            if causal:
                row = lax.broadcasted_iota(I32, (tq, tq), 0)
                col = lax.broadcasted_iota(I32, (tq, tq), 1)
                sc = jnp.where(col <= row, sc, NEG_INF)
            m_old = m_scr[rs, :]
            m_new = jnp.maximum(m_old, jnp.max(sc, axis=1, keepdims=True))
            alpha = jnp.exp2((m_old - m_new) * MLA_EXP2_SCALE)
            p = jnp.exp2((sc - jnp.concatenate([m_new] * (tq // LANES), axis=1)) * MLA_EXP2_SCALE)
            acc_scr[rs, :] = alpha * acc_scr[rs, :] + _dot(p.astype(BF16), vx_scr[pl.ds(k0, tq), hs])
            m_scr[rs, :] = m_new

    def full_chunk(c, carry):
        chunk(c, False)
        return carry

    lax.fori_loop(0, i, full_chunk, 0)
    chunk(i, True)
    lo = lax.broadcasted_iota(I32, (tq, LANES), 1) < DV_C
    for j in range(H_C // 2):
        ae = acc_scr[2 * j * tq:(2 * j + 1) * tq, :]
        ao = acc_scr[(2 * j + 1) * tq:(2 * j + 2) * tq, :]
        o_ref[0, :, j * LANES:(j + 1) * LANES] = jnp.where(
            lo, ae / pltpu.roll(ae, DV_C, 1), ao / pltpu.roll(ao, DV_C, 1))


def _mla_prompt(qcat3, kvpeb3, wk2, wv2):
    b, s, _ = qcat3.shape
    tq = min(TQ_MLA, s)
    assert s % tq == 0 and tq % LANES == 0
    rows = H_C * tq
    return pl.pallas_call(
        _mla_prompt_body,
        grid=(b, s // tq),
        in_specs=[pl.BlockSpec((1, tq, QCAT_W), lambda i, j: (i, j, 0)),
                  pl.BlockSpec((1, s, KVPE_W), lambda i, j: (i, 0, 0)),
                  pl.BlockSpec(wk2.shape, lambda i, j: (0, 0)),
                  pl.BlockSpec(wv2.shape, lambda i, j: (0, 0))],
        out_specs=pl.BlockSpec((1, tq, D_C), lambda i, j: (i, j, 0)),
        out_shape=jax.ShapeDtypeStruct((b, s, D_C), F32),
        scratch_shapes=[pltpu.VMEM((s, QCAT_W), BF16), pltpu.VMEM((s, QCAT_W), BF16),
                        pltpu.VMEM((rows, LANES), F32), pltpu.VMEM((rows, LANES), F32)],
        compiler_params=_cparams(2),
        name="mla_prompt",
    )(qcat3, kvpeb3, wk2, wv2)


def _mla_sample_body(g_pages, pt_ref, qcat_ref, knew_ref, wexp_ref, wuv_ref, ckv_hbm, kpe_hbm, o_ref,
                     cbuf, pbuf, csem, psem, qall_scr, ccat_scr, pcat_scr, m_scr, acc_scr):
    b = pl.program_id(0)
    c = pl.program_id(1)
    nsteps = pl.num_programs(1)
    t = qcat_ref.shape[1]
    slot = _PagedPrefetch(pt_ref, [(ckv_hbm, cbuf, csem), (kpe_hbm, pbuf, psem)], g_pages, nsteps).advance(
        b, c, pl.num_programs(0))

    @pl.when((b == 0) & (c == 0))
    def _():
        pcat_scr[...] = jnp.zeros(pcat_scr.shape, BF16)
        ccat_scr[:, KV_LORA:] = jnp.ones((ccat_scr.shape[0], KVPE_W - KV_LORA), BF16)

    @pl.when(c == 0)
    def _():
        qall_scr[...] = _mla_expand_q(qcat_ref[0], wexp_ref)
        m_scr[...] = jnp.full(m_scr.shape, NEG_INF, F32)
        acc_scr[...] = jnp.zeros(acc_scr.shape, F32)

    def update(s, vmat):
        m_new, alpha, pv = _mla_softmax_step(s, m_scr[...], vmat)
        acc_scr[...] = alpha * acc_scr[...] + pv
        m_scr[...] = m_new

    q = qall_scr[...]
    sub = min(MLA_SUB_PAGES, g_pages)
    m_run = m_scr[...]
    acc_run = acc_scr[...]
    nblk = g_pages // sub

    def scores(blk):
        rs = slice(blk * sub * PAGE_SIZE, (blk + 1) * sub * PAGE_SIZE)
        for g in range(blk * sub, (blk + 1) * sub):
            ccat_scr[g * PAGE_SIZE:(g + 1) * PAGE_SIZE, 0:KV_LORA] = cbuf[slot, g].astype(BF16)
            pcat_scr[0:ROPE_C, g * PAGE_SIZE:(g + 1) * PAGE_SIZE] = pbuf[slot, g].astype(BF16)
        return _dot_nt(q[:, :KV_LORA], ccat_scr[rs, 0:KV_LORA]) + _dot(q[:, KV_LORA:], pcat_scr[:, rs])

    s_next = scores(0)
    for blk in range(nblk):
        s = s_next
        if blk + 1 < nblk:
            s_next = scores(blk + 1)
        rs = slice(blk * sub * PAGE_SIZE, (blk + 1) * sub * PAGE_SIZE)
        m_run, alpha, pv = _mla_softmax_step(s, m_run, ccat_scr[rs, :])
        acc_run = alpha * acc_run + pv
    m_scr[...] = m_run
    acc_scr[...] = acc_run

    @pl.when(c == nsteps - 1)
    def _():
        kn = knew_ref[0]
        kn = jnp.concatenate([kn[:, :KV_LORA], _ones_beyond_rope(kn[:, KV_LORA:])], axis=1)
        knp = jnp.concatenate([kn, jnp.zeros((LANES - t, KVPE_W), F32)], axis=0).astype(BF16)
        sn = _dot_nt(q, knp)
        r = lax.broadcasted_iota(I32, (H_C * t, LANES), 0)
        col = lax.broadcasted_iota(I32, (H_C * t, LANES), 1)
        sn = jnp.where(col <= (r & (t - 1)), sn, NEG_INF)
        update(sn, knp)
        acc = acc_scr[...]
        outs = [_mla_head_pair_out(acc[2 * j * t:(2 * j + 1) * t], acc[(2 * j + 1) * t:(2 * j + 2) * t], wuv_ref[j])
                for j in range(H_C // 2)]
        o_ref[0] = jnp.concatenate(outs, axis=1)


def _mla_sample(page_table, qcat3, kvpeb3, wexp, wuvp, cache_ckv, cache_kpe):
    db, t, _ = qcat3.shape
    n_pages = page_table.shape[1]
    g = min(G_MLA, n_pages)
    assert n_pages % g == 0
    rows = H_C * t
    grid_spec = pltpu.PrefetchScalarGridSpec(
        num_scalar_prefetch=1,
        grid=(db, n_pages // g),
        in_specs=[pl.BlockSpec((1, t, QCAT_W), lambda b, c, pt: (b, 0, 0)),
                  pl.BlockSpec((1, t, KVPE_W), lambda b, c, pt: (b, 0, 0)),
                  pl.BlockSpec(wexp.shape, lambda b, c, pt: (0, 0, 0)),
                  pl.BlockSpec(wuvp.shape, lambda b, c, pt: (0, 0, 0)),
                  pl.BlockSpec(memory_space=pl.ANY), pl.BlockSpec(memory_space=pl.ANY)],
        out_specs=pl.BlockSpec((1, t, D_C), lambda b, c, pt: (b, 0, 0)),
        scratch_shapes=[pltpu.VMEM((2, g, PAGE_SIZE, KV_LORA), F32), pltpu.VMEM((2, g, ROPE_C, PAGE_SIZE), F32),
                        pltpu.SemaphoreType.DMA((2,)), pltpu.SemaphoreType.DMA((2,)),
                        pltpu.VMEM((rows, KVPE_W), BF16),
                        pltpu.VMEM((g * PAGE_SIZE, KVPE_W), BF16), pltpu.VMEM((LANES, g * PAGE_SIZE), BF16),
                        pltpu.VMEM((rows, LANES), F32), pltpu.VMEM((rows, KVPE_W), F32)],
    )
    return pl.pallas_call(
        functools.partial(_mla_sample_body, g),
        grid_spec=grid_spec,
        out_shape=jax.ShapeDtypeStruct((db, t, D_C), F32),
        compiler_params=_cparams(2),
        name="mla_sample",
    )(page_table, qcat3, kvpeb3, wexp, wuvp, cache_ckv, cache_kpe)


def _rope_cos_sin(pos):
    inv = ROPE_THETA ** (-jnp.arange(ROPE_HALF, dtype=F32) / ROPE_HALF)
    ang = pos.astype(F32)[:, None] * inv
    return jnp.cos(ang), jnp.sin(ang)


def _rope_lane_pattern(width, period, rot_start, limit):
    lane = np.arange(width)
    d = lane % period - rot_start
    inside = lane < limit
    x1 = inside & (d >= 0) & (d < ROPE_HALF)
    x2 = inside & (d >= ROPE_HALF) & (d < 2 * ROPE_HALF)
    return x1, x2


def _rope_tables(cos, sin, x1, x2, base):
    width = x1.shape[0]
    reps = width // ROPE_HALF
    cos_w = jnp.tile(cos, (1, reps))
    sin_w = jnp.tile(sin, (1, reps))
    cos_t = jnp.where(x1 | x2, cos_w, jnp.asarray(base, F32)[None, :])
    sin_t = jnp.where(x1, -sin_w, jnp.where(x2, sin_w, 0.0))
    return cos_t, sin_t


def _partner_columns(w, x1, x2):
    lane = np.arange(w.shape[1])
    src = lane + ROPE_HALF * x1 - ROPE_HALF * x2
    return jnp.where((x1 | x2)[None, :], w[:, src], 0.0)


def _tile_rows(tab, reps):
    return jnp.tile(tab, (reps, 1))


def kernel(x_prompt, x_sample, state_pool, cache_k_b, cache_v_b, cache_kidx_b, cache_ckv, cache_kpe, page_table, w_in_even, pool_w, pool_scale, w_out_even, rel_bias, w_in_odd, q_norm, w_q_b, kv_norm, w_uk, w_uv, w_out_odd, ln_g, ln_b):
    bsz, seq, _ = x_prompt.shape
    db, t, _ = x_sample.shape
    n_pages = page_table.shape[1]
    past = n_pages * PAGE_SIZE
    mp, ms = bsz * seq, db * t
    assert w_in_even.shape[0] == 1 and w_in_odd.shape[0] == 1 and t == 8

    cos_p, sin_p = _rope_cos_sin(jnp.arange(seq))
    cos_s, sin_s = _rope_cos_sin(past + jnp.arange(t))

    we = w_in_even[0]
    n_main = 6 * D_A
    wm = we[:, :n_main].astype(BF16)
    w_qi = we[:, n_main:n_main + H_IDX * D_IDX]
    w_ki = we[:, n_main + H_IDX * D_IDX:n_main + H_IDX * D_IDX + D_IDX]
    w_wi = we[:, n_main + H_IDX * D_IDX + D_IDX:]
    ws32 = jnp.concatenate([w_qi, w_ki, w_ki, w_wi, jnp.zeros((D_MODEL, IDX_W - IDX_WI - H_IDX), F32)], axis=1)
    x1e, x2e = _rope_lane_pattern(IDX_W, D_IDX, 0, IDX_WI)
    base_e = np.where(np.arange(IDX_W) < IDX_WI, 1.0, np.where(np.arange(IDX_W) < IDX_WI + H_IDX, H_IDX ** -0.5, 0.0))
    ws = ws32.astype(BF16)
    wr = _partner_columns(ws32, x1e, x2e).astype(BF16)
    pw = pool_w[0].astype(BF16)
    pscale = pool_scale[0][None, :]
    woe = w_out_even[0].astype(BF16)
    lng0, lnb0 = ln_g[0][None, :], ln_b[0][None, :]
    lng1, lnb1 = ln_g[1][None, :], ln_b[1][None, :]

    def even_tables(cos, sin, reps):
        ct, st = _rope_tables(cos, sin, x1e, x2e, base_e)
        return _tile_rows(ct, reps), _tile_rows(st, reps)

    xp2 = x_prompt.reshape(mp, D_MODEL)
    ct, st = even_tables(cos_p, sin_p, 1)
    u, ga, q, k_fm, v_fm, gb, idx, qb, kb, vx, kib = _even_proj(xp2, wm, ws, wr, ct, st, seq_len=seq)
    r3 = lambda a: a.reshape(bsz, seq, a.shape[-1])
    a_p = _pool_prompt(r3(u), pw, pscale).reshape(mp, D_A)
    o_p = _dsa_prompt(rel_bias, r3(qb), r3(idx), r3(kb), r3(vx), r3(kib)).reshape(mp, D_B)
    xp1 = _gate_out_ln(xp2, [(a_p, ga), (o_p, gb)], woe, lng0, lnb0, "even_out_prompt")
    pool_p = r3(u)[:, seq - POOL_STATE:][None]
    kb_p = jnp.transpose(k_fm.reshape(bsz, H_B, DH_B, seq), (0, 3, 1, 2))[None]
    vb_p = jnp.transpose(v_fm.reshape(bsz, H_B, DH_B, seq), (0, 3, 1, 2))[None]
    ki_p = r3(idx)[:, :, IDX_KI:IDX_KI + D_IDX][None]

    xs2 = x_sample.reshape(ms, D_MODEL)
    tm_s = min(TM_PROJ, ms)
    ct, st = even_tables(cos_s, sin_s, tm_s // t)
    u, ga, q, k, v, gb, idx, qb, kb, vx, kib = _even_proj(xs2, wm, ws, wr, ct, st)
    s3 = lambda a: a.reshape(db, t, a.shape[-1])
    u_s = s3(u)
    ext = jnp.concatenate([jnp.zeros((db, 16 - POOL_STATE, D_A), F32), state_pool[0], u_s], axis=1)
    a_s = _pool_sample(ext, pw, pscale)
    n_phys = cache_k_b.shape[1]
    kidx_t = jnp.transpose(cache_kidx_b, (0, 1, 3, 2))
    ck4 = jnp.transpose(cache_k_b, (0, 1, 3, 4, 2)).reshape(1, n_phys, D_B, PAGE_SIZE)
    cv4 = jnp.transpose(cache_v_b, (0, 1, 3, 4, 2)).reshape(1, n_phys, D_B, PAGE_SIZE)
    kpe_t = jnp.transpose(cache_kpe, (0, 1, 3, 2))
    sel = _dsa_select(page_table, s3(idx), kidx_t)
    o_s = _dsa_sample(page_table, rel_bias, s3(q), sel, s3(k), s3(v), ck4, cv4).reshape(ms, D_B)
    xs1 = _gate_out_ln(xs2, [(a_s, ga), (o_s, gb)], woe, lng0, lnb0, "even_out_sample")
    pool_s = ext[:, ext.shape[1] - POOL_STATE:][None]
    kb_s = k.reshape(1, db, t, H_B, DH_B)
    vb_s = v.reshape(1, db, t, H_B, DH_B)
    ki_s = s3(idx)[:, :, IDX_KI:IDX_KI + D_IDX][None]

    wo = w_in_odd[0]
    wa = wo[:, :Q_LORA].astype(BF16)
    wb32 = jnp.concatenate([wo[:, Q_LORA:Q_LORA + KV_LORA + ROPE_C],
                            jnp.zeros((D_MODEL, KVPE_W - KV_LORA - ROPE_C), F32)], axis=1)
    x1k, x2k = _rope_lane_pattern(KVPE_W, KVPE_W, KV_LORA, KVPE_W)
    wb = wb32.astype(BF16)
    wbr = _partner_columns(wb32, x1k, x2k).astype(BF16)
    wg = wo[:, Q_LORA + KV_LORA + ROPE_C:].astype(BF16)
    wqb = w_q_b[0]
    wq32 = jnp.concatenate([wqb, jnp.zeros((Q_LORA, H_C, LANES - NOPE_C - ROPE_C), F32)], axis=2)
    wq32 = wq32.reshape(Q_LORA, QCAT_W)
    x1q, x2q = _rope_lane_pattern(QCAT_W, LANES, NOPE_C, QCAT_W)
    wq = wq32.astype(BF16)
    wqr = _partner_columns(wq32, x1q, x2q).astype(BF16)
    qn = q_norm[0][None, :]
    kvn = kv_norm[0][None, :]
    base_k = np.zeros(LANES)
    base_q = np.where(np.arange(LANES) < NOPE_C, 1.0, 0.0)

    def odd_tables(cos, sin, reps):
        ck, sk = _rope_tables(cos, sin, x1k[KV_LORA:], x2k[KV_LORA:], base_k)
        cq, sq = _rope_tables(cos, sin, x1q[:LANES], x2q[:LANES], base_q)
        return (_tile_rows(jnp.concatenate([ck, cq], axis=1), reps),
                _tile_rows(jnp.concatenate([sk, sq], axis=1), reps))

    wuk = w_uk[0]
    wexp = jnp.zeros((H_C, LANES, KVPE_W), F32)
    wexp = wexp.at[:, :NOPE_C, :KV_LORA].set(jnp.transpose(wuk, (1, 2, 0)))
    wexp = wexp.at[:, NOPE_C:NOPE_C + ROPE_C, KV_LORA:KV_LORA + ROPE_C].set(jnp.eye(ROPE_C, dtype=F32)[None])
    wexp = wexp.astype(BF16)
    wuv = jnp.transpose(w_uv[0], (1, 0, 2))
    wuvp = jnp.zeros((H_C // 2, 2 * KV_LORA, 2 * DV_C), F32)
    wuvp = wuvp.at[:, :KV_LORA, :DV_C].set(wuv[0::2])
    wuvp = wuvp.at[:, KV_LORA:, DV_C:].set(wuv[1::2])
    wuvp = wuvp.astype(BF16)
    wk2 = jnp.zeros((KVPE_W, H_C, LANES), F32)
    wk2 = wk2.at[:KV_LORA, :, :NOPE_C].set(wuk)
    wk2 = wk2.at[KV_LORA:KV_LORA + ROPE_C, :, NOPE_C:NOPE_C + ROPE_C].set(
        jnp.broadcast_to(jnp.eye(ROPE_C, dtype=F32)[:, None, :], (ROPE_C, H_C, ROPE_C)))
    wk2 = wk2.reshape(KVPE_W, QCAT_W).astype(BF16)
    wv2 = jnp.zeros((KV_LORA, H_C // 2, 2, LANES), F32)
    wv2 = wv2.at[:, :, 0, :DV_C].set(w_uv[0][:, 0::2])
    wv2 = wv2.at[:, :, 1, DV_C:].set(w_uv[0][:, 1::2])
    wv2 = wv2.reshape(KV_LORA, QCAT_W).astype(BF16)
    woo = w_out_odd[0].astype(BF16)

    ct, st = odd_tables(cos_p, sin_p, 1)
    qcat, kvpe, kvpeb, g1 = _odd_proj(xp1, wa, wb, wbr, wg, wq, wqr, qn, kvn, ct, st, BF16)
    o1 = _mla_prompt(r3(qcat), r3(kvpeb), wk2, wv2).reshape(mp, D_C)
    y_p = _gate_out_ln(xp1, [(o1, g1)], woo, lng1, lnb1, "odd_out_prompt").reshape(bsz, seq, D_MODEL)
    ckv_p = r3(kvpe)[:, :, :KV_LORA][None]
    kpe_p = r3(kvpe)[:, :, KV_LORA:KV_LORA + ROPE_C][None]

    ct, st = odd_tables(cos_s, sin_s, tm_s // t)
    qcat, kvpe, kvpeb, g1 = _odd_proj(xs1, wa, wb, wbr, wg, wq, wqr, qn, kvn, ct, st, F32)
    o1 = _mla_sample(page_table, s3(qcat), s3(kvpe), wexp, wuvp, cache_ckv, kpe_t).reshape(ms, D_C)
    y_s = _gate_out_ln(xs1, [(o1, g1)], woo, lng1, lnb1, "odd_out_sample").reshape(db, t, D_MODEL)
    ckv_s = s3(kvpe)[:, :, :KV_LORA][None]
    kpe_s = s3(kvpe)[:, :, KV_LORA:KV_LORA + ROPE_C][None]

    return (y_p, y_s, pool_p, pool_s, kb_p, kb_s, vb_p, vb_s, ki_p, ki_s, ckv_p, ckv_s, kpe_p, kpe_s)
```

```python
import functools
import math

import numpy as np
import jax
import jax.numpy as jnp
from jax import lax
from jax.experimental import pallas as pl
from jax.experimental.pallas import tpu as pltpu

F32 = jnp.float32
BF16 = jnp.bfloat16
I32 = jnp.int32

D_MODEL = 1024
DEPTH = 2
PAGE_SIZE = 128
D_A = D_MODEL // 2
POOL_WINDOWS = (2, 4, 8, 16)
G_A = D_A // len(POOL_WINDOWS)
POOL_STATE = max(POOL_WINDOWS) - 1
H_B = 8
DH_B = 64
D_B = H_B * DH_B
H_IDX = 4
D_IDX = 64
TOPK_MAX = 256
N_BUCKETS = 32
MAX_DISTANCE = 128
H_C = 16
Q_LORA = 384
KV_LORA = 256
NOPE_C = 64
ROPE_C = 32
DV_C = 64
D_C = H_C * DV_C
MLA_SCALE = (NOPE_C + ROPE_C) ** -0.5
MLA_EXP2_SCALE = MLA_SCALE * math.log2(math.e)
ROPE_THETA = 10000.0
ROPE_HALF = 16
LN_EPS = 1e-5
RMS_EPS = 1e-6
ALPHA = (2 * DEPTH) ** 0.25

LANES = 128
IDX_W = 512
IDX_KI = 256
IDX_WI = 384
KVPE_W = 384
MLA_DEN_LANE = 320
QCAT_W = H_C * LANES

VMEM_LIMIT = 48 * 1024 * 1024

TM_PROJ = 256
TS_POOL = 512
G_POOL = 32
TM_OUT = 256
TQ = 128
TQ_MLA = 256
DSA_CHUNKS_PER_VARIANT = 4
R_PICK = 64
G_DSA = 16
G_MLA = 32
MLA_SUB_PAGES = 8

NEG_INF = float("-inf")
INT_MIN = -(2 ** 31)


def _cparams(n_axes):
    return pltpu.CompilerParams(dimension_semantics=("arbitrary",) * n_axes,
                                vmem_limit_bytes=VMEM_LIMIT)


def _dot(a, b):
    return jnp.dot(a, b, preferred_element_type=F32)


def _dot_nt(a, b):
    return lax.dot_general(a, b, (((1,), (1,)), ((), ())), preferred_element_type=F32)


def _t5_breaks():
    max_d = 2 * MAX_DISTANCE
    d = np.arange(max_d + 1)
    me = N_BUCKETS // 2
    large = me + (np.log(np.maximum(d, 1) / me) / math.log(MAX_DISTANCE / me) * (N_BUCKETS - me)).astype(np.int64)
    large = np.minimum(large, N_BUCKETS - 1)
    bucket = np.where(d < me, d, large)
    assert np.all(bucket[MAX_DISTANCE:] == N_BUCKETS - 1)
    return int(bucket[0]), [(int(i), int(bucket[i])) for i in range(1, max_d + 1) if bucket[i] != bucket[i - 1]]


_T5_FIRST, _T5_BREAKS = _t5_breaks()


def _bias_of_dist(dist, relb_ref, h):
    val = jnp.full(dist.shape, relb_ref[_T5_FIRST, h], F32)
    for p, bk in _T5_BREAKS:
        val = jnp.where(dist >= p, relb_ref[bk, h], val)
    return val


def _sortable_key(score):
    score = jnp.where(score == 0.0, 0.0, score)
    bits = lax.bitcast_convert_type(score, I32)
    return jnp.where(bits < 0, bits ^ jnp.int32(0x7FFFFFFF), bits)


def _kth_largest_key(keys_ref, nc, rows, k, splits):
    rs = rows // splits

    def count_ge(part, cand):
        blk = keys_ref[0:nc, part * rs:(part + 1) * rs, :]
        hit = jnp.where(blk >= cand[None], 1.0, 0.0)
        return jnp.sum(jnp.sum(hit, axis=0), axis=1, keepdims=True)

    def body(it, ts):
        bit = jnp.left_shift(jnp.int32(1), jnp.int32(31) - it)
        out = []
        for part, t in enumerate(ts):
            cand = t + bit
            out.append(jnp.where(count_ge(part, cand) >= k, cand, t))
        return tuple(out)

    init = tuple(jnp.full((rs, 1), INT_MIN, I32) for _ in range(splits))
    ts = lax.fori_loop(0, 32, body, init)
    return ts[0] if splits == 1 else jnp.concatenate(ts, axis=0)


def _select_topk(keys_ref, nc, rows, topk, splits):
    t = _kth_largest_key(keys_ref, nc, rows, topk, splits)[None]
    keys = keys_ref[0:nc]
    gt = keys > t
    eqm = keys == t
    cnt_gt = jnp.sum(jnp.sum(jnp.where(gt, 1.0, 0.0), axis=0), axis=1, keepdims=True)
    need = (topk - cnt_gt)[None]
    r = lax.broadcasted_iota(I32, (LANES, 2 * LANES), 0)
    c = lax.broadcasted_iota(I32, (LANES, 2 * LANES), 1)
    su_ones = jnp.where((r < c) | (c >= LANES), 1.0, 0.0).astype(BF16)
    eq = jnp.where(eqm, 1.0, 0.0).reshape(nc * rows, LANES).astype(BF16)
    pt = _dot(eq, su_ones)
    pre = pt[:, :LANES].reshape(nc, rows, LANES)
    tot = pt[:, LANES:].reshape(nc, rows, LANES)
    offs = []
    off = jnp.zeros((rows, LANES), F32)
    for cc in range(nc):
        offs.append(off)
        off = off + tot[cc]
    before = pre + jnp.stack(offs, axis=0)
    return gt | (eqm & (before < need))


def _silu_gate(v, g):
    return v * (g * (1.0 / (1.0 + jnp.exp(-g))))


def _even_proj_body(prompt, x_ref, wm_ref, ws_ref, wr_ref, cos_ref, sin_ref, u_ref, ga_ref, gb_ref, *outs):
    xb = x_ref[...].astype(BF16)

    def mm(n):
        return _dot(xb, wm_ref[:, n * D_A:(n + 1) * D_A])

    u_ref[...] = mm(0)
    ga_ref[...] = mm(1)
    gb_ref[...] = mm(5)
    q, k, v = mm(2), mm(3), mm(4)
    idx = _dot(xb, ws_ref[...]) * cos_ref[...] + _dot(xb, wr_ref[...]) * sin_ref[...]
    if prompt:
        kt_ref, vt_ref, idxt_ref, qtb_ref, vtb_ref, kb_ref, kib_ref = outs
        vt = v.T
        kt_ref[0] = k.T
        vt_ref[0] = vt
        vtb_ref[0] = vt.astype(BF16)
        idxt_ref[0] = idx.T
        qtb_ref[0] = (q * DH_B ** -0.5).T.astype(BF16)
        kb_ref[...] = k.astype(BF16)
        kib_ref[...] = idx[:, IDX_KI:IDX_KI + LANES].astype(BF16)
    else:
        q_ref, k_ref, v_ref, idx_ref = outs
        q_ref[...] = q
        k_ref[...] = k
        v_ref[...] = v
        idx_ref[...] = idx


def _even_proj(x2d, wm, ws, wr, cos_t, sin_t, seq_len=None):
    m = x2d.shape[0]
    tm = min(TM_PROJ, m)
    nt = cos_t.shape[0] // tm
    row = lambda w: pl.BlockSpec((tm, w), lambda i: (i, 0))
    full = lambda a: pl.BlockSpec(a.shape, lambda i: (0, 0))
    tab = pl.BlockSpec((tm, IDX_W), lambda i: (i % nt, 0))
    f = lambda w, dt: jax.ShapeDtypeStruct((m, w), dt)
    if seq_len is None:
        extra_specs = [row(D_B)] * 3 + [row(IDX_W)]
        extra_shapes = [f(D_B, F32)] * 3 + [f(IDX_W, F32)]
    else:
        spt = seq_len // tm
        fm = lambda w: pl.BlockSpec((1, w, tm), lambda i: (i // spt, 0, i % spt))
        fms = lambda w, dt: jax.ShapeDtypeStruct((m // seq_len, w, seq_len), dt)
        extra_specs = [fm(D_B), fm(D_B), fm(IDX_W), fm(D_B), fm(D_B), row(D_B), row(LANES)]
        extra_shapes = [fms(D_B, F32), fms(D_B, F32), fms(IDX_W, F32), fms(D_B, BF16), fms(D_B, BF16),
                        f(D_B, BF16), f(LANES, BF16)]
    return pl.pallas_call(
        functools.partial(_even_proj_body, seq_len is not None),
        grid=(m // tm,),
        in_specs=[row(D_MODEL), full(wm), full(ws), full(wr), tab, tab],
        out_specs=[row(D_A)] * 3 + extra_specs,
        out_shape=[f(D_A, F32)] * 3 + extra_shapes,
        compiler_params=_cparams(1),
        name="even_proj",
    )(x2d, wm, ws, wr, cos_t, sin_t)


def _pool_mix_group(win_sum, cur, inv_cnt, pw_ref, scale_ref, g):
    pooled = win_sum * inv_cnt - cur
    mixed = _dot(pooled.astype(BF16), pw_ref[g])
    return mixed * scale_ref[:, g * G_A:(g + 1) * G_A]


def _pool_prompt_body(u_ref, halo_ref, pw_ref, scale_ref, a_ref, ext_ref):
    s = pl.program_id(1)
    ts = u_ref.shape[1]
    halo = halo_ref[0]
    ext_ref[0:16, :] = jnp.where(s == 0, 0.0, halo)
    ext_ref[16:, :] = u_ref[0]
    pos = s * ts + lax.broadcasted_iota(I32, (ts, 1), 0)
    for g, w in enumerate(POOL_WINDOWS):
        sl = pl.ds(g * G_A, G_A)
        acc = ext_ref[pl.ds(16, ts), sl]
        for kk in range(1, w):
            acc = acc + ext_ref[pl.ds(16 - kk, ts), sl]
        cnt = jnp.minimum(pos + 1, w).astype(F32)
        a_ref[0, :, g * G_A:(g + 1) * G_A] = _pool_mix_group(
            acc, ext_ref[pl.ds(16, ts), sl], 1.0 / cnt, pw_ref, scale_ref, g)


def _pool_prompt(u3, pw, scale):
    b, s, _ = u3.shape
    ts = min(TS_POOL, s)
    hb = ts // 16
    return pl.pallas_call(
        _pool_prompt_body,
        grid=(b, s // ts),
        in_specs=[pl.BlockSpec((1, ts, D_A), lambda i, j: (i, j, 0)),
                  pl.BlockSpec((1, 16, D_A), lambda i, j: (i, jnp.maximum(j * hb - 1, 0), 0)),
                  pl.BlockSpec(pw.shape, lambda i, j: (0, 0, 0)),
                  pl.BlockSpec(scale.shape, lambda i, j: (0, 0))],
        out_specs=pl.BlockSpec((1, ts, D_A), lambda i, j: (i, j, 0)),
        out_shape=jax.ShapeDtypeStruct((b, s, D_A), F32),
        scratch_shapes=[pltpu.VMEM((ts + 16, D_A), F32)],
        compiler_params=_cparams(2),
        name="pool_prompt",
    )(u3, u3, pw, scale)


def _pool_sample_body(ext_ref, pw_ref, scale_ref, a_ref):
    gs = ext_ref.shape[0]
    t = ext_ref.shape[1] - 16
    for g, w in enumerate(POOL_WINDOWS):
        sl = pl.ds(g * G_A, G_A)
        cur = ext_ref[:, pl.ds(16, t), sl]
        acc = cur
        for kk in range(1, w):
            acc = acc + ext_ref[:, pl.ds(16 - kk, t), sl]
        acc = acc.reshape(gs * t, G_A)
        cur = cur.reshape(gs * t, G_A)
        a_ref[:, g * G_A:(g + 1) * G_A] = _pool_mix_group(acc, cur, 1.0 / w, pw_ref, scale_ref, g)


def _pool_sample(ext, pw, scale):
    db, e, _ = ext.shape
    t = e - 16
    gs = min(G_POOL, db)
    return pl.pallas_call(
        _pool_sample_body,
        grid=(db // gs,),
        in_specs=[pl.BlockSpec((gs, e, D_A), lambda i: (i, 0, 0)),
                  pl.BlockSpec(pw.shape, lambda i: (0, 0, 0)),
                  pl.BlockSpec(scale.shape, lambda i: (0, 0))],
        out_specs=pl.BlockSpec((gs * t, D_A), lambda i: (i, 0)),
        out_shape=jax.ShapeDtypeStruct((db * t, D_A), F32),
        compiler_params=_cparams(1),
        name="pool_sample",
    )(ext, pw, scale)


def _gate_out_ln_body(nparts, x_ref, *refs):
    parts = refs[:2 * nparts]
    w_ref, g_ref, b_ref, y_ref = refs[2 * nparts:]
    hs = [_silu_gate(parts[2 * p][...], parts[2 * p + 1][...]).astype(BF16) for p in range(nparts)]
    h = hs[0] if nparts == 1 else jnp.concatenate(hs, axis=1)
    z = ALPHA * x_ref[...] + _dot(h, w_ref[...])
    mu = jnp.mean(z, axis=1, keepdims=True)
    zc = z - mu
    var = jnp.mean(zc * zc, axis=1, keepdims=True)
    y_ref[...] = zc * lax.rsqrt(var + LN_EPS) * g_ref[...] + b_ref[...]


def _gate_out_ln(x2d, parts, w, ln_g, ln_b, name):
    m = x2d.shape[0]
    tm = min(TM_OUT, m)
    row = lambda a: pl.BlockSpec((tm, a.shape[1]), lambda i: (i, 0))
    full = lambda a: pl.BlockSpec(a.shape, lambda i: (0, 0))
    flat = [a for pair in parts for a in pair]
    return pl.pallas_call(
        functools.partial(_gate_out_ln_body, len(parts)),
        grid=(m // tm,),
        in_specs=[row(x2d)] + [row(a) for a in flat] + [full(w), full(ln_g), full(ln_b)],
        out_specs=row(x2d),
        out_shape=jax.ShapeDtypeStruct(x2d.shape, F32),
        compiler_params=_cparams(1),
        name=name,
    )(x2d, *flat, w, ln_g, ln_b)


def _kth_largest_key_keymajor(keys_ref, nc, k):
    def body(it, t):
        cand = t + jnp.left_shift(jnp.int32(1), jnp.int32(31) - it)
        hit = jnp.where(keys_ref[0:nc] >= cand[None], 1.0, 0.0)
        cnt = jnp.sum(jnp.sum(hit, axis=0), axis=0, keepdims=True)
        return jnp.where(cnt >= k, cand, t)
    return lax.fori_loop(0, 32, body, jnp.full((1, LANES), INT_MIN, I32))


def _select_topk_keymajor(keys_ref, nc, topk):
    t = _kth_largest_key_keymajor(keys_ref, nc, topk)
    r = lax.broadcasted_iota(I32, (2 * LANES, LANES), 0)
    c = lax.broadcasted_iota(I32, (2 * LANES, LANES), 1)
    sl_ones = jnp.where((c < r) | (r >= LANES), 1.0, 0.0).astype(BF16)
    gts, eqs = [], []
    cnt_gt = jnp.zeros((LANES, LANES), F32)
    for cc in range(nc):
        kc = keys_ref[cc]
        gts.append(kc > t)
        eqs.append(kc == t)
        cnt_gt = cnt_gt + jnp.where(gts[cc], 1.0, 0.0)
    need = topk - jnp.sum(cnt_gt, axis=0, keepdims=True)
    sels = []
    off = jnp.zeros((LANES, LANES), F32)
    for cc in range(nc):
        pt = _dot(sl_ones, jnp.where(eqs[cc], 1.0, 0.0).astype(BF16))
        sels.append(gts[cc] | (eqs[cc] & (pt[:LANES] + off < need)))
        off = off + pt[LANES:]
    return sels


def _dsa_prompt_tile(nc, topk, i, qt_ref, idxt_ref, kb_ref, vt_ref, kib_ref, o_ref,
                     keys_scr, madd_scr, bias_scr):
    w = nc * LANES
    row = lax.broadcasted_iota(I32, (LANES, LANES), 0)
    lane = lax.broadcasted_iota(I32, (LANES, LANES), 1)
    lo = row < DH_B
    idxt = idxt_ref[0]
    kib = kib_ref[0, 0:w, :]

    def head_pair_rhs(pair_t):
        zero = jnp.zeros_like(pair_t)
        return jnp.concatenate([jnp.where(lo, pair_t, zero), jnp.where(lo, zero, pair_t)], axis=1)

    score = None
    for j in range(H_IDX // 2):
        st = _dot(kib, head_pair_rhs(idxt[j * LANES:(j + 1) * LANES, :]).astype(BF16))
        for half in range(2):
            h = 2 * j + half
            wh = idxt[IDX_WI + h:IDX_WI + h + 1, :] * D_IDX ** -0.5
            term = wh * jnp.maximum(st[:, half * LANES:(half + 1) * LANES], 0.0)
            score = term if score is None else score + term
    qpos = i * LANES + lane
    valid = [(c * LANES + row) <= qpos for c in range(nc)]
    for c in range(nc):
        keys_scr[c] = _sortable_key(jnp.where(valid[c], score[c * LANES:(c + 1) * LANES, :], NEG_INF))
    sels = _select_topk_keymajor(keys_scr, nc, topk)
    for c in range(nc):
        madd_scr[c] = jnp.where(sels[c] & valid[c], 0.0, NEG_INF)

    for j in range(H_B // 2):
        logit2 = _dot(kb_ref[0, 0:w, j * LANES:(j + 1) * LANES],
                      head_pair_rhs(qt_ref[0, j * LANES:(j + 1) * LANES, :]))
        ps, dens = [], []
        for half in range(2):
            h = 2 * j + half
            pieces = []
            for c in range(nc):
                lc = logit2[c * LANES:(c + 1) * LANES, half * LANES:(half + 1) * LANES] + madd_scr[c]
                if c >= nc - DSA_CHUNKS_PER_VARIANT - 1:
                    lc = lc + bias_scr[h, jnp.clip(i - c, 0, 2)]
                pieces.append(lc)
            logit = jnp.concatenate(pieces, axis=0)
            p = jnp.exp(logit - jnp.max(logit, axis=0, keepdims=True))
            dens.append(jnp.sum(p, axis=0, keepdims=True))
            ps.append(p.astype(BF16))
        pv = _dot(vt_ref[0, j * LANES:(j + 1) * LANES, 0:w], jnp.concatenate(ps, axis=1))
        ot = jnp.where(lo, pv[:, :LANES] / dens[0], pv[:, LANES:] / dens[1])
        o_ref[0, :, j * LANES:(j + 1) * LANES] = ot.T


def _dsa_prompt_body(topk, relb_ref, qt_ref, idxt_ref, kb_ref, vt_ref, kib_ref, o_ref,
                     keys_scr, madd_scr, bias_scr):
    b = pl.program_id(0)
    i = pl.program_id(1)
    ncs = kb_ref.shape[1] // LANES

    @pl.when((b == 0) & (i == 0))
    def _():
        r = lax.broadcasted_iota(I32, (LANES, LANES), 0)
        c = lax.broadcasted_iota(I32, (LANES, LANES), 1)
        for h in range(H_B):
            far = relb_ref[N_BUCKETS - 1, h]
            bias_scr[h, 0] = _bias_of_dist(c - r, relb_ref, h) - far
            bias_scr[h, 1] = _bias_of_dist(LANES + c - r, relb_ref, h) - far
            bias_scr[h, 2] = jnp.zeros((LANES, LANES), F32)

    nvar = -(-ncs // DSA_CHUNKS_PER_VARIANT)
    for var in range(nvar):
        nc = min((var + 1) * DSA_CHUNKS_PER_VARIANT, ncs)

        @pl.when(i // DSA_CHUNKS_PER_VARIANT == var)
        def _(nc=nc):
            _dsa_prompt_tile(nc, topk, i, qt_ref, idxt_ref, kb_ref, vt_ref, kib_ref, o_ref,
                             keys_scr, madd_scr, bias_scr)


def _dsa_prompt(rel_bias, qt3, idxt3, kb3, vt3, kib3):
    b, s, _ = kb3.shape
    assert s % TQ == 0 and TQ == LANES
    topk = min(TOPK_MAX, s // 4)
    tspec = lambda w: pl.BlockSpec((1, w, TQ), lambda i, j: (i, 0, j))
    kspec = lambda w: pl.BlockSpec((1, s, w), lambda i, j: (i, 0, 0))
    ncs = s // LANES
    return pl.pallas_call(
        functools.partial(_dsa_prompt_body, topk),
        grid=(b, s // TQ),
        in_specs=[pl.BlockSpec(memory_space=pltpu.SMEM),
                  tspec(D_B), tspec(IDX_W), kspec(D_B), pl.BlockSpec((1, D_B, s), lambda i, j: (i, 0, 0)),
                  kspec(LANES)],
        out_specs=pl.BlockSpec((1, TQ, D_B), lambda i, j: (i, j, 0)),
        out_shape=jax.ShapeDtypeStruct((b, s, D_B), F32),
        scratch_shapes=[pltpu.VMEM((ncs, LANES, LANES), I32), pltpu.VMEM((ncs, LANES, LANES), F32),
                        pltpu.VMEM((H_B, 3, LANES, LANES), F32)],
        compiler_params=_cparams(2),
        name="dsa_prompt",
    )(rel_bias, qt3, idxt3, kb3, vt3, kib3)


class _PagedPrefetch:
    def __init__(self, pt_ref, caches, pages_per_step, steps_per_seq):
        self.pt_ref, self.caches = pt_ref, caches
        self.n, self.steps = pages_per_step, steps_per_seq

    def _copies(self, seq, group, slot):
        out = []
        for p in range(self.n):
            page = self.pt_ref[seq, group * self.n + p]
            for hbm, buf, sem in self.caches:
                out.append(pltpu.make_async_copy(hbm.at[0, page], buf.at[slot, p], sem.at[slot]))
        return out

    def advance(self, seq, group, n_seq):
        step = seq * self.steps + group
        slot = step & 1

        @pl.when(step == 0)
        def _():
            for cp in self._copies(seq, group, slot):
                cp.start()

        wrap = group + 1 == self.steps
        nseq = jnp.where(wrap, seq + 1, seq)
        ngroup = jnp.where(wrap, 0, group + 1)

        @pl.when(nseq < n_seq)
        def _():
            for cp in self._copies(nseq, ngroup, 1 - slot):
                cp.start()

        for cp in self._copies(seq, group, slot):
            cp.wait()
        return slot


def _dsa_score_body(n_pages, pt_ref, idx_ref, kidx_hbm, keys_ref, knew_ref, kbuf, ksem, kcat_scr):
    b = pl.program_id(0)
    slot = _PagedPrefetch(pt_ref, [(kidx_hbm, kbuf, ksem)], n_pages, 1).advance(b, 0, pl.num_programs(0))
    t = idx_ref.shape[1]
    idx = idx_ref[0]
    qrows = jnp.concatenate([idx[:, h * D_IDX:(h + 1) * D_IDX] for h in range(H_IDX)], axis=0).astype(BF16)

    def score_of(qk):
        sh = jnp.maximum(qk * D_IDX ** -0.5, 0.0)
        out = None
        for h in range(H_IDX):
            term = idx[:, IDX_WI + h:IDX_WI + h + 1] * sh[h * t:(h + 1) * t]
            out = term if out is None else out + term
        return out

    for g in range(n_pages):
        kcat_scr[:, g * PAGE_SIZE:(g + 1) * PAGE_SIZE] = kbuf[slot, g].astype(BF16)
    score = score_of(_dot(qrows, kcat_scr[...]))
    for g in range(n_pages):
        keys_ref[g] = _sortable_key(score[:, g * PAGE_SIZE:(g + 1) * PAGE_SIZE])

    knew = jnp.concatenate([idx[:, IDX_KI:IDX_KI + D_IDX], jnp.zeros((LANES - t, D_IDX), F32)], axis=0)
    sn = score_of(_dot_nt(qrows, knew.astype(BF16)))
    tq = lax.broadcasted_iota(I32, (t, LANES), 0)
    tk = lax.broadcasted_iota(I32, (t, LANES), 1)
    knew_ref[...] = _sortable_key(jnp.where(tk <= tq, sn, NEG_INF))


def _dsa_pick_body(topk, t, keys_ref, knew_ref, sel_ref, keys_scr):
    nc = keys_ref.shape[0]
    rows = keys_ref.shape[1]
    keys_scr[0:nc] = keys_ref[...]
    keys_scr[nc] = knew_ref[...]
    sel = _select_topk(keys_scr, nc + 1, rows, topk, 1)
    sel_ref[0:nc] = jnp.where(sel[0:nc], 1.0, 0.0)
    tq = lax.broadcasted_iota(I32, (rows, LANES), 0) & (t - 1)
    tk = lax.broadcasted_iota(I32, (rows, LANES), 1)
    sel_ref[nc] = jnp.where(sel[nc] & (tk <= tq), 1.0, 0.0)


def _dsa_select(page_table, idx3, cache_kidx):
    db, t, _ = idx3.shape
    n_pages = page_table.shape[1]
    assert t & (t - 1) == 0
    topk = min(TOPK_MAX, (n_pages * PAGE_SIZE + t) // 4)
    grid_spec = pltpu.PrefetchScalarGridSpec(
        num_scalar_prefetch=1,
        grid=(db,),
        in_specs=[pl.BlockSpec((1, t, IDX_W), lambda b, pt: (b, 0, 0)), pl.BlockSpec(memory_space=pl.ANY)],
        out_specs=[pl.BlockSpec((n_pages, t, LANES), lambda b, pt: (0, b, 0)),
                   pl.BlockSpec((t, LANES), lambda b, pt: (b, 0))],
        scratch_shapes=[pltpu.VMEM((2, n_pages, D_IDX, PAGE_SIZE), F32), pltpu.SemaphoreType.DMA((2,)),
                        pltpu.VMEM((D_IDX, n_pages * PAGE_SIZE), BF16)],
    )
    keys, knew = pl.pallas_call(
        functools.partial(_dsa_score_body, n_pages),
        grid_spec=grid_spec,
        out_shape=[jax.ShapeDtypeStruct((n_pages, db * t, LANES), I32),
                   jax.ShapeDtypeStruct((db * t, LANES), I32)],
        compiler_params=_cparams(1),
        name="dsa_score",
    )(page_table, idx3, cache_kidx)
    rows = min(R_PICK, db * t)
    return pl.pallas_call(
        functools.partial(_dsa_pick_body, topk, t),
        grid=(db * t // rows,),
        in_specs=[pl.BlockSpec((n_pages, rows, LANES), lambda r: (0, r, 0)),
                  pl.BlockSpec((rows, LANES), lambda r: (r, 0))],
        out_specs=pl.BlockSpec((n_pages + 1, rows, LANES), lambda r: (0, r, 0)),
        out_shape=jax.ShapeDtypeStruct((n_pages + 1, db * t, LANES), F32),
        scratch_shapes=[pltpu.VMEM((n_pages + 1, rows, LANES), I32)],
        compiler_params=_cparams(1),
        name="dsa_pick",
    )(keys, knew)


def _dsa_sample_body(g_pages, pt_ref, relb_ref, qb_ref, sel_ref, knew_ref, vnew_ref, k_hbm, v_hbm, o_ref,
                     kbuf, vbuf, ksem, vsem, qbd_scr, kcat_scr, vcat_scr, bias_scr, m_scr, l_scr, acc_scr):
    b = pl.program_id(0)
    c = pl.program_id(1)
    nsteps = pl.num_programs(1)
    t = qb_ref.shape[1]
    slot = _PagedPrefetch(pt_ref, [(k_hbm, kbuf, ksem), (v_hbm, vbuf, vsem)], g_pages, nsteps).advance(
        b, c, pl.num_programs(0))

    @pl.when((b == 0) & (c == 0))
    def _():
        tq = lax.broadcasted_iota(I32, (t, LANES), 0)
        col = lax.broadcasted_iota(I32, (t, LANES), 1)
        for h in range(H_B):
            rs = slice(h * t, (h + 1) * t)
            bias_scr[0, rs, :] = jnp.full((t, LANES), relb_ref[N_BUCKETS - 1, h], F32)
            bias_scr[1, rs, :] = _bias_of_dist(tq + PAGE_SIZE - col, relb_ref, h)
            bias_scr[2, rs, :] = _bias_of_dist(tq - col, relb_ref, h)

    @pl.when(c == 0)
    def _():
        q = qb_ref[0] * DH_B ** -0.5
        lane = lax.broadcasted_iota(I32, (t, D_B), 1)
        qbd = [jnp.where((lane >= h * DH_B) & (lane < (h + 1) * DH_B), q, 0.0) for h in range(H_B)]
        qbd_scr[...] = jnp.concatenate(qbd, axis=0).astype(BF16)
        m_scr[...] = jnp.full(m_scr.shape, NEG_INF, F32)
        l_scr[...] = jnp.zeros(l_scr.shape, F32)
        acc_scr[...] = jnp.zeros(acc_scr.shape, F32)

    def update(logit, sel, vmat, v_key_minor):
        selr = jnp.concatenate([sel] * H_B, axis=0) > 0.5
        lm = jnp.where(selr, logit, NEG_INF)
        m_old = m_scr[...]
        m_new = jnp.maximum(m_old, jnp.max(lm, axis=1, keepdims=True))
        m_safe = jnp.where(m_new == NEG_INF, 0.0, m_new)
        alpha = jnp.exp(m_old - m_safe)
        p = jnp.exp(lm - m_safe)
        l_scr[...] = alpha * l_scr[...] + jnp.sum(p, axis=1, keepdims=True)
        pv = _dot_nt(p.astype(BF16), vmat) if v_key_minor else _dot(p.astype(BF16), vmat)
        acc_scr[...] = alpha * acc_scr[...] + pv
        m_scr[...] = m_new

    for g in range(g_pages):
        kcat_scr[:, g * PAGE_SIZE:(g + 1) * PAGE_SIZE] = kbuf[slot, g].astype(BF16)
        vcat_scr[:, g * PAGE_SIZE:(g + 1) * PAGE_SIZE] = vbuf[slot, g].astype(BF16)
    logit = _dot(qbd_scr[...], kcat_scr[...])
    far = bias_scr[0]
    last = jnp.where(c == nsteps - 1, bias_scr[1], far)
    bias = jnp.concatenate([far] * (g_pages - 1) + [last], axis=1)
    sel = jnp.concatenate([sel_ref[c * g_pages + g] for g in range(g_pages)], axis=1)
    update(logit + bias, sel, vcat_scr[...], True)

    @pl.when(c == nsteps - 1)
    def _():
        pad = jnp.zeros((LANES - t, D_B), F32)
        knp = jnp.concatenate([knew_ref[0], pad], axis=0).astype(BF16)
        vnp = jnp.concatenate([vnew_ref[0], pad], axis=0).astype(BF16)
        ln = _dot_nt(qbd_scr[...], knp) + bias_scr[2]
        update(ln, sel_ref[sel_ref.shape[0] - 1], vnp, False)
        out = acc_scr[...] / l_scr[...]
        lane = lax.broadcasted_iota(I32, (t, D_B), 1)
        o = jnp.zeros((t, D_B), F32)
        for h in range(H_B):
            o = jnp.where((lane >= h * DH_B) & (lane < (h + 1) * DH_B), out[h * t:(h + 1) * t], o)
        o_ref[0] = o


def _dsa_sample(page_table, rel_bias, qb3, sel, k3, v3, cache_k, cache_v):
    db, t, _ = qb3.shape
    n_pages = page_table.shape[1]
    g = min(G_DSA, n_pages)
    assert n_pages % g == 0
    rows = H_B * t
    seq = lambda w: pl.BlockSpec((1, t, w), lambda b, c, pt: (b, 0, 0))
    sel_spec = pl.BlockSpec((sel.shape[0], t, LANES), lambda b, c, pt: (0, b, 0))
    grid_spec = pltpu.PrefetchScalarGridSpec(
        num_scalar_prefetch=1,
        grid=(db, n_pages // g),
        in_specs=[pl.BlockSpec(memory_space=pltpu.SMEM), seq(D_B), sel_spec, seq(D_B), seq(D_B),
                  pl.BlockSpec(memory_space=pl.ANY), pl.BlockSpec(memory_space=pl.ANY)],
        out_specs=seq(D_B),
        scratch_shapes=[pltpu.VMEM((2, g, D_B, PAGE_SIZE), F32), pltpu.VMEM((2, g, D_B, PAGE_SIZE), F32),
                        pltpu.SemaphoreType.DMA((2,)), pltpu.SemaphoreType.DMA((2,)),
                        pltpu.VMEM((rows, D_B), BF16),
                        pltpu.VMEM((D_B, g * PAGE_SIZE), BF16), pltpu.VMEM((D_B, g * PAGE_SIZE), BF16),
                        pltpu.VMEM((3, rows, LANES), F32),
                        pltpu.VMEM((rows, 1), F32), pltpu.VMEM((rows, 1), F32), pltpu.VMEM((rows, D_B), F32)],
    )
    return pl.pallas_call(
        functools.partial(_dsa_sample_body, g),
        grid_spec=grid_spec,
        out_shape=jax.ShapeDtypeStruct((db, t, D_B), F32),
        compiler_params=_cparams(2),
        name="dsa_sample",
    )(page_table, rel_bias, qb3, sel, k3, v3, cache_k, cache_v)


def _ones_beyond_rope(kpe_slab):
    lane = lax.broadcasted_iota(I32, kpe_slab.shape, 1)
    return jnp.where(lane < ROPE_C, kpe_slab, 1.0)


def _rms(x, g):
    return x * lax.rsqrt(jnp.mean(x * x, axis=1, keepdims=True) + RMS_EPS) * g


def _odd_proj_body(x_ref, wa_ref, wb_ref, wbr_ref, wg_ref, wq_ref, wqr_ref, qn_ref, kvn_ref, cos_ref, sin_ref,
                   qcat_ref, kvpe_ref, kvpeb_ref, g_ref):
    xb = x_ref[...].astype(BF16)
    cos_k, sin_k = cos_ref[:, :LANES], sin_ref[:, :LANES]
    cos_q, sin_q = cos_ref[:, LANES:], sin_ref[:, LANES:]
    g_ref[...] = _dot(xb, wg_ref[...])
    kv = _dot(xb, wb_ref[...])
    kvr = _dot(xb, wbr_ref[...])
    ckvn = _rms(kv[:, :KV_LORA], kvn_ref[...])
    kpe = kv[:, KV_LORA:] * cos_k + kvr[:, KV_LORA:] * sin_k
    kvpe = jnp.concatenate([ckvn, kpe], axis=1)
    kvpe_ref[...] = kvpe
    kvpeb_ref[...] = kvpe.astype(BF16)
    cqn = _rms(_dot(xb, wa_ref[...]), qn_ref[...]).astype(BF16)
    qc = _dot(cqn, wq_ref[...])
    qr = _dot(cqn, wqr_ref[...])
    for h in range(H_C):
        sl = slice(h * LANES, (h + 1) * LANES)
        qcat_ref[:, sl] = (qc[:, sl] * cos_q + qr[:, sl] * sin_q).astype(qcat_ref.dtype)


def _odd_proj(x2d, wa, wb, wbr, wg, wq, wqr, qn, kvn, cos_t, sin_t, qcat_dtype):
    m = x2d.shape[0]
    tm = min(TM_PROJ, m)
    nt = cos_t.shape[0] // tm
    row = lambda w: pl.BlockSpec((tm, w), lambda i: (i, 0))
    full = lambda a: pl.BlockSpec(a.shape, lambda i: (0, 0))
    tab = pl.BlockSpec((tm, 2 * LANES), lambda i: (i % nt, 0))
    f = lambda w, dt: jax.ShapeDtypeStruct((m, w), dt)
    return pl.pallas_call(
        _odd_proj_body,
        grid=(m // tm,),
        in_specs=[row(D_MODEL)] + [full(a) for a in (wa, wb, wbr, wg, wq, wqr, qn, kvn)] + [tab, tab],
        out_specs=[row(QCAT_W), row(KVPE_W), row(KVPE_W), row(D_C)],
        out_shape=[f(QCAT_W, qcat_dtype), f(KVPE_W, F32), f(KVPE_W, BF16), f(D_C, F32)],
        compiler_params=_cparams(1),
        name="odd_proj",
    )(x2d, wa, wb, wbr, wg, wq, wqr, qn, kvn, cos_t, sin_t)


def _mla_expand_q(qcat, wexp_ref):
    rows = [_dot(qcat[:, h * LANES:(h + 1) * LANES].astype(BF16), wexp_ref[h]) for h in range(H_C)]
    return jnp.concatenate(rows, axis=0).astype(BF16)


def _mla_softmax_step(s, m_old, vmat):
    m_new = jnp.maximum(m_old, jnp.max(s, axis=1, keepdims=True))
    alpha = jnp.exp2((m_old - m_new) * MLA_EXP2_SCALE)
    m_wide = jnp.concatenate([m_new] * (s.shape[1] // LANES), axis=1)
    p = jnp.exp2((s - m_wide) * MLA_EXP2_SCALE)
    pv = _dot(p.astype(BF16), vmat)
    return m_new, jnp.concatenate([alpha] * (KVPE_W // LANES), axis=1), pv


def _mla_head_pair_out(acc_a, acc_b, wuv_pair):
    lat = [a[:, :KV_LORA] / a[:, MLA_DEN_LANE:MLA_DEN_LANE + 1] for a in (acc_a, acc_b)]
    return _dot(jnp.concatenate(lat, axis=1).astype(BF16), wuv_pair)


def _mla_prompt_body(qcat_ref, kv_ref, wk_ref, wv_ref, o_ref, kx_scr, vx_scr, m_scr, acc_scr):
    i = pl.program_id(1)
    tq = qcat_ref.shape[1]
    s = kv_ref.shape[1]

    @pl.when(i == 0)
    def _():
        lane = lax.broadcasted_iota(I32, (tq, QCAT_W), 1)
        ones = (((lane >> 7) ^ (lane >> 6)) & 1) == 1
        for r in range(s // tq):
            rs = slice(r * tq, (r + 1) * tq)
            kv = kv_ref[0, rs, :]
            kx_scr[rs, :] = _dot(kv, wk_ref[...]).astype(BF16)
            vx_scr[rs, :] = jnp.where(ones, 1.0, _dot(kv[:, :KV_LORA], wv_ref[...])).astype(BF16)

    m_scr[...] = jnp.full(m_scr.shape, NEG_INF, F32)
    acc_scr[...] = jnp.zeros(acc_scr.shape, F32)

    def chunk(c, causal):
        k0 = pl.multiple_of(c * tq, tq)
        for h in range(H_C):
            hs = slice(h * LANES, (h + 1) * LANES)
            rs = slice(h * tq, (h + 1) * tq)
            sc = _dot_nt(qcat_ref[0, :, hs], kx_scr[pl.ds(k0, tq), hs])
            if causal:
                row = lax.broadcasted_iota(I32, (tq, tq), 0)
                col = lax.broadcasted_iota(I32, (tq, tq), 1)
                sc = jnp.where(col <= row, sc, NEG_INF)
            m_old = m_scr[rs, :]
            m_new = jnp.maximum(m_old, jnp.max(sc, axis=1, keepdims=True))
            alpha = jnp.exp2((m_old - m_new) * MLA_EXP2_SCALE)
            p = jnp.exp2((sc - jnp.concatenate([m_new] * (tq // LANES), axis=1)) * MLA_EXP2_SCALE)
            acc_scr[rs, :] = alpha * acc_scr[rs, :] + _dot(p.astype(BF16), vx_scr[pl.ds(k0, tq), hs])
            m_scr[rs, :] = m_new

    def full_chunk(c, carry):
        chunk(c, False)
        return carry

    lax.fori_loop(0, i, full_chunk, 0)
    chunk(i, True)
    lo = lax.broadcasted_iota(I32, (tq, LANES), 1) < DV_C
    for j in range(H_C // 2):
        ae = acc_scr[2 * j * tq:(2 * j + 1) * tq, :]
        ao = acc_scr[(2 * j + 1) * tq:(2 * j + 2) * tq, :]
        o_ref[0, :, j * LANES:(j + 1) * LANES] = jnp.where(
            lo, ae / pltpu.roll(ae, DV_C, 1), ao / pltpu.roll(ao, DV_C, 1))


def _mla_prompt(qcat3, kvpeb3, wk2, wv2):
    b, s, _ = qcat3.shape
    tq = min(TQ_MLA, s)
    assert s % tq == 0 and tq % LANES == 0
    rows = H_C * tq
    return pl.pallas_call(
        _mla_prompt_body,
        grid=(b, s // tq),
        in_specs=[pl.BlockSpec((1, tq, QCAT_W), lambda i, j: (i, j, 0)),
                  pl.BlockSpec((1, s, KVPE_W), lambda i, j: (i, 0, 0)),
                  pl.BlockSpec(wk2.shape, lambda i, j: (0, 0)),
                  pl.BlockSpec(wv2.shape, lambda i, j: (0, 0))],
        out_specs=pl.BlockSpec((1, tq, D_C), lambda i, j: (i, j, 0)),
        out_shape=jax.ShapeDtypeStruct((b, s, D_C), F32),
        scratch_shapes=[pltpu.VMEM((s, QCAT_W), BF16), pltpu.VMEM((s, QCAT_W), BF16),
                        pltpu.VMEM((rows, LANES), F32), pltpu.VMEM((rows, LANES), F32)],
        compiler_params=_cparams(2),
        name="mla_prompt",
    )(qcat3, kvpeb3, wk2, wv2)


def _mla_sample_body(g_pages, pt_ref, qcat_ref, knew_ref, wexp_ref, wuv_ref, ckv_hbm, kpe_hbm, o_ref,
                     cbuf, pbuf, csem, psem, qall_scr, ccat_scr, pcat_scr, m_scr, acc_scr):
    b = pl.program_id(0)
    c = pl.program_id(1)
    nsteps = pl.num_programs(1)
    t = qcat_ref.shape[1]
    slot = _PagedPrefetch(pt_ref, [(ckv_hbm, cbuf, csem), (kpe_hbm, pbuf, psem)], g_pages, nsteps).advance(
        b, c, pl.num_programs(0))

    @pl.when((b == 0) & (c == 0))
    def _():
        pcat_scr[...] = jnp.zeros(pcat_scr.shape, BF16)
        ccat_scr[:, KV_LORA:] = jnp.ones((ccat_scr.shape[0], KVPE_W - KV_LORA), BF16)

    @pl.when(c == 0)
    def _():
        qall_scr[...] = _mla_expand_q(qcat_ref[0], wexp_ref)
        m_scr[...] = jnp.full(m_scr.shape, NEG_INF, F32)
        acc_scr[...] = jnp.zeros(acc_scr.shape, F32)

    def update(s, vmat):
        m_new, alpha, pv = _mla_softmax_step(s, m_scr[...], vmat)
        acc_scr[...] = alpha * acc_scr[...] + pv
        m_scr[...] = m_new

    q = qall_scr[...]
    sub = min(MLA_SUB_PAGES, g_pages)
    m_run = m_scr[...]
    acc_run = acc_scr[...]
    nblk = g_pages // sub

    def scores(blk):
        rs = slice(blk * sub * PAGE_SIZE, (blk + 1) * sub * PAGE_SIZE)
        for g in range(blk * sub, (blk + 1) * sub):
            ccat_scr[g * PAGE_SIZE:(g + 1) * PAGE_SIZE, 0:KV_LORA] = cbuf[slot, g].astype(BF16)
            pcat_scr[0:ROPE_C, g * PAGE_SIZE:(g + 1) * PAGE_SIZE] = pbuf[slot, g].astype(BF16)
        return _dot_nt(q[:, :KV_LORA], ccat_scr[rs, 0:KV_LORA]) + _dot(q[:, KV_LORA:], pcat_scr[:, rs])

    s_next = scores(0)
    for blk in range(nblk):
        s = s_next
        if blk + 1 < nblk:
            s_next = scores(blk + 1)
        rs = slice(blk * sub * PAGE_SIZE, (blk + 1) * sub * PAGE_SIZE)
        m_run, alpha, pv = _mla_softmax_step(s, m_run, ccat_scr[rs, :])
        acc_run = alpha * acc_run + pv
    m_scr[...] = m_run
    acc_scr[...] = acc_run

    @pl.when(c == nsteps - 1)
    def _():
        kn = knew_ref[0]
        kn = jnp.concatenate([kn[:, :KV_LORA], _ones_beyond_rope(kn[:, KV_LORA:])], axis=1)
        knp = jnp.concatenate([kn, jnp.zeros((LANES - t, KVPE_W), F32)], axis=0).astype(BF16)
        sn = _dot_nt(q, knp)
        r = lax.broadcasted_iota(I32, (H_C * t, LANES), 0)
        col = lax.broadcasted_iota(I32, (H_C * t, LANES), 1)
        sn = jnp.where(col <= (r & (t - 1)), sn, NEG_INF)
        update(sn, knp)
        acc = acc_scr[...]
        outs = [_mla_head_pair_out(acc[2 * j * t:(2 * j + 1) * t], acc[(2 * j + 1) * t:(2 * j + 2) * t], wuv_ref[j])
                for j in range(H_C // 2)]
        o_ref[0] = jnp.concatenate(outs, axis=1)


def _mla_sample(page_table, qcat3, kvpeb3, wexp, wuvp, cache_ckv, cache_kpe):
    db, t, _ = qcat3.shape
    n_pages = page_table.shape[1]
    g = min(G_MLA, n_pages)
    assert n_pages % g == 0
    rows = H_C * t
    grid_spec = pltpu.PrefetchScalarGridSpec(
        num_scalar_prefetch=1,
        grid=(db, n_pages // g),
        in_specs=[pl.BlockSpec((1, t, QCAT_W), lambda b, c, pt: (b, 0, 0)),
                  pl.BlockSpec((1, t, KVPE_W), lambda b, c, pt: (b, 0, 0)),
                  pl.BlockSpec(wexp.shape, lambda b, c, pt: (0, 0, 0)),
                  pl.BlockSpec(wuvp.shape, lambda b, c, pt: (0, 0, 0)),
                  pl.BlockSpec(memory_space=pl.ANY), pl.BlockSpec(memory_space=pl.ANY)],
        out_specs=pl.BlockSpec((1, t, D_C), lambda b, c, pt: (b, 0, 0)),
        scratch_shapes=[pltpu.VMEM((2, g, PAGE_SIZE, KV_LORA), F32), pltpu.VMEM((2, g, ROPE_C, PAGE_SIZE), F32),
                        pltpu.SemaphoreType.DMA((2,)), pltpu.SemaphoreType.DMA((2,)),
                        pltpu.VMEM((rows, KVPE_W), BF16),
                        pltpu.VMEM((g * PAGE_SIZE, KVPE_W), BF16), pltpu.VMEM((LANES, g * PAGE_SIZE), BF16),
                        pltpu.VMEM((rows, LANES), F32), pltpu.VMEM((rows, KVPE_W), F32)],
    )
    return pl.pallas_call(
        functools.partial(_mla_sample_body, g),
        grid_spec=grid_spec,
        out_shape=jax.ShapeDtypeStruct((db, t, D_C), F32),
        compiler_params=_cparams(2),
        name="mla_sample",
    )(page_table, qcat3, kvpeb3, wexp, wuvp, cache_ckv, cache_kpe)


def _rope_cos_sin(pos):
    inv = ROPE_THETA ** (-jnp.arange(ROPE_HALF, dtype=F32) / ROPE_HALF)
    ang = pos.astype(F32)[:, None] * inv
    return jnp.cos(ang), jnp.sin(ang)


def _rope_lane_pattern(width, period, rot_start, limit):
    lane = np.arange(width)
    d = lane % period - rot_start
    inside = lane < limit
    x1 = inside & (d >= 0) & (d < ROPE_HALF)
    x2 = inside & (d >= ROPE_HALF) & (d < 2 * ROPE_HALF)
    return x1, x2


def _rope_tables(cos, sin, x1, x2, base):
    width = x1.shape[0]
    reps = width // ROPE_HALF
    cos_w = jnp.tile(cos, (1, reps))
    sin_w = jnp.tile(sin, (1, reps))
    cos_t = jnp.where(x1 | x2, cos_w, jnp.asarray(base, F32)[None, :])
    sin_t = jnp.where(x1, -sin_w, jnp.where(x2, sin_w, 0.0))
    return cos_t, sin_t


def _partner_columns(w, x1, x2):
    lane = np.arange(w.shape[1])
    src = lane + ROPE_HALF * x1 - ROPE_HALF * x2
    return jnp.where((x1 | x2)[None, :], w[:, src], 0.0)


def _tile_rows(tab, reps):
    return jnp.tile(tab, (reps, 1))


def kernel(x_prompt, x_sample, state_pool, cache_k_b, cache_v_b, cache_kidx_b, cache_ckv, cache_kpe, page_table, w_in_even, pool_w, pool_scale, w_out_even, rel_bias, w_in_odd, q_norm, w_q_b, kv_norm, w_uk, w_uv, w_out_odd, ln_g, ln_b):
    bsz, seq, _ = x_prompt.shape
    db, t, _ = x_sample.shape
    n_pages = page_table.shape[1]
    past = n_pages * PAGE_SIZE
    mp, ms = bsz * seq, db * t
    assert w_in_even.shape[0] == 1 and w_in_odd.shape[0] == 1 and t == 8

    cos_p, sin_p = _rope_cos_sin(jnp.arange(seq))
    cos_s, sin_s = _rope_cos_sin(past + jnp.arange(t))

    we = w_in_even[0]
    n_main = 6 * D_A
    wm = we[:, :n_main].astype(BF16)
    w_qi = we[:, n_main:n_main + H_IDX * D_IDX]
    w_ki = we[:, n_main + H_IDX * D_IDX:n_main + H_IDX * D_IDX + D_IDX]
    w_wi = we[:, n_main + H_IDX * D_IDX + D_IDX:]
    ws32 = jnp.concatenate([w_qi, w_ki, w_ki, w_wi, jnp.zeros((D_MODEL, IDX_W - IDX_WI - H_IDX), F32)], axis=1)
    x1e, x2e = _rope_lane_pattern(IDX_W, D_IDX, 0, IDX_WI)
    base_e = np.where(np.arange(IDX_W) < IDX_WI, 1.0, np.where(np.arange(IDX_W) < IDX_WI + H_IDX, H_IDX ** -0.5, 0.0))
    ws = ws32.astype(BF16)
    wr = _partner_columns(ws32, x1e, x2e).astype(BF16)
    pw = pool_w[0].astype(BF16)
    pscale = pool_scale[0][None, :]
    woe = w_out_even[0].astype(BF16)
    lng0, lnb0 = ln_g[0][None, :], ln_b[0][None, :]
    lng1, lnb1 = ln_g[1][None, :], ln_b[1][None, :]

    def even_tables(cos, sin, reps):
        ct, st = _rope_tables(cos, sin, x1e, x2e, base_e)
        return _tile_rows(ct, reps), _tile_rows(st, reps)

    xp2 = x_prompt.reshape(mp, D_MODEL)
    ct, st = even_tables(cos_p, sin_p, 1)
    u, ga, gb, k_fm, v_fm, idx_fm, qt, vt, kb, kib = _even_proj(xp2, wm, ws, wr, ct, st, seq_len=seq)
    r3 = lambda a: a.reshape(bsz, seq, a.shape[-1])
    a_p = _pool_prompt(r3(u), pw, pscale).reshape(mp, D_A)
    o_p = _dsa_prompt(rel_bias, qt, idx_fm, r3(kb), vt, r3(kib)).reshape(mp, D_B)
    xp1 = _gate_out_ln(xp2, [(a_p, ga), (o_p, gb)], woe, lng0, lnb0, "even_out_prompt")
    pool_p = r3(u)[:, seq - POOL_STATE:][None]
    kb_p = jnp.transpose(k_fm.reshape(bsz, H_B, DH_B, seq), (0, 3, 1, 2))[None]
    vb_p = jnp.transpose(v_fm.reshape(bsz, H_B, DH_B, seq), (0, 3, 1, 2))[None]
    ki_p = jnp.transpose(idx_fm[:, IDX_KI:IDX_KI + D_IDX, :], (0, 2, 1))[None]

    xs2 = x_sample.reshape(ms, D_MODEL)
    tm_s = min(TM_PROJ, ms)
    ct, st = even_tables(cos_s, sin_s, tm_s // t)
    u, ga, gb, q, k, v, idx = _even_proj(xs2, wm, ws, wr, ct, st)
    s3 = lambda a: a.reshape(db, t, a.shape[-1])
    u_s = s3(u)
    ext = jnp.concatenate([jnp.zeros((db, 16 - POOL_STATE, D_A), F32), state_pool[0], u_s], axis=1)
    a_s = _pool_sample(ext, pw, pscale)
    n_phys = cache_k_b.shape[1]
    kidx_t = jnp.transpose(cache_kidx_b, (0, 1, 3, 2))
    ck4 = jnp.transpose(cache_k_b, (0, 1, 3, 4, 2)).reshape(1, n_phys, D_B, PAGE_SIZE)
    cv4 = jnp.transpose(cache_v_b, (0, 1, 3, 4, 2)).reshape(1, n_phys, D_B, PAGE_SIZE)
    kpe_t = jnp.transpose(cache_kpe, (0, 1, 3, 2))
    sel = _dsa_select(page_table, s3(idx), kidx_t)
    o_s = _dsa_sample(page_table, rel_bias, s3(q), sel, s3(k), s3(v), ck4, cv4).reshape(ms, D_B)
    xs1 = _gate_out_ln(xs2, [(a_s, ga), (o_s, gb)], woe, lng0, lnb0, "even_out_sample")
    pool_s = ext[:, ext.shape[1] - POOL_STATE:][None]
    kb_s = k.reshape(1, db, t, H_B, DH_B)
    vb_s = v.reshape(1, db, t, H_B, DH_B)
    ki_s = s3(idx)[:, :, IDX_KI:IDX_KI + D_IDX][None]

    wo = w_in_odd[0]
    wa = wo[:, :Q_LORA].astype(BF16)
    wb32 = jnp.concatenate([wo[:, Q_LORA:Q_LORA + KV_LORA + ROPE_C],
                            jnp.zeros((D_MODEL, KVPE_W - KV_LORA - ROPE_C), F32)], axis=1)
    x1k, x2k = _rope_lane_pattern(KVPE_W, KVPE_W, KV_LORA, KVPE_W)
    wb = wb32.astype(BF16)
    wbr = _partner_columns(wb32, x1k, x2k).astype(BF16)
    wg = wo[:, Q_LORA + KV_LORA + ROPE_C:].astype(BF16)
    wqb = w_q_b[0]
    wq32 = jnp.concatenate([wqb, jnp.zeros((Q_LORA, H_C, LANES - NOPE_C - ROPE_C), F32)], axis=2)
    wq32 = wq32.reshape(Q_LORA, QCAT_W)
    x1q, x2q = _rope_lane_pattern(QCAT_W, LANES, NOPE_C, QCAT_W)
    wq = wq32.astype(BF16)
    wqr = _partner_columns(wq32, x1q, x2q).astype(BF16)
    qn = q_norm[0][None, :]
    kvn = kv_norm[0][None, :]
    base_k = np.zeros(LANES)
    base_q = np.where(np.arange(LANES) < NOPE_C, 1.0, 0.0)

    def odd_tables(cos, sin, reps):
        ck, sk = _rope_tables(cos, sin, x1k[KV_LORA:], x2k[KV_LORA:], base_k)
        cq, sq = _rope_tables(cos, sin, x1q[:LANES], x2q[:LANES], base_q)
        return (_tile_rows(jnp.concatenate([ck, cq], axis=1), reps),
                _tile_rows(jnp.concatenate([sk, sq], axis=1), reps))

    wuk = w_uk[0]
    wexp = jnp.zeros((H_C, LANES, KVPE_W), F32)
    wexp = wexp.at[:, :NOPE_C, :KV_LORA].set(jnp.transpose(wuk, (1, 2, 0)))
    wexp = wexp.at[:, NOPE_C:NOPE_C + ROPE_C, KV_LORA:KV_LORA + ROPE_C].set(jnp.eye(ROPE_C, dtype=F32)[None])
    wexp = wexp.astype(BF16)
    wuv = jnp.transpose(w_uv[0], (1, 0, 2))
    wuvp = jnp.zeros((H_C // 2, 2 * KV_LORA, 2 * DV_C), F32)
    wuvp = wuvp.at[:, :KV_LORA, :DV_C].set(wuv[0::2])
    wuvp = wuvp.at[:, KV_LORA:, DV_C:].set(wuv[1::2])
    wuvp = wuvp.astype(BF16)
    wk2 = jnp.zeros((KVPE_W, H_C, LANES), F32)
    wk2 = wk2.at[:KV_LORA, :, :NOPE_C].set(wuk)
    wk2 = wk2.at[KV_LORA:KV_LORA + ROPE_C, :, NOPE_C:NOPE_C + ROPE_C].set(
        jnp.broadcast_to(jnp.eye(ROPE_C, dtype=F32)[:, None, :], (ROPE_C, H_C, ROPE_C)))
    wk2 = wk2.reshape(KVPE_W, QCAT_W).astype(BF16)
    wv2 = jnp.zeros((KV_LORA, H_C // 2, 2, LANES), F32)
    wv2 = wv2.at[:, :, 0, :DV_C].set(w_uv[0][:, 0::2])
    wv2 = wv2.at[:, :, 1, DV_C:].set(w_uv[0][:, 1::2])
    wv2 = wv2.reshape(KV_LORA, QCAT_W).astype(BF16)
    woo = w_out_odd[0].astype(BF16)

    ct, st = odd_tables(cos_p, sin_p, 1)
    qcat, kvpe, kvpeb, g1 = _odd_proj(xp1, wa, wb, wbr, wg, wq, wqr, qn, kvn, ct, st, BF16)
    o1 = _mla_prompt(r3(qcat), r3(kvpeb), wk2, wv2).reshape(mp, D_C)
    y_p = _gate_out_ln(xp1, [(o1, g1)], woo, lng1, lnb1, "odd_out_prompt").reshape(bsz, seq, D_MODEL)
    ckv_p = r3(kvpe)[:, :, :KV_LORA][None]
    kpe_p = r3(kvpe)[:, :, KV_LORA:KV_LORA + ROPE_C][None]

    ct, st = odd_tables(cos_s, sin_s, tm_s // t)
    qcat, kvpe, kvpeb, g1 = _odd_proj(xs1, wa, wb, wbr, wg, wq, wqr, qn, kvn, ct, st, F32)
    o1 = _mla_sample(page_table, s3(qcat), s3(kvpe), wexp, wuvp, cache_ckv, kpe_t).reshape(ms, D_C)
    y_s = _gate_out_ln(xs1, [(o1, g1)], woo, lng1, lnb1, "odd_out_sample").reshape(db, t, D_MODEL)
    ckv_s = s3(kvpe)[:, :, :KV_LORA][None]
    kpe_s = s3(kvpe)[:, :, KV_LORA:KV_LORA + ROPE_C][None]

    return (y_p, y_s, pool_p, pool_s, kb_p, kb_s, vb_p, vb_s, ki_p, ki_s, ckv_p, ckv_s, kpe_p, kpe_s)
```

```python
import functools
import math

import numpy as np
import jax
import jax.numpy as jnp
from jax import lax
from jax.experimental import pallas as pl
from jax.experimental.pallas import tpu as pltpu

F32 = jnp.float32
BF16 = jnp.bfloat16
I32 = jnp.int32

D_MODEL = 1024
DEPTH = 2
PAGE_SIZE = 128
D_A = D_MODEL // 2
POOL_WINDOWS = (2, 4, 8, 16)
G_A = D_A // len(POOL_WINDOWS)
POOL_STATE = max(POOL_WINDOWS) - 1
H_B = 8
DH_B = 64
D_B = H_B * DH_B
H_IDX = 4
D_IDX = 64
TOPK_MAX = 256
N_BUCKETS = 32
MAX_DISTANCE = 128
H_C = 16
Q_LORA = 384
KV_LORA = 256
NOPE_C = 64
ROPE_C = 32
DV_C = 64
D_C = H_C * DV_C
MLA_SCALE = (NOPE_C + ROPE_C) ** -0.5
MLA_EXP2_SCALE = MLA_SCALE * math.log2(math.e)
ROPE_THETA = 10000.0
ROPE_HALF = 16
LN_EPS = 1e-5
RMS_EPS = 1e-6
ALPHA = (2 * DEPTH) ** 0.25

LANES = 128
IDX_W = 512
IDX_KI = 256
IDX_WI = 384
KVPE_W = 384
MLA_DEN_LANE = 320
QCAT_W = H_C * LANES

VMEM_LIMIT = 48 * 1024 * 1024

TM_PROJ = 256
TS_POOL = 512
G_POOL = 32
TM_OUT = 256
TQ = 128
TQ_MLA = 256
DSA_CHUNKS_PER_VARIANT = 4
R_PICK = 64
G_DSA = 16
G_MLA = 64
MLA_SUB_PAGES = 8

NEG_INF = float("-inf")
INT_MIN = -(2 ** 31)


def _cparams(n_axes):
    return pltpu.CompilerParams(dimension_semantics=("arbitrary",) * n_axes,
                                vmem_limit_bytes=VMEM_LIMIT)


def _dot(a, b):
    return jnp.dot(a, b, preferred_element_type=F32)


def _dot_nt(a, b):
    return lax.dot_general(a, b, (((1,), (1,)), ((), ())), preferred_element_type=F32)


def _t5_breaks():
    max_d = 2 * MAX_DISTANCE
    d = np.arange(max_d + 1)
    me = N_BUCKETS // 2
    large = me + (np.log(np.maximum(d, 1) / me) / math.log(MAX_DISTANCE / me) * (N_BUCKETS - me)).astype(np.int64)
    large = np.minimum(large, N_BUCKETS - 1)
    bucket = np.where(d < me, d, large)
    assert np.all(bucket[MAX_DISTANCE:] == N_BUCKETS - 1)
    return int(bucket[0]), [(int(i), int(bucket[i])) for i in range(1, max_d + 1) if bucket[i] != bucket[i - 1]]


_T5_FIRST, _T5_BREAKS = _t5_breaks()


def _bias_of_dist(dist, relb_ref, h):
    val = jnp.full(dist.shape, relb_ref[_T5_FIRST, h], F32)
    for p, bk in _T5_BREAKS:
        val = jnp.where(dist >= p, relb_ref[bk, h], val)
    return val


def _sortable_key(score):
    score = jnp.where(score == 0.0, 0.0, score)
    bits = lax.bitcast_convert_type(score, I32)
    return jnp.where(bits < 0, bits ^ jnp.int32(0x7FFFFFFF), bits)


def _kth_largest_key(keys_ref, nc, rows, k, splits):
    rs = rows // splits

    def count_ge(part, cand):
        blk = keys_ref[0:nc, part * rs:(part + 1) * rs, :]
        hit = jnp.where(blk >= cand[None], 1.0, 0.0)
        return jnp.sum(jnp.sum(hit, axis=0), axis=1, keepdims=True)

    def body(it, ts):
        bit = jnp.left_shift(jnp.int32(1), jnp.int32(31) - it)
        out = []
        for part, t in enumerate(ts):
            cand = t + bit
            out.append(jnp.where(count_ge(part, cand) >= k, cand, t))
        return tuple(out)

    init = tuple(jnp.full((rs, 1), INT_MIN, I32) for _ in range(splits))
    ts = lax.fori_loop(0, 32, body, init)
    return ts[0] if splits == 1 else jnp.concatenate(ts, axis=0)


def _select_topk(keys_ref, nc, rows, topk, splits):
    t = _kth_largest_key(keys_ref, nc, rows, topk, splits)[None]
    keys = keys_ref[0:nc]
    gt = keys > t
    eqm = keys == t
    cnt_gt = jnp.sum(jnp.sum(jnp.where(gt, 1.0, 0.0), axis=0), axis=1, keepdims=True)
    need = (topk - cnt_gt)[None]
    r = lax.broadcasted_iota(I32, (LANES, 2 * LANES), 0)
    c = lax.broadcasted_iota(I32, (LANES, 2 * LANES), 1)
    su_ones = jnp.where((r < c) | (c >= LANES), 1.0, 0.0).astype(BF16)
    eq = jnp.where(eqm, 1.0, 0.0).reshape(nc * rows, LANES).astype(BF16)
    pt = _dot(eq, su_ones)
    pre = pt[:, :LANES].reshape(nc, rows, LANES)
    tot = pt[:, LANES:].reshape(nc, rows, LANES)
    offs = []
    off = jnp.zeros((rows, LANES), F32)
    for cc in range(nc):
        offs.append(off)
        off = off + tot[cc]
    before = pre + jnp.stack(offs, axis=0)
    return gt | (eqm & (before < need))


def _silu_gate(v, g):
    return v * (g * (1.0 / (1.0 + jnp.exp(-g))))


def _even_proj_body(prompt, x_ref, wm_ref, ws_ref, wr_ref, cos_ref, sin_ref, u_ref, ga_ref, gb_ref, *outs):
    xb = x_ref[...].astype(BF16)

    def mm(n):
        return _dot(xb, wm_ref[:, n * D_A:(n + 1) * D_A])

    u_ref[...] = mm(0)
    ga_ref[...] = mm(1)
    gb_ref[...] = mm(5)
    q, k, v = mm(2), mm(3), mm(4)
    idx = _dot(xb, ws_ref[...]) * cos_ref[...] + _dot(xb, wr_ref[...]) * sin_ref[...]
    if prompt:
        kt_ref, vt_ref, idxt_ref, qtb_ref, vtb_ref, kb_ref, kib_ref = outs
        vt = v.T
        kt_ref[0] = k.T
        vt_ref[0] = vt
        vtb_ref[0] = vt.astype(BF16)
        idxt_ref[0] = idx.T
        qtb_ref[0] = (q * DH_B ** -0.5).T.astype(BF16)
        kb_ref[...] = k.astype(BF16)
        kib_ref[...] = idx[:, IDX_KI:IDX_KI + LANES].astype(BF16)
    else:
        q_ref, k_ref, v_ref, idx_ref = outs
        q_ref[...] = q
        k_ref[...] = k
        v_ref[...] = v
        idx_ref[...] = idx


def _even_proj(x2d, wm, ws, wr, cos_t, sin_t, seq_len=None):
    m = x2d.shape[0]
    tm = min(TM_PROJ, m)
    nt = cos_t.shape[0] // tm
    row = lambda w: pl.BlockSpec((tm, w), lambda i: (i, 0))
    full = lambda a: pl.BlockSpec(a.shape, lambda i: (0, 0))
    tab = pl.BlockSpec((tm, IDX_W), lambda i: (i % nt, 0))
    f = lambda w, dt: jax.ShapeDtypeStruct((m, w), dt)
    if seq_len is None:
        extra_specs = [row(D_B)] * 3 + [row(IDX_W)]
        extra_shapes = [f(D_B, F32)] * 3 + [f(IDX_W, F32)]
    else:
        spt = seq_len // tm
        fm = lambda w: pl.BlockSpec((1, w, tm), lambda i: (i // spt, 0, i % spt))
        fms = lambda w, dt: jax.ShapeDtypeStruct((m // seq_len, w, seq_len), dt)
        extra_specs = [fm(D_B), fm(D_B), fm(IDX_W), fm(D_B), fm(D_B), row(D_B), row(LANES)]
        extra_shapes = [fms(D_B, F32), fms(D_B, F32), fms(IDX_W, F32), fms(D_B, BF16), fms(D_B, BF16),
                        f(D_B, BF16), f(LANES, BF16)]
    return pl.pallas_call(
        functools.partial(_even_proj_body, seq_len is not None),
        grid=(m // tm,),
        in_specs=[row(D_MODEL), full(wm), full(ws), full(wr), tab, tab],
        out_specs=[row(D_A)] * 3 + extra_specs,
        out_shape=[f(D_A, F32)] * 3 + extra_shapes,
        compiler_params=_cparams(1),
        name="even_proj",
    )(x2d, wm, ws, wr, cos_t, sin_t)


def _pool_mix_group(win_sum, cur, inv_cnt, pw_ref, scale_ref, g):
    pooled = win_sum * inv_cnt - cur
    mixed = _dot(pooled.astype(BF16), pw_ref[g])
    return mixed * scale_ref[:, g * G_A:(g + 1) * G_A]


def _pool_prompt_body(u_ref, halo_ref, pw_ref, scale_ref, a_ref, ext_ref):
    s = pl.program_id(1)
    ts = u_ref.shape[1]
    halo = halo_ref[0]
    ext_ref[0:16, :] = jnp.where(s == 0, 0.0, halo)
    ext_ref[16:, :] = u_ref[0]
    pos = s * ts + lax.broadcasted_iota(I32, (ts, 1), 0)
    for g, w in enumerate(POOL_WINDOWS):
        sl = pl.ds(g * G_A, G_A)
        acc = ext_ref[pl.ds(16, ts), sl]
        for kk in range(1, w):
            acc = acc + ext_ref[pl.ds(16 - kk, ts), sl]
        cnt = jnp.minimum(pos + 1, w).astype(F32)
        a_ref[0, :, g * G_A:(g + 1) * G_A] = _pool_mix_group(
            acc, ext_ref[pl.ds(16, ts), sl], 1.0 / cnt, pw_ref, scale_ref, g)


def _pool_prompt(u3, pw, scale):
    b, s, _ = u3.shape
    ts = min(TS_POOL, s)
    hb = ts // 16
    return pl.pallas_call(
        _pool_prompt_body,
        grid=(b, s // ts),
        in_specs=[pl.BlockSpec((1, ts, D_A), lambda i, j: (i, j, 0)),
                  pl.BlockSpec((1, 16, D_A), lambda i, j: (i, jnp.maximum(j * hb - 1, 0), 0)),
                  pl.BlockSpec(pw.shape, lambda i, j: (0, 0, 0)),
                  pl.BlockSpec(scale.shape, lambda i, j: (0, 0))],
        out_specs=pl.BlockSpec((1, ts, D_A), lambda i, j: (i, j, 0)),
        out_shape=jax.ShapeDtypeStruct((b, s, D_A), F32),
        scratch_shapes=[pltpu.VMEM((ts + 16, D_A), F32)],
        compiler_params=_cparams(2),
        name="pool_prompt",
    )(u3, u3, pw, scale)


def _pool_sample_body(ext_ref, pw_ref, scale_ref, a_ref):
    gs = ext_ref.shape[0]
    t = ext_ref.shape[1] - 16
    for g, w in enumerate(POOL_WINDOWS):
        sl = pl.ds(g * G_A, G_A)
        cur = ext_ref[:, pl.ds(16, t), sl]
        acc = cur
        for kk in range(1, w):
            acc = acc + ext_ref[:, pl.ds(16 - kk, t), sl]
        acc = acc.reshape(gs * t, G_A)
        cur = cur.reshape(gs * t, G_A)
        a_ref[:, g * G_A:(g + 1) * G_A] = _pool_mix_group(acc, cur, 1.0 / w, pw_ref, scale_ref, g)


def _pool_sample(ext, pw, scale):
    db, e, _ = ext.shape
    t = e - 16
    gs = min(G_POOL, db)
    return pl.pallas_call(
        _pool_sample_body,
        grid=(db // gs,),
        in_specs=[pl.BlockSpec((gs, e, D_A), lambda i: (i, 0, 0)),
                  pl.BlockSpec(pw.shape, lambda i: (0, 0, 0)),
                  pl.BlockSpec(scale.shape, lambda i: (0, 0))],
        out_specs=pl.BlockSpec((gs * t, D_A), lambda i: (i, 0)),
        out_shape=jax.ShapeDtypeStruct((db * t, D_A), F32),
        compiler_params=_cparams(1),
        name="pool_sample",
    )(ext, pw, scale)


def _gate_out_ln_body(nparts, x_ref, *refs):
    parts = refs[:2 * nparts]
    w_ref, g_ref, b_ref, y_ref = refs[2 * nparts:]
    hs = [_silu_gate(parts[2 * p][...], parts[2 * p + 1][...]).astype(BF16) for p in range(nparts)]
    h = hs[0] if nparts == 1 else jnp.concatenate(hs, axis=1)
    z = ALPHA * x_ref[...] + _dot(h, w_ref[...])
    mu = jnp.mean(z, axis=1, keepdims=True)
    zc = z - mu
    var = jnp.mean(zc * zc, axis=1, keepdims=True)
    y_ref[...] = zc * lax.rsqrt(var + LN_EPS) * g_ref[...] + b_ref[...]


def _gate_out_ln(x2d, parts, w, ln_g, ln_b, name):
    m = x2d.shape[0]
    tm = min(TM_OUT, m)
    row = lambda a: pl.BlockSpec((tm, a.shape[1]), lambda i: (i, 0))
    full = lambda a: pl.BlockSpec(a.shape, lambda i: (0, 0))
    flat = [a for pair in parts for a in pair]
    return pl.pallas_call(
        functools.partial(_gate_out_ln_body, len(parts)),
        grid=(m // tm,),
        in_specs=[row(x2d)] + [row(a) for a in flat] + [full(w), full(ln_g), full(ln_b)],
        out_specs=row(x2d),
        out_shape=jax.ShapeDtypeStruct(x2d.shape, F32),
        compiler_params=_cparams(1),
        name=name,
    )(x2d, *flat, w, ln_g, ln_b)


def _kth_largest_key_keymajor(keys_ref, nc, k):
    def body(it, t):
        cand = t + jnp.left_shift(jnp.int32(1), jnp.int32(31) - it)
        hit = jnp.where(keys_ref[0:nc] >= cand[None], 1.0, 0.0)
        cnt = jnp.sum(jnp.sum(hit, axis=0), axis=0, keepdims=True)
        return jnp.where(cnt >= k, cand, t)
    return lax.fori_loop(0, 32, body, jnp.full((1, LANES), INT_MIN, I32))


def _select_topk_keymajor(keys_ref, nc, topk):
    t = _kth_largest_key_keymajor(keys_ref, nc, topk)
    r = lax.broadcasted_iota(I32, (2 * LANES, LANES), 0)
    c = lax.broadcasted_iota(I32, (2 * LANES, LANES), 1)
    sl_ones = jnp.where((c < r) | (r >= LANES), 1.0, 0.0).astype(BF16)
    gts, eqs = [], []
    cnt_gt = jnp.zeros((LANES, LANES), F32)
    for cc in range(nc):
        kc = keys_ref[cc]
        gts.append(kc > t)
        eqs.append(kc == t)
        cnt_gt = cnt_gt + jnp.where(gts[cc], 1.0, 0.0)
    need = topk - jnp.sum(cnt_gt, axis=0, keepdims=True)
    sels = []
    off = jnp.zeros((LANES, LANES), F32)
    for cc in range(nc):
        pt = _dot(sl_ones, jnp.where(eqs[cc], 1.0, 0.0).astype(BF16))
        sels.append(gts[cc] | (eqs[cc] & (pt[:LANES] + off < need)))
        off = off + pt[LANES:]
    return sels


def _dsa_prompt_tile(nc, topk, i, qt_ref, idxt_ref, kb_ref, vt_ref, kib_ref, o_ref,
                     keys_scr, madd_scr, bias_scr):
    w = nc * LANES
    row = lax.broadcasted_iota(I32, (LANES, LANES), 0)
    lane = lax.broadcasted_iota(I32, (LANES, LANES), 1)
    lo = row < DH_B
    idxt = idxt_ref[0]
    kib = kib_ref[0, 0:w, :]

    def head_pair_rhs(pair_t):
        zero = jnp.zeros_like(pair_t)
        return jnp.concatenate([jnp.where(lo, pair_t, zero), jnp.where(lo, zero, pair_t)], axis=1)

    score = None
    for j in range(H_IDX // 2):
        st = _dot(kib, head_pair_rhs(idxt[j * LANES:(j + 1) * LANES, :]).astype(BF16))
        for half in range(2):
            h = 2 * j + half
            wh = idxt[IDX_WI + h:IDX_WI + h + 1, :] * D_IDX ** -0.5
            term = wh * jnp.maximum(st[:, half * LANES:(half + 1) * LANES], 0.0)
            score = term if score is None else score + term
    qpos = i * LANES + lane
    valid = [(c * LANES + row) <= qpos for c in range(nc)]
    for c in range(nc):
        keys_scr[c] = _sortable_key(jnp.where(valid[c], score[c * LANES:(c + 1) * LANES, :], NEG_INF))
    sels = _select_topk_keymajor(keys_scr, nc, topk)
    for c in range(nc):
        madd_scr[c] = jnp.where(sels[c] & valid[c], 0.0, NEG_INF)

    for j in range(H_B // 2):
        logit2 = _dot(kb_ref[0, 0:w, j * LANES:(j + 1) * LANES],
                      head_pair_rhs(qt_ref[0, j * LANES:(j + 1) * LANES, :]))
        ps, dens = [], []
        for half in range(2):
            h = 2 * j + half
            pieces = []
            for c in range(nc):
                lc = logit2[c * LANES:(c + 1) * LANES, half * LANES:(half + 1) * LANES] + madd_scr[c]
                if c >= nc - DSA_CHUNKS_PER_VARIANT - 1:
                    lc = lc + bias_scr[h, jnp.clip(i - c, 0, 2)]
                pieces.append(lc)
            logit = jnp.concatenate(pieces, axis=0)
            p = jnp.exp(logit - jnp.max(logit, axis=0, keepdims=True))
            dens.append(jnp.sum(p, axis=0, keepdims=True))
            ps.append(p.astype(BF16))
        pv = _dot(vt_ref[0, j * LANES:(j + 1) * LANES, 0:w], jnp.concatenate(ps, axis=1))
        ot = jnp.where(lo, pv[:, :LANES] / dens[0], pv[:, LANES:] / dens[1])
        o_ref[0, :, j * LANES:(j + 1) * LANES] = ot.T


def _dsa_prompt_body(topk, relb_ref, qt_ref, idxt_ref, kb_ref, vt_ref, kib_ref, o_ref,
                     keys_scr, madd_scr, bias_scr):
    b = pl.program_id(0)
    i = pl.program_id(1)
    ncs = kb_ref.shape[1] // LANES

    @pl.when((b == 0) & (i == 0))
    def _():
        r = lax.broadcasted_iota(I32, (LANES, LANES), 0)
        c = lax.broadcasted_iota(I32, (LANES, LANES), 1)
        for h in range(H_B):
            far = relb_ref[N_BUCKETS - 1, h]
            bias_scr[h, 0] = _bias_of_dist(c - r, relb_ref, h) - far
            bias_scr[h, 1] = _bias_of_dist(LANES + c - r, relb_ref, h) - far
            bias_scr[h, 2] = jnp.zeros((LANES, LANES), F32)

    nvar = -(-ncs // DSA_CHUNKS_PER_VARIANT)
    for var in range(nvar):
        nc = min((var + 1) * DSA_CHUNKS_PER_VARIANT, ncs)

        @pl.when(i // DSA_CHUNKS_PER_VARIANT == var)
        def _(nc=nc):
            _dsa_prompt_tile(nc, topk, i, qt_ref, idxt_ref, kb_ref, vt_ref, kib_ref, o_ref,
                             keys_scr, madd_scr, bias_scr)


def _dsa_prompt(rel_bias, qt3, idxt3, kb3, vt3, kib3):
    b, s, _ = kb3.shape
    assert s % TQ == 0 and TQ == LANES
    topk = min(TOPK_MAX, s // 4)
    tspec = lambda w: pl.BlockSpec((1, w, TQ), lambda i, j: (i, 0, j))
    kspec = lambda w: pl.BlockSpec((1, s, w), lambda i, j: (i, 0, 0))
    ncs = s // LANES
    return pl.pallas_call(
        functools.partial(_dsa_prompt_body, topk),
        grid=(b, s // TQ),
        in_specs=[pl.BlockSpec(memory_space=pltpu.SMEM),
                  tspec(D_B), tspec(IDX_W), kspec(D_B), pl.BlockSpec((1, D_B, s), lambda i, j: (i, 0, 0)),
                  kspec(LANES)],
        out_specs=pl.BlockSpec((1, TQ, D_B), lambda i, j: (i, j, 0)),
        out_shape=jax.ShapeDtypeStruct((b, s, D_B), F32),
        scratch_shapes=[pltpu.VMEM((ncs, LANES, LANES), I32), pltpu.VMEM((ncs, LANES, LANES), F32),
                        pltpu.VMEM((H_B, 3, LANES, LANES), F32)],
        compiler_params=_cparams(2),
        name="dsa_prompt",
    )(rel_bias, qt3, idxt3, kb3, vt3, kib3)


class _PagedPrefetch:
    def __init__(self, pt_ref, caches, pages_per_step, steps_per_seq):
        self.pt_ref, self.caches = pt_ref, caches
        self.n, self.steps = pages_per_step, steps_per_seq

    def _copies(self, seq, group, slot):
        out = []
        for p in range(self.n):
            page = self.pt_ref[seq, group * self.n + p]
            for hbm, buf, sem in self.caches:
                out.append(pltpu.make_async_copy(hbm.at[0, page], buf.at[slot, p], sem.at[slot]))
        return out

    def advance(self, seq, group, n_seq):
        step = seq * self.steps + group
        slot = step & 1

        @pl.when(step == 0)
        def _():
            for cp in self._copies(seq, group, slot):
                cp.start()

        wrap = group + 1 == self.steps
        nseq = jnp.where(wrap, seq + 1, seq)
        ngroup = jnp.where(wrap, 0, group + 1)

        @pl.when(nseq < n_seq)
        def _():
            for cp in self._copies(nseq, ngroup, 1 - slot):
                cp.start()

        for cp in self._copies(seq, group, slot):
            cp.wait()
        return slot


def _dsa_score_body(n_pages, pt_ref, idx_ref, kidx_hbm, keys_ref, knew_ref, kbuf, ksem, kcat_scr):
    b = pl.program_id(0)
    slot = _PagedPrefetch(pt_ref, [(kidx_hbm, kbuf, ksem)], n_pages, 1).advance(b, 0, pl.num_programs(0))
    t = idx_ref.shape[1]
    idx = idx_ref[0]
    qrows = jnp.concatenate([idx[:, h * D_IDX:(h + 1) * D_IDX] for h in range(H_IDX)], axis=0).astype(BF16)

    def score_of(qk):
        sh = jnp.maximum(qk * D_IDX ** -0.5, 0.0)
        out = None
        for h in range(H_IDX):
            term = idx[:, IDX_WI + h:IDX_WI + h + 1] * sh[h * t:(h + 1) * t]
            out = term if out is None else out + term
        return out

    for g in range(n_pages):
        kcat_scr[:, g * PAGE_SIZE:(g + 1) * PAGE_SIZE] = kbuf[slot, g].astype(BF16)
    score = score_of(_dot(qrows, kcat_scr[...]))
    for g in range(n_pages):
        keys_ref[g] = _sortable_key(score[:, g * PAGE_SIZE:(g + 1) * PAGE_SIZE])

    knew = jnp.concatenate([idx[:, IDX_KI:IDX_KI + D_IDX], jnp.zeros((LANES - t, D_IDX), F32)], axis=0)
    sn = score_of(_dot_nt(qrows, knew.astype(BF16)))
    tq = lax.broadcasted_iota(I32, (t, LANES), 0)
    tk = lax.broadcasted_iota(I32, (t, LANES), 1)
    knew_ref[...] = _sortable_key(jnp.where(tk <= tq, sn, NEG_INF))


def _dsa_pick_body(topk, t, keys_ref, knew_ref, sel_ref, keys_scr):
    nc = keys_ref.shape[0]
    rows = keys_ref.shape[1]
    keys_scr[0:nc] = keys_ref[...]
    keys_scr[nc] = knew_ref[...]
    sel = _select_topk(keys_scr, nc + 1, rows, topk, 1)
    sel_ref[0:nc] = jnp.where(sel[0:nc], 1.0, 0.0)
    tq = lax.broadcasted_iota(I32, (rows, LANES), 0) & (t - 1)
    tk = lax.broadcasted_iota(I32, (rows, LANES), 1)
    sel_ref[nc] = jnp.where(sel[nc] & (tk <= tq), 1.0, 0.0)


def _dsa_select(page_table, idx3, cache_kidx):
    db, t, _ = idx3.shape
    n_pages = page_table.shape[1]
    assert t & (t - 1) == 0
    topk = min(TOPK_MAX, (n_pages * PAGE_SIZE + t) // 4)
    grid_spec = pltpu.PrefetchScalarGridSpec(
        num_scalar_prefetch=1,
        grid=(db,),
        in_specs=[pl.BlockSpec((1, t, IDX_W), lambda b, pt: (b, 0, 0)), pl.BlockSpec(memory_space=pl.ANY)],
        out_specs=[pl.BlockSpec((n_pages, t, LANES), lambda b, pt: (0, b, 0)),
                   pl.BlockSpec((t, LANES), lambda b, pt: (b, 0))],
        scratch_shapes=[pltpu.VMEM((2, n_pages, D_IDX, PAGE_SIZE), F32), pltpu.SemaphoreType.DMA((2,)),
                        pltpu.VMEM((D_IDX, n_pages * PAGE_SIZE), BF16)],
    )
    keys, knew = pl.pallas_call(
        functools.partial(_dsa_score_body, n_pages),
        grid_spec=grid_spec,
        out_shape=[jax.ShapeDtypeStruct((n_pages, db * t, LANES), I32),
                   jax.ShapeDtypeStruct((db * t, LANES), I32)],
        compiler_params=_cparams(1),
        name="dsa_score",
    )(page_table, idx3, cache_kidx)
    rows = min(R_PICK, db * t)
    return pl.pallas_call(
        functools.partial(_dsa_pick_body, topk, t),
        grid=(db * t // rows,),
        in_specs=[pl.BlockSpec((n_pages, rows, LANES), lambda r: (0, r, 0)),
                  pl.BlockSpec((rows, LANES), lambda r: (r, 0))],
        out_specs=pl.BlockSpec((n_pages + 1, rows, LANES), lambda r: (0, r, 0)),
        out_shape=jax.ShapeDtypeStruct((n_pages + 1, db * t, LANES), F32),
        scratch_shapes=[pltpu.VMEM((n_pages + 1, rows, LANES), I32)],
        compiler_params=_cparams(1),
        name="dsa_pick",
    )(keys, knew)


def _dsa_sample_body(g_pages, pt_ref, relb_ref, qb_ref, sel_ref, knew_ref, vnew_ref, k_hbm, v_hbm, o_ref,
                     kbuf, vbuf, ksem, vsem, qbd_scr, kcat_scr, vcat_scr, bias_scr, m_scr, l_scr, acc_scr):
    b = pl.program_id(0)
    c = pl.program_id(1)
    nsteps = pl.num_programs(1)
    t = qb_ref.shape[1]
    slot = _PagedPrefetch(pt_ref, [(k_hbm, kbuf, ksem), (v_hbm, vbuf, vsem)], g_pages, nsteps).advance(
        b, c, pl.num_programs(0))

    @pl.when((b == 0) & (c == 0))
    def _():
        tq = lax.broadcasted_iota(I32, (t, LANES), 0)
        col = lax.broadcasted_iota(I32, (t, LANES), 1)
        for h in range(H_B):
            rs = slice(h * t, (h + 1) * t)
            bias_scr[0, rs, :] = jnp.full((t, LANES), relb_ref[N_BUCKETS - 1, h], F32)
            bias_scr[1, rs, :] = _bias_of_dist(tq + PAGE_SIZE - col, relb_ref, h)
            bias_scr[2, rs, :] = _bias_of_dist(tq - col, relb_ref, h)

    @pl.when(c == 0)
    def _():
        q = qb_ref[0] * DH_B ** -0.5
        lane = lax.broadcasted_iota(I32, (t, D_B), 1)
        qbd = [jnp.where((lane >= h * DH_B) & (lane < (h + 1) * DH_B), q, 0.0) for h in range(H_B)]
        qbd_scr[...] = jnp.concatenate(qbd, axis=0).astype(BF16)
        m_scr[...] = jnp.full(m_scr.shape, NEG_INF, F32)
        l_scr[...] = jnp.zeros(l_scr.shape, F32)
        acc_scr[...] = jnp.zeros(acc_scr.shape, F32)

    def update(logit, sel, vmat, v_key_minor):
        selr = jnp.concatenate([sel] * H_B, axis=0) > 0.5
        lm = jnp.where(selr, logit, NEG_INF)
        m_old = m_scr[...]
        m_new = jnp.maximum(m_old, jnp.max(lm, axis=1, keepdims=True))
        m_safe = jnp.where(m_new == NEG_INF, 0.0, m_new)
        alpha = jnp.exp(m_old - m_safe)
        p = jnp.exp(lm - m_safe)
        l_scr[...] = alpha * l_scr[...] + jnp.sum(p, axis=1, keepdims=True)
        pv = _dot_nt(p.astype(BF16), vmat) if v_key_minor else _dot(p.astype(BF16), vmat)
        acc_scr[...] = alpha * acc_scr[...] + pv
        m_scr[...] = m_new

    for g in range(g_pages):
        kcat_scr[:, g * PAGE_SIZE:(g + 1) * PAGE_SIZE] = kbuf[slot, g].astype(BF16)
        vcat_scr[:, g * PAGE_SIZE:(g + 1) * PAGE_SIZE] = vbuf[slot, g].astype(BF16)
    logit = _dot(qbd_scr[...], kcat_scr[...])
    far = bias_scr[0]
    last = jnp.where(c == nsteps - 1, bias_scr[1], far)
    bias = jnp.concatenate([far] * (g_pages - 1) + [last], axis=1)
    sel = jnp.concatenate([sel_ref[c * g_pages + g] for g in range(g_pages)], axis=1)
    update(logit + bias, sel, vcat_scr[...], True)

    @pl.when(c == nsteps - 1)
    def _():
        pad = jnp.zeros((LANES - t, D_B), F32)
        knp = jnp.concatenate([knew_ref[0], pad], axis=0).astype(BF16)
        vnp = jnp.concatenate([vnew_ref[0], pad], axis=0).astype(BF16)
        ln = _dot_nt(qbd_scr[...], knp) + bias_scr[2]
        update(ln, sel_ref[sel_ref.shape[0] - 1], vnp, False)
        out = acc_scr[...] / l_scr[...]
        lane = lax.broadcasted_iota(I32, (t, D_B), 1)
        o = jnp.zeros((t, D_B), F32)
        for h in range(H_B):
            o = jnp.where((lane >= h * DH_B) & (lane < (h + 1) * DH_B), out[h * t:(h + 1) * t], o)
        o_ref[0] = o


def _dsa_sample(page_table, rel_bias, qb3, sel, k3, v3, cache_k, cache_v):
    db, t, _ = qb3.shape
    n_pages = page_table.shape[1]
    g = min(G_DSA, n_pages)
    assert n_pages % g == 0
    rows = H_B * t
    seq = lambda w: pl.BlockSpec((1, t, w), lambda b, c, pt: (b, 0, 0))
    sel_spec = pl.BlockSpec((sel.shape[0], t, LANES), lambda b, c, pt: (0, b, 0))
    grid_spec = pltpu.PrefetchScalarGridSpec(
        num_scalar_prefetch=1,
        grid=(db, n_pages // g),
        in_specs=[pl.BlockSpec(memory_space=pltpu.SMEM), seq(D_B), sel_spec, seq(D_B), seq(D_B),
                  pl.BlockSpec(memory_space=pl.ANY), pl.BlockSpec(memory_space=pl.ANY)],
        out_specs=seq(D_B),
        scratch_shapes=[pltpu.VMEM((2, g, D_B, PAGE_SIZE), F32), pltpu.VMEM((2, g, D_B, PAGE_SIZE), F32),
                        pltpu.SemaphoreType.DMA((2,)), pltpu.SemaphoreType.DMA((2,)),
                        pltpu.VMEM((rows, D_B), BF16),
                        pltpu.VMEM((D_B, g * PAGE_SIZE), BF16), pltpu.VMEM((D_B, g * PAGE_SIZE), BF16),
                        pltpu.VMEM((3, rows, LANES), F32),
                        pltpu.VMEM((rows, 1), F32), pltpu.VMEM((rows, 1), F32), pltpu.VMEM((rows, D_B), F32)],
    )
    return pl.pallas_call(
        functools.partial(_dsa_sample_body, g),
        grid_spec=grid_spec,
        out_shape=jax.ShapeDtypeStruct((db, t, D_B), F32),
        compiler_params=_cparams(2),
        name="dsa_sample",
    )(page_table, rel_bias, qb3, sel, k3, v3, cache_k, cache_v)


def _ones_beyond_rope(kpe_slab):
    lane = lax.broadcasted_iota(I32, kpe_slab.shape, 1)
    return jnp.where(lane < ROPE_C, kpe_slab, 1.0)


def _rms(x, g):
    return x * lax.rsqrt(jnp.mean(x * x, axis=1, keepdims=True) + RMS_EPS) * g


def _odd_proj_body(x_ref, wa_ref, wb_ref, wbr_ref, wg_ref, wq_ref, wqr_ref, qn_ref, kvn_ref, cos_ref, sin_ref,
                   qcat_ref, kvpe_ref, kvpeb_ref, g_ref):
    xb = x_ref[...].astype(BF16)
    cos_k, sin_k = cos_ref[:, :LANES], sin_ref[:, :LANES]
    cos_q, sin_q = cos_ref[:, LANES:], sin_ref[:, LANES:]
    g_ref[...] = _dot(xb, wg_ref[...])
    kv = _dot(xb, wb_ref[...])
    kvr = _dot(xb, wbr_ref[...])
    ckvn = _rms(kv[:, :KV_LORA], kvn_ref[...])
    kpe = kv[:, KV_LORA:] * cos_k + kvr[:, KV_LORA:] * sin_k
    kvpe = jnp.concatenate([ckvn, kpe], axis=1)
    kvpe_ref[...] = kvpe
    kvpeb_ref[...] = kvpe.astype(BF16)
    cqn = _rms(_dot(xb, wa_ref[...]), qn_ref[...]).astype(BF16)
    qc = _dot(cqn, wq_ref[...])
    qr = _dot(cqn, wqr_ref[...])
    for h in range(H_C):
        sl = slice(h * LANES, (h + 1) * LANES)
        qcat_ref[:, sl] = (qc[:, sl] * cos_q + qr[:, sl] * sin_q).astype(qcat_ref.dtype)


def _odd_proj(x2d, wa, wb, wbr, wg, wq, wqr, qn, kvn, cos_t, sin_t, qcat_dtype):
    m = x2d.shape[0]
    tm = min(TM_PROJ, m)
    nt = cos_t.shape[0] // tm
    row = lambda w: pl.BlockSpec((tm, w), lambda i: (i, 0))
    full = lambda a: pl.BlockSpec(a.shape, lambda i: (0, 0))
    tab = pl.BlockSpec((tm, 2 * LANES), lambda i: (i % nt, 0))
    f = lambda w, dt: jax.ShapeDtypeStruct((m, w), dt)
    return pl.pallas_call(
        _odd_proj_body,
        grid=(m // tm,),
        in_specs=[row(D_MODEL)] + [full(a) for a in (wa, wb, wbr, wg, wq, wqr, qn, kvn)] + [tab, tab],
        out_specs=[row(QCAT_W), row(KVPE_W), row(KVPE_W), row(D_C)],
        out_shape=[f(QCAT_W, qcat_dtype), f(KVPE_W, F32), f(KVPE_W, BF16), f(D_C, F32)],
        compiler_params=_cparams(1),
        name="odd_proj",
    )(x2d, wa, wb, wbr, wg, wq, wqr, qn, kvn, cos_t, sin_t)


def _mla_expand_q(qcat, wexp_ref):
    rows = [_dot(qcat[:, h * LANES:(h + 1) * LANES].astype(BF16), wexp_ref[h]) for h in range(H_C)]
    return jnp.concatenate(rows, axis=0).astype(BF16)


def _mla_softmax_step(s, m_old, vmat):
    m_new = jnp.maximum(m_old, jnp.max(s, axis=1, keepdims=True))
    alpha = jnp.exp2((m_old - m_new) * MLA_EXP2_SCALE)
    m_wide = jnp.concatenate([m_new] * (s.shape[1] // LANES), axis=1)
    p = jnp.exp2((s - m_wide) * MLA_EXP2_SCALE)
    pv = _dot(p.astype(BF16), vmat)
    return m_new, jnp.concatenate([alpha] * (KVPE_W // LANES), axis=1), pv


def _mla_head_pair_out(acc_a, acc_b, wuv_pair):
    lat = [a[:, :KV_LORA] / a[:, MLA_DEN_LANE:MLA_DEN_LANE + 1] for a in (acc_a, acc_b)]
    return _dot(jnp.concatenate(lat, axis=1).astype(BF16), wuv_pair)


def _mla_prompt_body(qcat_ref, kv_ref, wk_ref, wv_ref, o_ref, kx_scr, vx_scr, m_scr, acc_scr, mask_scr):
    i = pl.program_id(1)
    tq = qcat_ref.shape[1]
    s = kv_ref.shape[1]

    @pl.when(i == 0)
    def _():
        lane = lax.broadcasted_iota(I32, (tq, QCAT_W), 1)
        ones = (((lane >> 7) ^ (lane >> 6)) & 1) == 1
        for r in range(s // tq):
            rs = slice(r * tq, (r + 1) * tq)
            kv = kv_ref[0, rs, :]
            kx_scr[rs, :] = _dot(kv, wk_ref[...]).astype(BF16)
            vx_scr[rs, :] = jnp.where(ones, 1.0, _dot(kv[:, :KV_LORA], wv_ref[...])).astype(BF16)

    @pl.when((pl.program_id(0) == 0) & (i == 0))
    def _():
        row = lax.broadcasted_iota(I32, (tq, tq), 0)
        col = lax.broadcasted_iota(I32, (tq, tq), 1)
        mask_scr[0] = jnp.zeros((tq, tq), F32)
        mask_scr[1] = jnp.where(col <= row, 0.0, NEG_INF)

    m_scr[...] = jnp.full(m_scr.shape, NEG_INF, F32)
    acc_scr[...] = jnp.zeros(acc_scr.shape, F32)

    def chunk(c, carry):
        k0 = pl.multiple_of(c * tq, tq)
        mask = mask_scr[(c == i).astype(I32)]
        for h in range(H_C):
            hs = slice(h * LANES, (h + 1) * LANES)
            rs = slice(h * tq, (h + 1) * tq)
            sc = _dot_nt(qcat_ref[0, :, hs], kx_scr[pl.ds(k0, tq), hs]) + mask
            m_old = m_scr[rs, :]
            m_new = jnp.maximum(m_old, jnp.max(sc, axis=1, keepdims=True))
            alpha = jnp.exp2((m_old - m_new) * MLA_EXP2_SCALE)
            p = jnp.exp2((sc - jnp.concatenate([m_new] * (tq // LANES), axis=1)) * MLA_EXP2_SCALE)
            acc_scr[rs, :] = alpha * acc_scr[rs, :] + _dot(p.astype(BF16), vx_scr[pl.ds(k0, tq), hs])
            m_scr[rs, :] = m_new
        return carry

    lax.fori_loop(0, i + 1, chunk, 0)
    lo = lax.broadcasted_iota(I32, (tq, LANES), 1) < DV_C
    for j in range(H_C // 2):
        ae = acc_scr[2 * j * tq:(2 * j + 1) * tq, :]
        ao = acc_scr[(2 * j + 1) * tq:(2 * j + 2) * tq, :]
        o_ref[0, :, j * LANES:(j + 1) * LANES] = jnp.where(
            lo, ae / pltpu.roll(ae, DV_C, 1), ao / pltpu.roll(ao, DV_C, 1))


def _mla_prompt(qcat3, kvpeb3, wk2, wv2):
    b, s, _ = qcat3.shape
    tq = min(TQ_MLA, s)
    assert s % tq == 0 and tq % LANES == 0
    rows = H_C * tq
    return pl.pallas_call(
        _mla_prompt_body,
        grid=(b, s // tq),
        in_specs=[pl.BlockSpec((1, tq, QCAT_W), lambda i, j: (i, j, 0)),
                  pl.BlockSpec((1, s, KVPE_W), lambda i, j: (i, 0, 0)),
                  pl.BlockSpec(wk2.shape, lambda i, j: (0, 0)),
                  pl.BlockSpec(wv2.shape, lambda i, j: (0, 0))],
        out_specs=pl.BlockSpec((1, tq, D_C), lambda i, j: (i, j, 0)),
        out_shape=jax.ShapeDtypeStruct((b, s, D_C), F32),
        scratch_shapes=[pltpu.VMEM((s, QCAT_W), BF16), pltpu.VMEM((s, QCAT_W), BF16),
                        pltpu.VMEM((rows, LANES), F32), pltpu.VMEM((rows, LANES), F32),
                        pltpu.VMEM((2, tq, tq), F32)],
        compiler_params=_cparams(2),
        name="mla_prompt",
    )(qcat3, kvpeb3, wk2, wv2)


def _mla_sample_body(g_pages, pt_ref, qcat_ref, knew_ref, wexp_ref, wuv_ref, ckv_hbm, kpe_hbm, o_ref,
                     cbuf, pbuf, csem, psem, qall_scr, ccat_scr, pcat_scr, m_scr, acc_scr):
    b = pl.program_id(0)
    c = pl.program_id(1)
    nsteps = pl.num_programs(1)
    t = qcat_ref.shape[1]
    slot = _PagedPrefetch(pt_ref, [(ckv_hbm, cbuf, csem), (kpe_hbm, pbuf, psem)], g_pages, nsteps).advance(
        b, c, pl.num_programs(0))

    @pl.when((b == 0) & (c == 0))
    def _():
        pcat_scr[...] = jnp.zeros(pcat_scr.shape, BF16)
        ccat_scr[:, KV_LORA:] = jnp.ones((ccat_scr.shape[0], KVPE_W - KV_LORA), BF16)

    @pl.when(c == 0)
    def _():
        qall_scr[...] = _mla_expand_q(qcat_ref[0], wexp_ref)
        m_scr[...] = jnp.full(m_scr.shape, NEG_INF, F32)
        acc_scr[...] = jnp.zeros(acc_scr.shape, F32)

    def update(s, vmat):
        m_new, alpha, pv = _mla_softmax_step(s, m_scr[...], vmat)
        acc_scr[...] = alpha * acc_scr[...] + pv
        m_scr[...] = m_new

    q = qall_scr[...]
    sub = min(MLA_SUB_PAGES, g_pages)
    m_run = m_scr[...]
    acc_run = acc_scr[...]
    nblk = g_pages // sub

    def scores(blk):
        rs = slice(blk * sub * PAGE_SIZE, (blk + 1) * sub * PAGE_SIZE)
        for g in range(blk * sub, (blk + 1) * sub):
            ccat_scr[g * PAGE_SIZE:(g + 1) * PAGE_SIZE, 0:KV_LORA] = cbuf[slot, g].astype(BF16)
            pcat_scr[0:ROPE_C, g * PAGE_SIZE:(g + 1) * PAGE_SIZE] = pbuf[slot, g].astype(BF16)
        return _dot_nt(q[:, :KV_LORA], ccat_scr[rs, 0:KV_LORA]) + _dot(q[:, KV_LORA:], pcat_scr[:, rs])

    s_next = scores(0)
    for blk in range(nblk):
        s = s_next
        if blk + 1 < nblk:
            s_next = scores(blk + 1)
        rs = slice(blk * sub * PAGE_SIZE, (blk + 1) * sub * PAGE_SIZE)
        m_run, alpha, pv = _mla_softmax_step(s, m_run, ccat_scr[rs, :])
        acc_run = alpha * acc_run + pv
    m_scr[...] = m_run
    acc_scr[...] = acc_run

    @pl.when(c == nsteps - 1)
    def _():
        kn = knew_ref[0]
        kn = jnp.concatenate([kn[:, :KV_LORA], _ones_beyond_rope(kn[:, KV_LORA:])], axis=1)
        knp = jnp.concatenate([kn, jnp.zeros((LANES - t, KVPE_W), F32)], axis=0).astype(BF16)
        sn = _dot_nt(q, knp)
        r = lax.broadcasted_iota(I32, (H_C * t, LANES), 0)
        col = lax.broadcasted_iota(I32, (H_C * t, LANES), 1)
        sn = jnp.where(col <= (r & (t - 1)), sn, NEG_INF)
        update(sn, knp)
        acc = acc_scr[...]
        outs = [_mla_head_pair_out(acc[2 * j * t:(2 * j + 1) * t], acc[(2 * j + 1) * t:(2 * j + 2) * t], wuv_ref[j])
                for j in range(H_C // 2)]
        o_ref[0] = jnp.concatenate(outs, axis=1)


def _mla_sample(page_table, qcat3, kvpeb3, wexp, wuvp, cache_ckv, cache_kpe):
    db, t, _ = qcat3.shape
    n_pages = page_table.shape[1]
    g = min(G_MLA, n_pages)
    assert n_pages % g == 0
    rows = H_C * t
    grid_spec = pltpu.PrefetchScalarGridSpec(
        num_scalar_prefetch=1,
        grid=(db, n_pages // g),
        in_specs=[pl.BlockSpec((1, t, QCAT_W), lambda b, c, pt: (b, 0, 0)),
                  pl.BlockSpec((1, t, KVPE_W), lambda b, c, pt: (b, 0, 0)),
                  pl.BlockSpec(wexp.shape, lambda b, c, pt: (0, 0, 0)),
                  pl.BlockSpec(wuvp.shape, lambda b, c, pt: (0, 0, 0)),
                  pl.BlockSpec(memory_space=pl.ANY), pl.BlockSpec(memory_space=pl.ANY)],
        out_specs=pl.BlockSpec((1, t, D_C), lambda b, c, pt: (b, 0, 0)),
        scratch_shapes=[pltpu.VMEM((2, g, PAGE_SIZE, KV_LORA), F32), pltpu.VMEM((2, g, ROPE_C, PAGE_SIZE), F32),
                        pltpu.SemaphoreType.DMA((2,)), pltpu.SemaphoreType.DMA((2,)),
                        pltpu.VMEM((rows, KVPE_W), BF16),
                        pltpu.VMEM((g * PAGE_SIZE, KVPE_W), BF16), pltpu.VMEM((LANES, g * PAGE_SIZE), BF16),
                        pltpu.VMEM((rows, LANES), F32), pltpu.VMEM((rows, KVPE_W), F32)],
    )
    return pl.pallas_call(
        functools.partial(_mla_sample_body, g),
        grid_spec=grid_spec,
        out_shape=jax.ShapeDtypeStruct((db, t, D_C), F32),
        compiler_params=_cparams(2),
        name="mla_sample",
    )(page_table, qcat3, kvpeb3, wexp, wuvp, cache_ckv, cache_kpe)


def _rope_cos_sin(pos):
    inv = ROPE_THETA ** (-jnp.arange(ROPE_HALF, dtype=F32) / ROPE_HALF)
    ang = pos.astype(F32)[:, None] * inv
    return jnp.cos(ang), jnp.sin(ang)


def _rope_lane_pattern(width, period, rot_start, limit):
    lane = np.arange(width)
    d = lane % period - rot_start
    inside = lane < limit
    x1 = inside & (d >= 0) & (d < ROPE_HALF)
    x2 = inside & (d >= ROPE_HALF) & (d < 2 * ROPE_HALF)
    return x1, x2


def _rope_tables(cos, sin, x1, x2, base):
    width = x1.shape[0]
    reps = width // ROPE_HALF
    cos_w = jnp.tile(cos, (1, reps))
    sin_w = jnp.tile(sin, (1, reps))
    cos_t = jnp.where(x1 | x2, cos_w, jnp.asarray(base, F32)[None, :])
    sin_t = jnp.where(x1, -sin_w, jnp.where(x2, sin_w, 0.0))
    return cos_t, sin_t


def _partner_columns(w, x1, x2):
    lane = np.arange(w.shape[1])
    src = lane + ROPE_HALF * x1 - ROPE_HALF * x2
    return jnp.where((x1 | x2)[None, :], w[:, src], 0.0)


def _tile_rows(tab, reps):
    return jnp.tile(tab, (reps, 1))


def kernel(x_prompt, x_sample, state_pool, cache_k_b, cache_v_b, cache_kidx_b, cache_ckv, cache_kpe, page_table, w_in_even, pool_w, pool_scale, w_out_even, rel_bias, w_in_odd, q_norm, w_q_b, kv_norm, w_uk, w_uv, w_out_odd, ln_g, ln_b):
    bsz, seq, _ = x_prompt.shape
    db, t, _ = x_sample.shape
    n_pages = page_table.shape[1]
    past = n_pages * PAGE_SIZE
    mp, ms = bsz * seq, db * t
    assert w_in_even.shape[0] == 1 and w_in_odd.shape[0] == 1 and t == 8

    cos_p, sin_p = _rope_cos_sin(jnp.arange(seq))
    cos_s, sin_s = _rope_cos_sin(past + jnp.arange(t))

    we = w_in_even[0]
    n_main = 6 * D_A
    wm = we[:, :n_main].astype(BF16)
    w_qi = we[:, n_main:n_main + H_IDX * D_IDX]
    w_ki = we[:, n_main + H_IDX * D_IDX:n_main + H_IDX * D_IDX + D_IDX]
    w_wi = we[:, n_main + H_IDX * D_IDX + D_IDX:]
    ws32 = jnp.concatenate([w_qi, w_ki, w_ki, w_wi, jnp.zeros((D_MODEL, IDX_W - IDX_WI - H_IDX), F32)], axis=1)
    x1e, x2e = _rope_lane_pattern(IDX_W, D_IDX, 0, IDX_WI)
    base_e = np.where(np.arange(IDX_W) < IDX_WI, 1.0, np.where(np.arange(IDX_W) < IDX_WI + H_IDX, H_IDX ** -0.5, 0.0))
    ws = ws32.astype(BF16)
    wr = _partner_columns(ws32, x1e, x2e).astype(BF16)
    pw = pool_w[0].astype(BF16)
    pscale = pool_scale[0][None, :]
    woe = w_out_even[0].astype(BF16)
    lng0, lnb0 = ln_g[0][None, :], ln_b[0][None, :]
    lng1, lnb1 = ln_g[1][None, :], ln_b[1][None, :]

    def even_tables(cos, sin, reps):
        ct, st = _rope_tables(cos, sin, x1e, x2e, base_e)
        return _tile_rows(ct, reps), _tile_rows(st, reps)

    xp2 = x_prompt.reshape(mp, D_MODEL)
    ct, st = even_tables(cos_p, sin_p, 1)
    u, ga, gb, k_fm, v_fm, idx_fm, qt, vt, kb, kib = _even_proj(xp2, wm, ws, wr, ct, st, seq_len=seq)
    r3 = lambda a: a.reshape(bsz, seq, a.shape[-1])
    a_p = _pool_prompt(r3(u), pw, pscale).reshape(mp, D_A)
    o_p = _dsa_prompt(rel_bias, qt, idx_fm, r3(kb), vt, r3(kib)).reshape(mp, D_B)
    xp1 = _gate_out_ln(xp2, [(a_p, ga), (o_p, gb)], woe, lng0, lnb0, "even_out_prompt")
    pool_p = r3(u)[:, seq - POOL_STATE:][None]
    kb_p = jnp.transpose(k_fm.reshape(bsz, H_B, DH_B, seq), (0, 3, 1, 2))[None]
    vb_p = jnp.transpose(v_fm.reshape(bsz, H_B, DH_B, seq), (0, 3, 1, 2))[None]
    ki_p = jnp.transpose(idx_fm[:, IDX_KI:IDX_KI + D_IDX, :], (0, 2, 1))[None]

    xs2 = x_sample.reshape(ms, D_MODEL)
    tm_s = min(TM_PROJ, ms)
    ct, st = even_tables(cos_s, sin_s, tm_s // t)
    u, ga, gb, q, k, v, idx = _even_proj(xs2, wm, ws, wr, ct, st)
    s3 = lambda a: a.reshape(db, t, a.shape[-1])
    u_s = s3(u)
    ext = jnp.concatenate([jnp.zeros((db, 16 - POOL_STATE, D_A), F32), state_pool[0], u_s], axis=1)
    a_s = _pool_sample(ext, pw, pscale)
    n_phys = cache_k_b.shape[1]
    kidx_t = jnp.transpose(cache_kidx_b, (0, 1, 3, 2))
    ck4 = jnp.transpose(cache_k_b, (0, 1, 3, 4, 2)).reshape(1, n_phys, D_B, PAGE_SIZE)
    cv4 = jnp.transpose(cache_v_b, (0, 1, 3, 4, 2)).reshape(1, n_phys, D_B, PAGE_SIZE)
    kpe_t = jnp.transpose(cache_kpe, (0, 1, 3, 2))
    sel = _dsa_select(page_table, s3(idx), kidx_t)
    o_s = _dsa_sample(page_table, rel_bias, s3(q), sel, s3(k), s3(v), ck4, cv4).reshape(ms, D_B)
    xs1 = _gate_out_ln(xs2, [(a_s, ga), (o_s, gb)], woe, lng0, lnb0, "even_out_sample")
    pool_s = ext[:, ext.shape[1] - POOL_STATE:][None]
    kb_s = k.reshape(1, db, t, H_B, DH_B)
    vb_s = v.reshape(1, db, t, H_B, DH_B)
    ki_s = s3(idx)[:, :, IDX_KI:IDX_KI + D_IDX][None]

    wo = w_in_odd[0]
    wa = wo[:, :Q_LORA].astype(BF16)
    wb32 = jnp.concatenate([wo[:, Q_LORA:Q_LORA + KV_LORA + ROPE_C],
                            jnp.zeros((D_MODEL, KVPE_W - KV_LORA - ROPE_C), F32)], axis=1)
    x1k, x2k = _rope_lane_pattern(KVPE_W, KVPE_W, KV_LORA, KVPE_W)
    wb = wb32.astype(BF16)
    wbr = _partner_columns(wb32, x1k, x2k).astype(BF16)
    wg = wo[:, Q_LORA + KV_LORA + ROPE_C:].astype(BF16)
    wqb = w_q_b[0]
    wq32 = jnp.concatenate([wqb, jnp.zeros((Q_LORA, H_C, LANES - NOPE_C - ROPE_C), F32)], axis=2)
    wq32 = wq32.reshape(Q_LORA, QCAT_W)
    x1q, x2q = _rope_lane_pattern(QCAT_W, LANES, NOPE_C, QCAT_W)
    wq = wq32.astype(BF16)
    wqr = _partner_columns(wq32, x1q, x2q).astype(BF16)
    qn = q_norm[0][None, :]
    kvn = kv_norm[0][None, :]
    base_k = np.zeros(LANES)
    base_q = np.where(np.arange(LANES) < NOPE_C, 1.0, 0.0)

    def odd_tables(cos, sin, reps):
        ck, sk = _rope_tables(cos, sin, x1k[KV_LORA:], x2k[KV_LORA:], base_k)
        cq, sq = _rope_tables(cos, sin, x1q[:LANES], x2q[:LANES], base_q)
        return (_tile_rows(jnp.concatenate([ck, cq], axis=1), reps),
                _tile_rows(jnp.concatenate([sk, sq], axis=1), reps))

    wuk = w_uk[0]
    wexp = jnp.zeros((H_C, LANES, KVPE_W), F32)
    wexp = wexp.at[:, :NOPE_C, :KV_LORA].set(jnp.transpose(wuk, (1, 2, 0)))
    wexp = wexp.at[:, NOPE_C:NOPE_C + ROPE_C, KV_LORA:KV_LORA + ROPE_C].set(jnp.eye(ROPE_C, dtype=F32)[None])
    wexp = wexp.astype(BF16)
    wuv = jnp.transpose(w_uv[0], (1, 0, 2))
    wuvp = jnp.zeros((H_C // 2, 2 * KV_LORA, 2 * DV_C), F32)
    wuvp = wuvp.at[:, :KV_LORA, :DV_C].set(wuv[0::2])
    wuvp = wuvp.at[:, KV_LORA:, DV_C:].set(wuv[1::2])
    wuvp = wuvp.astype(BF16)
    wk2 = jnp.zeros((KVPE_W, H_C, LANES), F32)
    wk2 = wk2.at[:KV_LORA, :, :NOPE_C].set(wuk)
    wk2 = wk2.at[KV_LORA:KV_LORA + ROPE_C, :, NOPE_C:NOPE_C + ROPE_C].set(
        jnp.broadcast_to(jnp.eye(ROPE_C, dtype=F32)[:, None, :], (ROPE_C, H_C, ROPE_C)))
    wk2 = wk2.reshape(KVPE_W, QCAT_W).astype(BF16)
    wv2 = jnp.zeros((KV_LORA, H_C // 2, 2, LANES), F32)
    wv2 = wv2.at[:, :, 0, :DV_C].set(w_uv[0][:, 0::2])
    wv2 = wv2.at[:, :, 1, DV_C:].set(w_uv[0][:, 1::2])
    wv2 = wv2.reshape(KV_LORA, QCAT_W).astype(BF16)
    woo = w_out_odd[0].astype(BF16)

    ct, st = odd_tables(cos_p, sin_p, 1)
    qcat, kvpe, kvpeb, g1 = _odd_proj(xp1, wa, wb, wbr, wg, wq, wqr, qn, kvn, ct, st, BF16)
    o1 = _mla_prompt(r3(qcat), r3(kvpeb), wk2, wv2).reshape(mp, D_C)
    y_p = _gate_out_ln(xp1, [(o1, g1)], woo, lng1, lnb1, "odd_out_prompt").reshape(bsz, seq, D_MODEL)
    ckv_p = r3(kvpe)[:, :, :KV_LORA][None]
    kpe_p = r3(kvpe)[:, :, KV_LORA:KV_LORA + ROPE_C][None]

    ct, st = odd_tables(cos_s, sin_s, tm_s // t)
    qcat, kvpe, kvpeb, g1 = _odd_proj(xs1, wa, wb, wbr, wg, wq, wqr, qn, kvn, ct, st, F32)
    o1 = _mla_sample(page_table, s3(qcat), s3(kvpe), wexp, wuvp, cache_ckv, kpe_t).reshape(ms, D_C)
    y_s = _gate_out_ln(xs1, [(o1, g1)], woo, lng1, lnb1, "odd_out_sample").reshape(db, t, D_MODEL)
    ckv_s = s3(kvpe)[:, :, :KV_LORA][None]
    kpe_s = s3(kvpe)[:, :, KV_LORA:KV_LORA + ROPE_C][None]

    return (y_p, y_s, pool_p, pool_s, kb_p, kb_s, vb_p, vb_s, ki_p, ki_s, ckv_p, ckv_s, kpe_p, kpe_s)
```

```python
import functools
import math

import numpy as np
import jax
import jax.numpy as jnp
from jax import lax
from jax.experimental import pallas as pl
from jax.experimental.pallas import tpu as pltpu

F32 = jnp.float32
BF16 = jnp.bfloat16
I32 = jnp.int32

D_MODEL = 1024
DEPTH = 2
PAGE_SIZE = 128
D_A = D_MODEL // 2
POOL_WINDOWS = (2, 4, 8, 16)
G_A = D_A // len(POOL_WINDOWS)
POOL_STATE = max(POOL_WINDOWS) - 1
H_B = 8
DH_B = 64
D_B = H_B * DH_B
H_IDX = 4
D_IDX = 64
TOPK_MAX = 256
N_BUCKETS = 32
MAX_DISTANCE = 128
H_C = 16
Q_LORA = 384
KV_LORA = 256
NOPE_C = 64
ROPE_C = 32
DV_C = 64
D_C = H_C * DV_C
MLA_SCALE = (NOPE_C + ROPE_C) ** -0.5
MLA_EXP2_SCALE = MLA_SCALE * math.log2(math.e)
ROPE_THETA = 10000.0
ROPE_HALF = 16
LN_EPS = 1e-5
RMS_EPS = 1e-6
ALPHA = (2 * DEPTH) ** 0.25

LANES = 128
IDX_W = 512
IDX_KI = 256
IDX_WI = 384
KVPE_W = 384
MLA_DEN_LANE = 320
QCAT_W = H_C * LANES

VMEM_LIMIT = 48 * 1024 * 1024

TM_PROJ = 256
TS_POOL = 512
G_POOL = 32
TM_OUT = 256
TQ = 128
TQ_MLA = 256
DSA_CHUNKS_PER_VARIANT = 2
R_PICK = 64
G_DSA = 16
G_MLA = 64
MLA_SUB_PAGES = 8

NEG_INF = float("-inf")
INT_MIN = -(2 ** 31)


def _cparams(n_axes):
    return pltpu.CompilerParams(dimension_semantics=("arbitrary",) * n_axes,
                                vmem_limit_bytes=VMEM_LIMIT)


def _dot(a, b):
    return jnp.dot(a, b, preferred_element_type=F32)


def _dot_nt(a, b):
    return lax.dot_general(a, b, (((1,), (1,)), ((), ())), preferred_element_type=F32)


def _t5_breaks():
    max_d = 2 * MAX_DISTANCE
    d = np.arange(max_d + 1)
    me = N_BUCKETS // 2
    large = me + (np.log(np.maximum(d, 1) / me) / math.log(MAX_DISTANCE / me) * (N_BUCKETS - me)).astype(np.int64)
    large = np.minimum(large, N_BUCKETS - 1)
    bucket = np.where(d < me, d, large)
    assert np.all(bucket[MAX_DISTANCE:] == N_BUCKETS - 1)
    return int(bucket[0]), [(int(i), int(bucket[i])) for i in range(1, max_d + 1) if bucket[i] != bucket[i - 1]]


_T5_FIRST, _T5_BREAKS = _t5_breaks()


def _bias_of_dist(dist, relb_ref, h):
    val = jnp.full(dist.shape, relb_ref[_T5_FIRST, h], F32)
    for p, bk in _T5_BREAKS:
        val = jnp.where(dist >= p, relb_ref[bk, h], val)
    return val


def _sortable_key(score):
    score = jnp.where(score == 0.0, 0.0, score)
    bits = lax.bitcast_convert_type(score, I32)
    return jnp.where(bits < 0, bits ^ jnp.int32(0x7FFFFFFF), bits)


def _kth_largest_key(keys_ref, nc, rows, k, splits):
    rs = rows // splits

    def count_ge(part, cand):
        blk = keys_ref[0:nc, part * rs:(part + 1) * rs, :]
        hit = jnp.where(blk >= cand[None], 1.0, 0.0)
        return jnp.sum(jnp.sum(hit, axis=0), axis=1, keepdims=True)

    def body(it, ts):
        bit = jnp.left_shift(jnp.int32(1), jnp.int32(31) - it)
        out = []
        for part, t in enumerate(ts):
            cand = t + bit
            out.append(jnp.where(count_ge(part, cand) >= k, cand, t))
        return tuple(out)

    init = tuple(jnp.full((rs, 1), INT_MIN, I32) for _ in range(splits))
    ts = lax.fori_loop(0, 32, body, init)
    return ts[0] if splits == 1 else jnp.concatenate(ts, axis=0)


def _select_topk(keys_ref, nc, rows, topk, splits):
    t = _kth_largest_key(keys_ref, nc, rows, topk, splits)[None]
    keys = keys_ref[0:nc]
    gt = keys > t
    eqm = keys == t
    cnt_gt = jnp.sum(jnp.sum(jnp.where(gt, 1.0, 0.0), axis=0), axis=1, keepdims=True)
    need = (topk - cnt_gt)[None]
    r = lax.broadcasted_iota(I32, (LANES, 2 * LANES), 0)
    c = lax.broadcasted_iota(I32, (LANES, 2 * LANES), 1)
    su_ones = jnp.where((r < c) | (c >= LANES), 1.0, 0.0).astype(BF16)
    eq = jnp.where(eqm, 1.0, 0.0).reshape(nc * rows, LANES).astype(BF16)
    pt = _dot(eq, su_ones)
    pre = pt[:, :LANES].reshape(nc, rows, LANES)
    tot = pt[:, LANES:].reshape(nc, rows, LANES)
    offs = []
    off = jnp.zeros((rows, LANES), F32)
    for cc in range(nc):
        offs.append(off)
        off = off + tot[cc]
    before = pre + jnp.stack(offs, axis=0)
    return gt | (eqm & (before < need))


def _silu_gate(v, g):
    return v * (g * (1.0 / (1.0 + jnp.exp(-g))))


def _even_proj_body(prompt, x_ref, wm_ref, ws_ref, wr_ref, cos_ref, sin_ref, u_ref, ga_ref, gb_ref, *outs):
    xb = x_ref[...].astype(BF16)

    def mm(n):
        return _dot(xb, wm_ref[:, n * D_A:(n + 1) * D_A])

    u_ref[...] = mm(0)
    ga_ref[...] = mm(1)
    gb_ref[...] = mm(5)
    q, k, v = mm(2), mm(3), mm(4)
    idx = _dot(xb, ws_ref[...]) * cos_ref[...] + _dot(xb, wr_ref[...]) * sin_ref[...]
    if prompt:
        kt_ref, vt_ref, idxt_ref, qtb_ref, vtb_ref, kb_ref, kib_ref = outs
        vt = v.T
        kt_ref[0] = k.T
        vt_ref[0] = vt
        vtb_ref[0] = vt.astype(BF16)
        idxt_ref[0] = idx.T
        qtb_ref[0] = (q * DH_B ** -0.5).T.astype(BF16)
        kb_ref[...] = k.astype(BF16)
        kib_ref[...] = idx[:, IDX_KI:IDX_KI + LANES].astype(BF16)
    else:
        q_ref, k_ref, v_ref, idx_ref = outs
        q_ref[...] = q
        k_ref[...] = k
        v_ref[...] = v
        idx_ref[...] = idx


def _even_proj(x2d, wm, ws, wr, cos_t, sin_t, seq_len=None):
    m = x2d.shape[0]
    tm = min(TM_PROJ, m)
    nt = cos_t.shape[0] // tm
    row = lambda w: pl.BlockSpec((tm, w), lambda i: (i, 0))
    full = lambda a: pl.BlockSpec(a.shape, lambda i: (0, 0))
    tab = pl.BlockSpec((tm, IDX_W), lambda i: (i % nt, 0))
    f = lambda w, dt: jax.ShapeDtypeStruct((m, w), dt)
    if seq_len is None:
        extra_specs = [row(D_B)] * 3 + [row(IDX_W)]
        extra_shapes = [f(D_B, F32)] * 3 + [f(IDX_W, F32)]
    else:
        spt = seq_len // tm
        fm = lambda w: pl.BlockSpec((1, w, tm), lambda i: (i // spt, 0, i % spt))
        fms = lambda w, dt: jax.ShapeDtypeStruct((m // seq_len, w, seq_len), dt)
        extra_specs = [fm(D_B), fm(D_B), fm(IDX_W), fm(D_B), fm(D_B), row(D_B), row(LANES)]
        extra_shapes = [fms(D_B, F32), fms(D_B, F32), fms(IDX_W, F32), fms(D_B, BF16), fms(D_B, BF16),
                        f(D_B, BF16), f(LANES, BF16)]
    return pl.pallas_call(
        functools.partial(_even_proj_body, seq_len is not None),
        grid=(m // tm,),
        in_specs=[row(D_MODEL), full(wm), full(ws), full(wr), tab, tab],
        out_specs=[row(D_A)] * 3 + extra_specs,
        out_shape=[f(D_A, F32)] * 3 + extra_shapes,
        compiler_params=_cparams(1),
        name="even_proj",
    )(x2d, wm, ws, wr, cos_t, sin_t)


def _pool_mix_group(win_sum, cur, inv_cnt, pw_ref, scale_ref, g):
    pooled = win_sum * inv_cnt - cur
    mixed = _dot(pooled.astype(BF16), pw_ref[g])
    return mixed * scale_ref[:, g * G_A:(g + 1) * G_A]


def _pool_prompt_body(u_ref, halo_ref, pw_ref, scale_ref, a_ref, ext_ref):
    s = pl.program_id(1)
    ts = u_ref.shape[1]
    halo = halo_ref[0]
    ext_ref[0:16, :] = jnp.where(s == 0, 0.0, halo)
    ext_ref[16:, :] = u_ref[0]
    pos = s * ts + lax.broadcasted_iota(I32, (ts, 1), 0)
    for g, w in enumerate(POOL_WINDOWS):
        sl = pl.ds(g * G_A, G_A)
        acc = ext_ref[pl.ds(16, ts), sl]
        for kk in range(1, w):
            acc = acc + ext_ref[pl.ds(16 - kk, ts), sl]
        cnt = jnp.minimum(pos + 1, w).astype(F32)
        a_ref[0, :, g * G_A:(g + 1) * G_A] = _pool_mix_group(
            acc, ext_ref[pl.ds(16, ts), sl], 1.0 / cnt, pw_ref, scale_ref, g)


def _pool_prompt(u3, pw, scale):
    b, s, _ = u3.shape
    ts = min(TS_POOL, s)
    hb = ts // 16
    return pl.pallas_call(
        _pool_prompt_body,
        grid=(b, s // ts),
        in_specs=[pl.BlockSpec((1, ts, D_A), lambda i, j: (i, j, 0)),
                  pl.BlockSpec((1, 16, D_A), lambda i, j: (i, jnp.maximum(j * hb - 1, 0), 0)),
                  pl.BlockSpec(pw.shape, lambda i, j: (0, 0, 0)),
                  pl.BlockSpec(scale.shape, lambda i, j: (0, 0))],
        out_specs=pl.BlockSpec((1, ts, D_A), lambda i, j: (i, j, 0)),
        out_shape=jax.ShapeDtypeStruct((b, s, D_A), F32),
        scratch_shapes=[pltpu.VMEM((ts + 16, D_A), F32)],
        compiler_params=_cparams(2),
        name="pool_prompt",
    )(u3, u3, pw, scale)


def _pool_sample_body(ext_ref, pw_ref, scale_ref, a_ref):
    gs = ext_ref.shape[0]
    t = ext_ref.shape[1] - 16
    for g, w in enumerate(POOL_WINDOWS):
        sl = pl.ds(g * G_A, G_A)
        cur = ext_ref[:, pl.ds(16, t), sl]
        acc = cur
        for kk in range(1, w):
            acc = acc + ext_ref[:, pl.ds(16 - kk, t), sl]
        acc = acc.reshape(gs * t, G_A)
        cur = cur.reshape(gs * t, G_A)
        a_ref[:, g * G_A:(g + 1) * G_A] = _pool_mix_group(acc, cur, 1.0 / w, pw_ref, scale_ref, g)


def _pool_sample(ext, pw, scale):
    db, e, _ = ext.shape
    t = e - 16
    gs = min(G_POOL, db)
    return pl.pallas_call(
        _pool_sample_body,
        grid=(db // gs,),
        in_specs=[pl.BlockSpec((gs, e, D_A), lambda i: (i, 0, 0)),
                  pl.BlockSpec(pw.shape, lambda i: (0, 0, 0)),
                  pl.BlockSpec(scale.shape, lambda i: (0, 0))],
        out_specs=pl.BlockSpec((gs * t, D_A), lambda i: (i, 0)),
        out_shape=jax.ShapeDtypeStruct((db * t, D_A), F32),
        compiler_params=_cparams(1),
        name="pool_sample",
    )(ext, pw, scale)


def _gate_out_ln_body(nparts, x_ref, *refs):
    parts = refs[:2 * nparts]
    w_ref, g_ref, b_ref, y_ref = refs[2 * nparts:]
    hs = [_silu_gate(parts[2 * p][...], parts[2 * p + 1][...]).astype(BF16) for p in range(nparts)]
    h = hs[0] if nparts == 1 else jnp.concatenate(hs, axis=1)
    z = ALPHA * x_ref[...] + _dot(h, w_ref[...])
    mu = jnp.mean(z, axis=1, keepdims=True)
    zc = z - mu
    var = jnp.mean(zc * zc, axis=1, keepdims=True)
    y_ref[...] = zc * lax.rsqrt(var + LN_EPS) * g_ref[...] + b_ref[...]


def _gate_out_ln(x2d, parts, w, ln_g, ln_b, name):
    m = x2d.shape[0]
    tm = min(TM_OUT, m)
    row = lambda a: pl.BlockSpec((tm, a.shape[1]), lambda i: (i, 0))
    full = lambda a: pl.BlockSpec(a.shape, lambda i: (0, 0))
    flat = [a for pair in parts for a in pair]
    return pl.pallas_call(
        functools.partial(_gate_out_ln_body, len(parts)),
        grid=(m // tm,),
        in_specs=[row(x2d)] + [row(a) for a in flat] + [full(w), full(ln_g), full(ln_b)],
        out_specs=row(x2d),
        out_shape=jax.ShapeDtypeStruct(x2d.shape, F32),
        compiler_params=_cparams(1),
        name=name,
    )(x2d, *flat, w, ln_g, ln_b)


def _kth_largest_key_keymajor(keys_ref, nc, k):
    def body(it, t):
        cand = t + jnp.left_shift(jnp.int32(1), jnp.int32(31) - it)
        hit = jnp.where(keys_ref[0:nc] >= cand[None], 1.0, 0.0)
        cnt = jnp.sum(jnp.sum(hit, axis=0), axis=0, keepdims=True)
        return jnp.where(cnt >= k, cand, t)
    return lax.fori_loop(0, 32, body, jnp.full((1, LANES), INT_MIN, I32))


def _select_topk_keymajor(keys_ref, nc, topk):
    t = _kth_largest_key_keymajor(keys_ref, nc, topk)
    r = lax.broadcasted_iota(I32, (2 * LANES, LANES), 0)
    c = lax.broadcasted_iota(I32, (2 * LANES, LANES), 1)
    sl_ones = jnp.where((c < r) | (r >= LANES), 1.0, 0.0).astype(BF16)
    gts, eqs = [], []
    cnt_gt = jnp.zeros((LANES, LANES), F32)
    for cc in range(nc):
        kc = keys_ref[cc]
        gts.append(kc > t)
        eqs.append(kc == t)
        cnt_gt = cnt_gt + jnp.where(gts[cc], 1.0, 0.0)
    need = topk - jnp.sum(cnt_gt, axis=0, keepdims=True)
    sels = []
    off = jnp.zeros((LANES, LANES), F32)
    for cc in range(nc):
        pt = _dot(sl_ones, jnp.where(eqs[cc], 1.0, 0.0).astype(BF16))
        sels.append(gts[cc] | (eqs[cc] & (pt[:LANES] + off < need)))
        off = off + pt[LANES:]
    return sels


def _dsa_prompt_tile(nc, topk, i, qt_ref, idxt_ref, kb_ref, vt_ref, kib_ref, o_ref,
                     keys_scr, madd_scr, bias_scr):
    w = nc * LANES
    row = lax.broadcasted_iota(I32, (LANES, LANES), 0)
    lane = lax.broadcasted_iota(I32, (LANES, LANES), 1)
    lo = row < DH_B
    idxt = idxt_ref[0]
    kib = kib_ref[0, 0:w, :]

    def head_pair_rhs(pair_t):
        zero = jnp.zeros_like(pair_t)
        return jnp.concatenate([jnp.where(lo, pair_t, zero), jnp.where(lo, zero, pair_t)], axis=1)

    score = None
    for j in range(H_IDX // 2):
        st = _dot(kib, head_pair_rhs(idxt[j * LANES:(j + 1) * LANES, :]).astype(BF16))
        for half in range(2):
            h = 2 * j + half
            wh = idxt[IDX_WI + h:IDX_WI + h + 1, :] * D_IDX ** -0.5
            term = wh * jnp.maximum(st[:, half * LANES:(half + 1) * LANES], 0.0)
            score = term if score is None else score + term
    qpos = i * LANES + lane
    valid = [(c * LANES + row) <= qpos for c in range(nc)]
    for c in range(nc):
        keys_scr[c] = _sortable_key(jnp.where(valid[c], score[c * LANES:(c + 1) * LANES, :], NEG_INF))
    sels = _select_topk_keymajor(keys_scr, nc, topk)
    for c in range(nc):
        madd_scr[c] = jnp.where(sels[c] & valid[c], 0.0, NEG_INF)

    for j in range(H_B // 2):
        logit2 = _dot(kb_ref[0, 0:w, j * LANES:(j + 1) * LANES],
                      head_pair_rhs(qt_ref[0, j * LANES:(j + 1) * LANES, :]))
        ps, dens = [], []
        for half in range(2):
            h = 2 * j + half
            pieces = []
            for c in range(nc):
                lc = logit2[c * LANES:(c + 1) * LANES, half * LANES:(half + 1) * LANES] + madd_scr[c]
                if c >= nc - DSA_CHUNKS_PER_VARIANT - 1:
                    lc = lc + bias_scr[h, jnp.clip(i - c, 0, 2)]
                pieces.append(lc)
            logit = jnp.concatenate(pieces, axis=0)
            p = jnp.exp(logit - jnp.max(logit, axis=0, keepdims=True))
            dens.append(jnp.sum(p, axis=0, keepdims=True))
            ps.append(p.astype(BF16))
        pv = _dot(vt_ref[0, j * LANES:(j + 1) * LANES, 0:w], jnp.concatenate(ps, axis=1))
        ot = jnp.where(lo, pv[:, :LANES] / dens[0], pv[:, LANES:] / dens[1])
        o_ref[0, :, j * LANES:(j + 1) * LANES] = ot.T


def _dsa_prompt_body(topk, relb_ref, qt_ref, idxt_ref, kb_ref, vt_ref, kib_ref, o_ref,
                     keys_scr, madd_scr, bias_scr):
    b = pl.program_id(0)
    i = pl.program_id(1)
    ncs = kb_ref.shape[1] // LANES

    @pl.when((b == 0) & (i == 0))
    def _():
        r = lax.broadcasted_iota(I32, (LANES, LANES), 0)
        c = lax.broadcasted_iota(I32, (LANES, LANES), 1)
        for h in range(H_B):
            far = relb_ref[N_BUCKETS - 1, h]
            bias_scr[h, 0] = _bias_of_dist(c - r, relb_ref, h) - far
            bias_scr[h, 1] = _bias_of_dist(LANES + c - r, relb_ref, h) - far
            bias_scr[h, 2] = jnp.zeros((LANES, LANES), F32)

    nvar = -(-ncs // DSA_CHUNKS_PER_VARIANT)
    for var in range(nvar):
        nc = min((var + 1) * DSA_CHUNKS_PER_VARIANT, ncs)

        @pl.when(i // DSA_CHUNKS_PER_VARIANT == var)
        def _(nc=nc):
            _dsa_prompt_tile(nc, topk, i, qt_ref, idxt_ref, kb_ref, vt_ref, kib_ref, o_ref,
                             keys_scr, madd_scr, bias_scr)


def _dsa_prompt(rel_bias, qt3, idxt3, kb3, vt3, kib3):
    b, s, _ = kb3.shape
    assert s % TQ == 0 and TQ == LANES
    topk = min(TOPK_MAX, s // 4)
    tspec = lambda w: pl.BlockSpec((1, w, TQ), lambda i, j: (i, 0, j))
    kspec = lambda w: pl.BlockSpec((1, s, w), lambda i, j: (i, 0, 0))
    ncs = s // LANES
    return pl.pallas_call(
        functools.partial(_dsa_prompt_body, topk),
        grid=(b, s // TQ),
        in_specs=[pl.BlockSpec(memory_space=pltpu.SMEM),
                  tspec(D_B), tspec(IDX_W), kspec(D_B), pl.BlockSpec((1, D_B, s), lambda i, j: (i, 0, 0)),
                  kspec(LANES)],
        out_specs=pl.BlockSpec((1, TQ, D_B), lambda i, j: (i, j, 0)),
        out_shape=jax.ShapeDtypeStruct((b, s, D_B), F32),
        scratch_shapes=[pltpu.VMEM((ncs, LANES, LANES), I32), pltpu.VMEM((ncs, LANES, LANES), F32),
                        pltpu.VMEM((H_B, 3, LANES, LANES), F32)],
        compiler_params=_cparams(2),
        name="dsa_prompt",
    )(rel_bias, qt3, idxt3, kb3, vt3, kib3)


class _PagedPrefetch:
    def __init__(self, pt_ref, caches, pages_per_step, steps_per_seq):
        self.pt_ref, self.caches = pt_ref, caches
        self.n, self.steps = pages_per_step, steps_per_seq

    def _copies(self, seq, group, slot):
        out = []
        for p in range(self.n):
            page = self.pt_ref[seq, group * self.n + p]
            for hbm, buf, sem in self.caches:
                out.append(pltpu.make_async_copy(hbm.at[0, page], buf.at[slot, p], sem.at[slot]))
        return out

    def advance(self, seq, group, n_seq):
        step = seq * self.steps + group
        slot = step & 1

        @pl.when(step == 0)
        def _():
            for cp in self._copies(seq, group, slot):
                cp.start()

        wrap = group + 1 == self.steps
        nseq = jnp.where(wrap, seq + 1, seq)
        ngroup = jnp.where(wrap, 0, group + 1)

        @pl.when(nseq < n_seq)
        def _():
            for cp in self._copies(nseq, ngroup, 1 - slot):
                cp.start()

        for cp in self._copies(seq, group, slot):
            cp.wait()
        return slot


def _dsa_score_body(n_pages, pt_ref, idx_ref, kidx_hbm, keys_ref, knew_ref, kbuf, ksem, kcat_scr):
    b = pl.program_id(0)
    slot = _PagedPrefetch(pt_ref, [(kidx_hbm, kbuf, ksem)], n_pages, 1).advance(b, 0, pl.num_programs(0))
    t = idx_ref.shape[1]
    idx = idx_ref[0]
    qrows = jnp.concatenate([idx[:, h * D_IDX:(h + 1) * D_IDX] for h in range(H_IDX)], axis=0).astype(BF16)

    def score_of(qk):
        sh = jnp.maximum(qk * D_IDX ** -0.5, 0.0)
        out = None
        for h in range(H_IDX):
            term = idx[:, IDX_WI + h:IDX_WI + h + 1] * sh[h * t:(h + 1) * t]
            out = term if out is None else out + term
        return out

    for g in range(n_pages):
        kcat_scr[:, g * PAGE_SIZE:(g + 1) * PAGE_SIZE] = kbuf[slot, g].astype(BF16)
    score = score_of(_dot(qrows, kcat_scr[...]))
    for g in range(n_pages):
        keys_ref[g] = _sortable_key(score[:, g * PAGE_SIZE:(g + 1) * PAGE_SIZE])

    knew = jnp.concatenate([idx[:, IDX_KI:IDX_KI + D_IDX], jnp.zeros((LANES - t, D_IDX), F32)], axis=0)
    sn = score_of(_dot_nt(qrows, knew.astype(BF16)))
    tq = lax.broadcasted_iota(I32, (t, LANES), 0)
    tk = lax.broadcasted_iota(I32, (t, LANES), 1)
    knew_ref[...] = _sortable_key(jnp.where(tk <= tq, sn, NEG_INF))


def _dsa_pick_body(topk, t, keys_ref, knew_ref, sel_ref, keys_scr):
    nc = keys_ref.shape[0]
    rows = keys_ref.shape[1]
    keys_scr[0:nc] = keys_ref[...]
    keys_scr[nc] = knew_ref[...]
    sel = _select_topk(keys_scr, nc + 1, rows, topk, 1)
    sel_ref[0:nc] = jnp.where(sel[0:nc], 1.0, 0.0)
    tq = lax.broadcasted_iota(I32, (rows, LANES), 0) & (t - 1)
    tk = lax.broadcasted_iota(I32, (rows, LANES), 1)
    sel_ref[nc] = jnp.where(sel[nc] & (tk <= tq), 1.0, 0.0)


def _dsa_select(page_table, idx3, cache_kidx):
    db, t, _ = idx3.shape
    n_pages = page_table.shape[1]
    assert t & (t - 1) == 0
    topk = min(TOPK_MAX, (n_pages * PAGE_SIZE + t) // 4)
    grid_spec = pltpu.PrefetchScalarGridSpec(
        num_scalar_prefetch=1,
        grid=(db,),
        in_specs=[pl.BlockSpec((1, t, IDX_W), lambda b, pt: (b, 0, 0)), pl.BlockSpec(memory_space=pl.ANY)],
        out_specs=[pl.BlockSpec((n_pages, t, LANES), lambda b, pt: (0, b, 0)),
                   pl.BlockSpec((t, LANES), lambda b, pt: (b, 0))],
        scratch_shapes=[pltpu.VMEM((2, n_pages, D_IDX, PAGE_SIZE), F32), pltpu.SemaphoreType.DMA((2,)),
                        pltpu.VMEM((D_IDX, n_pages * PAGE_SIZE), BF16)],
    )
    keys, knew = pl.pallas_call(
        functools.partial(_dsa_score_body, n_pages),
        grid_spec=grid_spec,
        out_shape=[jax.ShapeDtypeStruct((n_pages, db * t, LANES), I32),
                   jax.ShapeDtypeStruct((db * t, LANES), I32)],
        compiler_params=_cparams(1),
        name="dsa_score",
    )(page_table, idx3, cache_kidx)
    rows = min(R_PICK, db * t)
    return pl.pallas_call(
        functools.partial(_dsa_pick_body, topk, t),
        grid=(db * t // rows,),
        in_specs=[pl.BlockSpec((n_pages, rows, LANES), lambda r: (0, r, 0)),
                  pl.BlockSpec((rows, LANES), lambda r: (r, 0))],
        out_specs=pl.BlockSpec((n_pages + 1, rows, LANES), lambda r: (0, r, 0)),
        out_shape=jax.ShapeDtypeStruct((n_pages + 1, db * t, LANES), F32),
        scratch_shapes=[pltpu.VMEM((n_pages + 1, rows, LANES), I32)],
        compiler_params=_cparams(1),
        name="dsa_pick",
    )(keys, knew)


def _dsa_sample_body(g_pages, pt_ref, relb_ref, qb_ref, sel_ref, knew_ref, vnew_ref, k_hbm, v_hbm, o_ref,
                     kbuf, vbuf, ksem, vsem, qbd_scr, kcat_scr, vcat_scr, bias_scr, m_scr, l_scr, acc_scr):
    b = pl.program_id(0)
    c = pl.program_id(1)
    nsteps = pl.num_programs(1)
    t = qb_ref.shape[1]
    slot = _PagedPrefetch(pt_ref, [(k_hbm, kbuf, ksem), (v_hbm, vbuf, vsem)], g_pages, nsteps).advance(
        b, c, pl.num_programs(0))

    @pl.when((b == 0) & (c == 0))
    def _():
        tq = lax.broadcasted_iota(I32, (t, LANES), 0)
        col = lax.broadcasted_iota(I32, (t, LANES), 1)
        for h in range(H_B):
            rs = slice(h * t, (h + 1) * t)
            bias_scr[0, rs, :] = jnp.full((t, LANES), relb_ref[N_BUCKETS - 1, h], F32)
            bias_scr[1, rs, :] = _bias_of_dist(tq + PAGE_SIZE - col, relb_ref, h)
            bias_scr[2, rs, :] = _bias_of_dist(tq - col, relb_ref, h)

    @pl.when(c == 0)
    def _():
        q = qb_ref[0] * DH_B ** -0.5
        lane = lax.broadcasted_iota(I32, (t, D_B), 1)
        qbd = [jnp.where((lane >= h * DH_B) & (lane < (h + 1) * DH_B), q, 0.0) for h in range(H_B)]
        qbd_scr[...] = jnp.concatenate(qbd, axis=0).astype(BF16)
        m_scr[...] = jnp.full(m_scr.shape, NEG_INF, F32)
        l_scr[...] = jnp.zeros(l_scr.shape, F32)
        acc_scr[...] = jnp.zeros(acc_scr.shape, F32)

    def update(logit, sel, vmat, v_key_minor):
        selr = jnp.concatenate([sel] * H_B, axis=0) > 0.5
        lm = jnp.where(selr, logit, NEG_INF)
        m_old = m_scr[...]
        m_new = jnp.maximum(m_old, jnp.max(lm, axis=1, keepdims=True))
        m_safe = jnp.where(m_new == NEG_INF, 0.0, m_new)
        alpha = jnp.exp(m_old - m_safe)
        p = jnp.exp(lm - m_safe)
        l_scr[...] = alpha * l_scr[...] + jnp.sum(p, axis=1, keepdims=True)
        pv = _dot_nt(p.astype(BF16), vmat) if v_key_minor else _dot(p.astype(BF16), vmat)
        acc_scr[...] = alpha * acc_scr[...] + pv
        m_scr[...] = m_new

    for g in range(g_pages):
        kcat_scr[:, g * PAGE_SIZE:(g + 1) * PAGE_SIZE] = kbuf[slot, g].astype(BF16)
        vcat_scr[:, g * PAGE_SIZE:(g + 1) * PAGE_SIZE] = vbuf[slot, g].astype(BF16)
    logit = _dot(qbd_scr[...], kcat_scr[...])
    far = bias_scr[0]
    last = jnp.where(c == nsteps - 1, bias_scr[1], far)
    bias = jnp.concatenate([far] * (g_pages - 1) + [last], axis=1)
    sel = jnp.concatenate([sel_ref[c * g_pages + g] for g in range(g_pages)], axis=1)
    update(logit + bias, sel, vcat_scr[...], True)

    @pl.when(c == nsteps - 1)
    def _():
        pad = jnp.zeros((LANES - t, D_B), F32)
        knp = jnp.concatenate([knew_ref[0], pad], axis=0).astype(BF16)
        vnp = jnp.concatenate([vnew_ref[0], pad], axis=0).astype(BF16)
        ln = _dot_nt(qbd_scr[...], knp) + bias_scr[2]
        update(ln, sel_ref[sel_ref.shape[0] - 1], vnp, False)
        out = acc_scr[...] / l_scr[...]
        lane = lax.broadcasted_iota(I32, (t, D_B), 1)
        o = jnp.zeros((t, D_B), F32)
        for h in range(H_B):
            o = jnp.where((lane >= h * DH_B) & (lane < (h + 1) * DH_B), out[h * t:(h + 1) * t], o)
        o_ref[0] = o


def _dsa_sample(page_table, rel_bias, qb3, sel, k3, v3, cache_k, cache_v):
    db, t, _ = qb3.shape
    n_pages = page_table.shape[1]
    g = min(G_DSA, n_pages)
    assert n_pages % g == 0
    rows = H_B * t
    seq = lambda w: pl.BlockSpec((1, t, w), lambda b, c, pt: (b, 0, 0))
    sel_spec = pl.BlockSpec((sel.shape[0], t, LANES), lambda b, c, pt: (0, b, 0))
    grid_spec = pltpu.PrefetchScalarGridSpec(
        num_scalar_prefetch=1,
        grid=(db, n_pages // g),
        in_specs=[pl.BlockSpec(memory_space=pltpu.SMEM), seq(D_B), sel_spec, seq(D_B), seq(D_B),
                  pl.BlockSpec(memory_space=pl.ANY), pl.BlockSpec(memory_space=pl.ANY)],
        out_specs=seq(D_B),
        scratch_shapes=[pltpu.VMEM((2, g, D_B, PAGE_SIZE), F32), pltpu.VMEM((2, g, D_B, PAGE_SIZE), F32),
                        pltpu.SemaphoreType.DMA((2,)), pltpu.SemaphoreType.DMA((2,)),
                        pltpu.VMEM((rows, D_B), BF16),
                        pltpu.VMEM((D_B, g * PAGE_SIZE), BF16), pltpu.VMEM((D_B, g * PAGE_SIZE), BF16),
                        pltpu.VMEM((3, rows, LANES), F32),
                        pltpu.VMEM((rows, 1), F32), pltpu.VMEM((rows, 1), F32), pltpu.VMEM((rows, D_B), F32)],
    )
    return pl.pallas_call(
        functools.partial(_dsa_sample_body, g),
        grid_spec=grid_spec,
        out_shape=jax.ShapeDtypeStruct((db, t, D_B), F32),
        compiler_params=_cparams(2),
        name="dsa_sample",
    )(page_table, rel_bias, qb3, sel, k3, v3, cache_k, cache_v)


def _ones_beyond_rope(kpe_slab):
    lane = lax.broadcasted_iota(I32, kpe_slab.shape, 1)
    return jnp.where(lane < ROPE_C, kpe_slab, 1.0)


def _rms(x, g):
    return x * lax.rsqrt(jnp.mean(x * x, axis=1, keepdims=True) + RMS_EPS) * g


def _odd_proj_body(x_ref, wa_ref, wb_ref, wbr_ref, wg_ref, wq_ref, wqr_ref, qn_ref, kvn_ref, cos_ref, sin_ref,
                   qcat_ref, kvpe_ref, kvpeb_ref, g_ref):
    xb = x_ref[...].astype(BF16)
    cos_k, sin_k = cos_ref[:, :LANES], sin_ref[:, :LANES]
    cos_q, sin_q = cos_ref[:, LANES:], sin_ref[:, LANES:]
    g_ref[...] = _dot(xb, wg_ref[...])
    kv = _dot(xb, wb_ref[...])
    kvr = _dot(xb, wbr_ref[...])
    ckvn = _rms(kv[:, :KV_LORA], kvn_ref[...])
    kpe = kv[:, KV_LORA:] * cos_k + kvr[:, KV_LORA:] * sin_k
    kvpe = jnp.concatenate([ckvn, kpe], axis=1)
    kvpe_ref[...] = kvpe
    kvpeb_ref[...] = kvpe.astype(BF16)
    cqn = _rms(_dot(xb, wa_ref[...]), qn_ref[...]).astype(BF16)
    qc = _dot(cqn, wq_ref[...])
    qr = _dot(cqn, wqr_ref[...])
    for h in range(H_C):
        sl = slice(h * LANES, (h + 1) * LANES)
        qcat_ref[:, sl] = (qc[:, sl] * cos_q + qr[:, sl] * sin_q).astype(qcat_ref.dtype)


def _odd_proj(x2d, wa, wb, wbr, wg, wq, wqr, qn, kvn, cos_t, sin_t, qcat_dtype):
    m = x2d.shape[0]
    tm = min(TM_PROJ, m)
    nt = cos_t.shape[0] // tm
    row = lambda w: pl.BlockSpec((tm, w), lambda i: (i, 0))
    full = lambda a: pl.BlockSpec(a.shape, lambda i: (0, 0))
    tab = pl.BlockSpec((tm, 2 * LANES), lambda i: (i % nt, 0))
    f = lambda w, dt: jax.ShapeDtypeStruct((m, w), dt)
    return pl.pallas_call(
        _odd_proj_body,
        grid=(m // tm,),
        in_specs=[row(D_MODEL)] + [full(a) for a in (wa, wb, wbr, wg, wq, wqr, qn, kvn)] + [tab, tab],
        out_specs=[row(QCAT_W), row(KVPE_W), row(KVPE_W), row(D_C)],
        out_shape=[f(QCAT_W, qcat_dtype), f(KVPE_W, F32), f(KVPE_W, BF16), f(D_C, F32)],
        compiler_params=_cparams(1),
        name="odd_proj",
    )(x2d, wa, wb, wbr, wg, wq, wqr, qn, kvn, cos_t, sin_t)


def _mla_expand_q(qcat, wexp_ref):
    rows = [_dot(qcat[:, h * LANES:(h + 1) * LANES].astype(BF16), wexp_ref[h]) for h in range(H_C)]
    return jnp.concatenate(rows, axis=0).astype(BF16)


def _mla_softmax_step(s, m_old, vmat):
    m_new = jnp.maximum(m_old, jnp.max(s, axis=1, keepdims=True))
    alpha = jnp.exp2((m_old - m_new) * MLA_EXP2_SCALE)
    m_wide = jnp.concatenate([m_new] * (s.shape[1] // LANES), axis=1)
    p = jnp.exp2((s - m_wide) * MLA_EXP2_SCALE)
    pv = _dot(p.astype(BF16), vmat)
    return m_new, jnp.concatenate([alpha] * (KVPE_W // LANES), axis=1), pv


def _mla_head_pair_out(acc_a, acc_b, wuv_pair):
    lat = [a[:, :KV_LORA] / a[:, MLA_DEN_LANE:MLA_DEN_LANE + 1] for a in (acc_a, acc_b)]
    return _dot(jnp.concatenate(lat, axis=1).astype(BF16), wuv_pair)


def _mla_prompt_body(qcat_ref, kv_ref, wk_ref, wv_ref, o_ref, kx_scr, vx_scr, m_scr, acc_scr, mask_scr):
    i = pl.program_id(1)
    tq = qcat_ref.shape[1]
    s = kv_ref.shape[1]

    @pl.when(i == 0)
    def _():
        lane = lax.broadcasted_iota(I32, (tq, QCAT_W), 1)
        ones = (((lane >> 7) ^ (lane >> 6)) & 1) == 1
        for r in range(s // tq):
            rs = slice(r * tq, (r + 1) * tq)
            kv = kv_ref[0, rs, :]
            kx_scr[rs, :] = _dot(kv, wk_ref[...]).astype(BF16)
            vx_scr[rs, :] = jnp.where(ones, 1.0, _dot(kv[:, :KV_LORA], wv_ref[...])).astype(BF16)

    @pl.when((pl.program_id(0) == 0) & (i == 0))
    def _():
        row = lax.broadcasted_iota(I32, (tq, tq), 0)
        col = lax.broadcasted_iota(I32, (tq, tq), 1)
        mask_scr[0] = jnp.zeros((tq, tq), F32)
        mask_scr[1] = jnp.where(col <= row, 0.0, NEG_INF)

    m_scr[...] = jnp.full(m_scr.shape, NEG_INF, F32)
    acc_scr[...] = jnp.zeros(acc_scr.shape, F32)

    def chunk(c, carry):
        k0 = pl.multiple_of(c * tq, tq)
        mask = mask_scr[(c == i).astype(I32)]
        for h in range(H_C):
            hs = slice(h * LANES, (h + 1) * LANES)
            rs = slice(h * tq, (h + 1) * tq)
            sc = _dot_nt(qcat_ref[0, :, hs], kx_scr[pl.ds(k0, tq), hs]) + mask
            m_old = m_scr[rs, :]
            m_new = jnp.maximum(m_old, jnp.max(sc, axis=1, keepdims=True))
            alpha = jnp.exp2((m_old - m_new) * MLA_EXP2_SCALE)
            p = jnp.exp2((sc - jnp.concatenate([m_new] * (tq // LANES), axis=1)) * MLA_EXP2_SCALE)
            acc_scr[rs, :] = alpha * acc_scr[rs, :] + _dot(p.astype(BF16), vx_scr[pl.ds(k0, tq), hs])
            m_scr[rs, :] = m_new
        return carry

    lax.fori_loop(0, i + 1, chunk, 0)
    lo = lax.broadcasted_iota(I32, (tq, LANES), 1) < DV_C
    for j in range(H_C // 2):
        ae = acc_scr[2 * j * tq:(2 * j + 1) * tq, :]
        ao = acc_scr[(2 * j + 1) * tq:(2 * j + 2) * tq, :]
        o_ref[0, :, j * LANES:(j + 1) * LANES] = jnp.where(
            lo, ae / pltpu.roll(ae, DV_C, 1), ao / pltpu.roll(ao, DV_C, 1))


def _mla_prompt(qcat3, kvpeb3, wk2, wv2):
    b, s, _ = qcat3.shape
    tq = min(TQ_MLA, s)
    assert s % tq == 0 and tq % LANES == 0
    rows = H_C * tq
    return pl.pallas_call(
        _mla_prompt_body,
        grid=(b, s // tq),
        in_specs=[pl.BlockSpec((1, tq, QCAT_W), lambda i, j: (i, j, 0)),
                  pl.BlockSpec((1, s, KVPE_W), lambda i, j: (i, 0, 0)),
                  pl.BlockSpec(wk2.shape, lambda i, j: (0, 0)),
                  pl.BlockSpec(wv2.shape, lambda i, j: (0, 0))],
        out_specs=pl.BlockSpec((1, tq, D_C), lambda i, j: (i, j, 0)),
        out_shape=jax.ShapeDtypeStruct((b, s, D_C), F32),
        scratch_shapes=[pltpu.VMEM((s, QCAT_W), BF16), pltpu.VMEM((s, QCAT_W), BF16),
                        pltpu.VMEM((rows, LANES), F32), pltpu.VMEM((rows, LANES), F32),
                        pltpu.VMEM((2, tq, tq), F32)],
        compiler_params=_cparams(2),
        name="mla_prompt",
    )(qcat3, kvpeb3, wk2, wv2)


def _mla_sample_body(g_pages, pt_ref, qcat_ref, knew_ref, wexp_ref, wuv_ref, ckv_hbm, kpe_hbm, o_ref,
                     cbuf, pbuf, csem, psem, qall_scr, ccat_scr, pcat_scr, m_scr, acc_scr):
    b = pl.program_id(0)
    c = pl.program_id(1)
    nsteps = pl.num_programs(1)
    t = qcat_ref.shape[1]
    slot = _PagedPrefetch(pt_ref, [(ckv_hbm, cbuf, csem), (kpe_hbm, pbuf, psem)], g_pages, nsteps).advance(
        b, c, pl.num_programs(0))

    @pl.when((b == 0) & (c == 0))
    def _():
        pcat_scr[...] = jnp.zeros(pcat_scr.shape, BF16)
        ccat_scr[:, KV_LORA:] = jnp.ones((ccat_scr.shape[0], KVPE_W - KV_LORA), BF16)

    @pl.when(c == 0)
    def _():
        qall_scr[...] = _mla_expand_q(qcat_ref[0], wexp_ref)
        m_scr[...] = jnp.full(m_scr.shape, NEG_INF, F32)
        acc_scr[...] = jnp.zeros(acc_scr.shape, F32)

    def update(s, vmat):
        m_new, alpha, pv = _mla_softmax_step(s, m_scr[...], vmat)
        acc_scr[...] = alpha * acc_scr[...] + pv
        m_scr[...] = m_new

    q = qall_scr[...]
    sub = min(MLA_SUB_PAGES, g_pages)
    m_run = m_scr[...]
    acc_run = acc_scr[...]
    nblk = g_pages // sub

    def scores(blk):
        rs = slice(blk * sub * PAGE_SIZE, (blk + 1) * sub * PAGE_SIZE)
        for g in range(blk * sub, (blk + 1) * sub):
            ccat_scr[g * PAGE_SIZE:(g + 1) * PAGE_SIZE, 0:KV_LORA] = cbuf[slot, g].astype(BF16)
            pcat_scr[0:ROPE_C, g * PAGE_SIZE:(g + 1) * PAGE_SIZE] = pbuf[slot, g].astype(BF16)
        return _dot_nt(q[:, :KV_LORA], ccat_scr[rs, 0:KV_LORA]) + _dot(q[:, KV_LORA:], pcat_scr[:, rs])

    s_next = scores(0)
    for blk in range(nblk):
        s = s_next
        if blk + 1 < nblk:
            s_next = scores(blk + 1)
        rs = slice(blk * sub * PAGE_SIZE, (blk + 1) * sub * PAGE_SIZE)
        m_run, alpha, pv = _mla_softmax_step(s, m_run, ccat_scr[rs, :])
        acc_run = alpha * acc_run + pv
    m_scr[...] = m_run
    acc_scr[...] = acc_run

    @pl.when(c == nsteps - 1)
    def _():
        kn = knew_ref[0]
        kn = jnp.concatenate([kn[:, :KV_LORA], _ones_beyond_rope(kn[:, KV_LORA:])], axis=1)
        knp = jnp.concatenate([kn, jnp.zeros((LANES - t, KVPE_W), F32)], axis=0).astype(BF16)
        sn = _dot_nt(q, knp)
        r = lax.broadcasted_iota(I32, (H_C * t, LANES), 0)
        col = lax.broadcasted_iota(I32, (H_C * t, LANES), 1)
        sn = jnp.where(col <= (r & (t - 1)), sn, NEG_INF)
        update(sn, knp)
        acc = acc_scr[...]
        outs = [_mla_head_pair_out(acc[2 * j * t:(2 * j + 1) * t], acc[(2 * j + 1) * t:(2 * j + 2) * t], wuv_ref[j])
                for j in range(H_C // 2)]
        o_ref[0] = jnp.concatenate(outs, axis=1)


def _mla_sample(page_table, qcat3, kvpeb3, wexp, wuvp, cache_ckv, cache_kpe):
    db, t, _ = qcat3.shape
    n_pages = page_table.shape[1]
    g = min(G_MLA, n_pages)
    assert n_pages % g == 0
    rows = H_C * t
    grid_spec = pltpu.PrefetchScalarGridSpec(
        num_scalar_prefetch=1,
        grid=(db, n_pages // g),
        in_specs=[pl.BlockSpec((1, t, QCAT_W), lambda b, c, pt: (b, 0, 0)),
                  pl.BlockSpec((1, t, KVPE_W), lambda b, c, pt: (b, 0, 0)),
                  pl.BlockSpec(wexp.shape, lambda b, c, pt: (0, 0, 0)),
                  pl.BlockSpec(wuvp.shape, lambda b, c, pt: (0, 0, 0)),
                  pl.BlockSpec(memory_space=pl.ANY), pl.BlockSpec(memory_space=pl.ANY)],
        out_specs=pl.BlockSpec((1, t, D_C), lambda b, c, pt: (b, 0, 0)),
        scratch_shapes=[pltpu.VMEM((2, g, PAGE_SIZE, KV_LORA), F32), pltpu.VMEM((2, g, ROPE_C, PAGE_SIZE), F32),
                        pltpu.SemaphoreType.DMA((2,)), pltpu.SemaphoreType.DMA((2,)),
                        pltpu.VMEM((rows, KVPE_W), BF16),
                        pltpu.VMEM((g * PAGE_SIZE, KVPE_W), BF16), pltpu.VMEM((LANES, g * PAGE_SIZE), BF16),
                        pltpu.VMEM((rows, LANES), F32), pltpu.VMEM((rows, KVPE_W), F32)],
    )
    return pl.pallas_call(
        functools.partial(_mla_sample_body, g),
        grid_spec=grid_spec,
        out_shape=jax.ShapeDtypeStruct((db, t, D_C), F32),
        compiler_params=_cparams(2),
        name="mla_sample",
    )(page_table, qcat3, kvpeb3, wexp, wuvp, cache_ckv, cache_kpe)


def _rope_cos_sin(pos):
    inv = ROPE_THETA ** (-jnp.arange(ROPE_HALF, dtype=F32) / ROPE_HALF)
    ang = pos.astype(F32)[:, None] * inv
    return jnp.cos(ang), jnp.sin(ang)


def _rope_lane_pattern(width, period, rot_start, limit):
    lane = np.arange(width)
    d = lane % period - rot_start
    inside = lane < limit
    x1 = inside & (d >= 0) & (d < ROPE_HALF)
    x2 = inside & (d >= ROPE_HALF) & (d < 2 * ROPE_HALF)
    return x1, x2


def _rope_tables(cos, sin, x1, x2, base):
    width = x1.shape[0]
    reps = width // ROPE_HALF
    cos_w = jnp.tile(cos, (1, reps))
    sin_w = jnp.tile(sin, (1, reps))
    cos_t = jnp.where(x1 | x2, cos_w, jnp.asarray(base, F32)[None, :])
    sin_t = jnp.where(x1, -sin_w, jnp.where(x2, sin_w, 0.0))
    return cos_t, sin_t


def _partner_columns(w, x1, x2):
    lane = np.arange(w.shape[1])
    src = lane + ROPE_HALF * x1 - ROPE_HALF * x2
    return jnp.where((x1 | x2)[None, :], w[:, src], 0.0)


def _tile_rows(tab, reps):
    return jnp.tile(tab, (reps, 1))


def kernel(x_prompt, x_sample, state_pool, cache_k_b, cache_v_b, cache_kidx_b, cache_ckv, cache_kpe, page_table, w_in_even, pool_w, pool_scale, w_out_even, rel_bias, w_in_odd, q_norm, w_q_b, kv_norm, w_uk, w_uv, w_out_odd, ln_g, ln_b):
    bsz, seq, _ = x_prompt.shape
    db, t, _ = x_sample.shape
    n_pages = page_table.shape[1]
    past = n_pages * PAGE_SIZE
    mp, ms = bsz * seq, db * t
    assert w_in_even.shape[0] == 1 and w_in_odd.shape[0] == 1 and t == 8

    cos_p, sin_p = _rope_cos_sin(jnp.arange(seq))
    cos_s, sin_s = _rope_cos_sin(past + jnp.arange(t))

    we = w_in_even[0]
    n_main = 6 * D_A
    wm = we[:, :n_main].astype(BF16)
    w_qi = we[:, n_main:n_main + H_IDX * D_IDX]
    w_ki = we[:, n_main + H_IDX * D_IDX:n_main + H_IDX * D_IDX + D_IDX]
    w_wi = we[:, n_main + H_IDX * D_IDX + D_IDX:]
    ws32 = jnp.concatenate([w_qi, w_ki, w_ki, w_wi, jnp.zeros((D_MODEL, IDX_W - IDX_WI - H_IDX), F32)], axis=1)
    x1e, x2e = _rope_lane_pattern(IDX_W, D_IDX, 0, IDX_WI)
    base_e = np.where(np.arange(IDX_W) < IDX_WI, 1.0, np.where(np.arange(IDX_W) < IDX_WI + H_IDX, H_IDX ** -0.5, 0.0))
    ws = ws32.astype(BF16)
    wr = _partner_columns(ws32, x1e, x2e).astype(BF16)
    pw = pool_w[0].astype(BF16)
    pscale = pool_scale[0][None, :]
    woe = w_out_even[0].astype(BF16)
    lng0, lnb0 = ln_g[0][None, :], ln_b[0][None, :]
    lng1, lnb1 = ln_g[1][None, :], ln_b[1][None, :]

    def even_tables(cos, sin, reps):
        ct, st = _rope_tables(cos, sin, x1e, x2e, base_e)
        return _tile_rows(ct, reps), _tile_rows(st, reps)

    xp2 = x_prompt.reshape(mp, D_MODEL)
    ct, st = even_tables(cos_p, sin_p, 1)
    u, ga, gb, k_fm, v_fm, idx_fm, qt, vt, kb, kib = _even_proj(xp2, wm, ws, wr, ct, st, seq_len=seq)
    r3 = lambda a: a.reshape(bsz, seq, a.shape[-1])
    a_p = _pool_prompt(r3(u), pw, pscale).reshape(mp, D_A)
    o_p = _dsa_prompt(rel_bias, qt, idx_fm, r3(kb), vt, r3(kib)).reshape(mp, D_B)
    xp1 = _gate_out_ln(xp2, [(a_p, ga), (o_p, gb)], woe, lng0, lnb0, "even_out_prompt")
    pool_p = r3(u)[:, seq - POOL_STATE:][None]
    kb_p = jnp.transpose(k_fm.reshape(bsz, H_B, DH_B, seq), (0, 3, 1, 2))[None]
    vb_p = jnp.transpose(v_fm.reshape(bsz, H_B, DH_B, seq), (0, 3, 1, 2))[None]
    ki_p = jnp.transpose(idx_fm[:, IDX_KI:IDX_KI + D_IDX, :], (0, 2, 1))[None]

    xs2 = x_sample.reshape(ms, D_MODEL)
    tm_s = min(TM_PROJ, ms)
    ct, st = even_tables(cos_s, sin_s, tm_s // t)
    u, ga, gb, q, k, v, idx = _even_proj(xs2, wm, ws, wr, ct, st)
    s3 = lambda a: a.reshape(db, t, a.shape[-1])
    u_s = s3(u)
    ext = jnp.concatenate([jnp.zeros((db, 16 - POOL_STATE, D_A), F32), state_pool[0], u_s], axis=1)
    a_s = _pool_sample(ext, pw, pscale)
    n_phys = cache_k_b.shape[1]
    kidx_t = jnp.transpose(cache_kidx_b, (0, 1, 3, 2))
    ck4 = jnp.transpose(cache_k_b, (0, 1, 3, 4, 2)).reshape(1, n_phys, D_B, PAGE_SIZE)
    cv4 = jnp.transpose(cache_v_b, (0, 1, 3, 4, 2)).reshape(1, n_phys, D_B, PAGE_SIZE)
    kpe_t = jnp.transpose(cache_kpe, (0, 1, 3, 2))
    sel = _dsa_select(page_table, s3(idx), kidx_t)
    o_s = _dsa_sample(page_table, rel_bias, s3(q), sel, s3(k), s3(v), ck4, cv4).reshape(ms, D_B)
    xs1 = _gate_out_ln(xs2, [(a_s, ga), (o_s, gb)], woe, lng0, lnb0, "even_out_sample")
    pool_s = ext[:, ext.shape[1] - POOL_STATE:][None]
    kb_s = k.reshape(1, db, t, H_B, DH_B)
    vb_s = v.reshape(1, db, t, H_B, DH_B)
    ki_s = s3(idx)[:, :, IDX_KI:IDX_KI + D_IDX][None]

    wo = w_in_odd[0]
    wa = wo[:, :Q_LORA].astype(BF16)
    wb32 = jnp.concatenate([wo[:, Q_LORA:Q_LORA + KV_LORA + ROPE_C],
                            jnp.zeros((D_MODEL, KVPE_W - KV_LORA - ROPE_C), F32)], axis=1)
    x1k, x2k = _rope_lane_pattern(KVPE_W, KVPE_W, KV_LORA, KVPE_W)
    wb = wb32.astype(BF16)
    wbr = _partner_columns(wb32, x1k, x2k).astype(BF16)
    wg = wo[:, Q_LORA + KV_LORA + ROPE_C:].astype(BF16)
    wqb = w_q_b[0]
    wq32 = jnp.concatenate([wqb, jnp.zeros((Q_LORA, H_C, LANES - NOPE_C - ROPE_C), F32)], axis=2)
    wq32 = wq32.reshape(Q_LORA, QCAT_W)
    x1q, x2q = _rope_lane_pattern(QCAT_W, LANES, NOPE_C, QCAT_W)
    wq = wq32.astype(BF16)
    wqr = _partner_columns(wq32, x1q, x2q).astype(BF16)
    qn = q_norm[0][None, :]
    kvn = kv_norm[0][None, :]
    base_k = np.zeros(LANES)
    base_q = np.where(np.arange(LANES) < NOPE_C, 1.0, 0.0)

    def odd_tables(cos, sin, reps):
        ck, sk = _rope_tables(cos, sin, x1k[KV_LORA:], x2k[KV_LORA:], base_k)
        cq, sq = _rope_tables(cos, sin, x1q[:LANES], x2q[:LANES], base_q)
        return (_tile_rows(jnp.concatenate([ck, cq], axis=1), reps),
                _tile_rows(jnp.concatenate([sk, sq], axis=1), reps))

    wuk = w_uk[0]
    wexp = jnp.zeros((H_C, LANES, KVPE_W), F32)
    wexp = wexp.at[:, :NOPE_C, :KV_LORA].set(jnp.transpose(wuk, (1, 2, 0)))
    wexp = wexp.at[:, NOPE_C:NOPE_C + ROPE_C, KV_LORA:KV_LORA + ROPE_C].set(jnp.eye(ROPE_C, dtype=F32)[None])
    wexp = wexp.astype(BF16)
    wuv = jnp.transpose(w_uv[0], (1, 0, 2))
    wuvp = jnp.zeros((H_C // 2, 2 * KV_LORA, 2 * DV_C), F32)
    wuvp = wuvp.at[:, :KV_LORA, :DV_C].set(wuv[0::2])
    wuvp = wuvp.at[:, KV_LORA:, DV_C:].set(wuv[1::2])
    wuvp = wuvp.astype(BF16)
    wk2 = jnp.zeros((KVPE_W, H_C, LANES), F32)
    wk2 = wk2.at[:KV_LORA, :, :NOPE_C].set(wuk)
    wk2 = wk2.at[KV_LORA:KV_LORA + ROPE_C, :, NOPE_C:NOPE_C + ROPE_C].set(
        jnp.broadcast_to(jnp.eye(ROPE_C, dtype=F32)[:, None, :], (ROPE_C, H_C, ROPE_C)))
    wk2 = wk2.reshape(KVPE_W, QCAT_W).astype(BF16)
    wv2 = jnp.zeros((KV_LORA, H_C // 2, 2, LANES), F32)
    wv2 = wv2.at[:, :, 0, :DV_C].set(w_uv[0][:, 0::2])
    wv2 = wv2.at[:, :, 1, DV_C:].set(w_uv[0][:, 1::2])
    wv2 = wv2.reshape(KV_LORA, QCAT_W).astype(BF16)
    woo = w_out_odd[0].astype(BF16)

    ct, st = odd_tables(cos_p, sin_p, 1)
    qcat, kvpe, kvpeb, g1 = _odd_proj(xp1, wa, wb, wbr, wg, wq, wqr, qn, kvn, ct, st, BF16)
    o1 = _mla_prompt(r3(qcat), r3(kvpeb), wk2, wv2).reshape(mp, D_C)
    y_p = _gate_out_ln(xp1, [(o1, g1)], woo, lng1, lnb1, "odd_out_prompt").reshape(bsz, seq, D_MODEL)
    ckv_p = r3(kvpe)[:, :, :KV_LORA][None]
    kpe_p = r3(kvpe)[:, :, KV_LORA:KV_LORA + ROPE_C][None]

    ct, st = odd_tables(cos_s, sin_s, tm_s // t)
    qcat, kvpe, kvpeb, g1 = _odd_proj(xs1, wa, wb, wbr, wg, wq, wqr, qn, kvn, ct, st, F32)
    o1 = _mla_sample(page_table, s3(qcat), s3(kvpe), wexp, wuvp, cache_ckv, kpe_t).reshape(ms, D_C)
    y_s = _gate_out_ln(xs1, [(o1, g1)], woo, lng1, lnb1, "odd_out_sample").reshape(db, t, D_MODEL)
    ckv_s = s3(kvpe)[:, :, :KV_LORA][None]
    kpe_s = s3(kvpe)[:, :, KV_LORA:KV_LORA + ROPE_C][None]

    return (y_p, y_s, pool_p, pool_s, kb_p, kb_s, vb_p, vb_s, ki_p, ki_s, ckv_p, ckv_s, kpe_p, kpe_s)
```

```python
import functools
import math

import numpy as np
import jax
import jax.numpy as jnp
from jax import lax
from jax.experimental import pallas as pl
from jax.experimental.pallas import tpu as pltpu

F32 = jnp.float32
BF16 = jnp.bfloat16
I32 = jnp.int32

D_MODEL = 1024
DEPTH = 2
PAGE_SIZE = 128
D_A = D_MODEL // 2
POOL_WINDOWS = (2, 4, 8, 16)
G_A = D_A // len(POOL_WINDOWS)
POOL_STATE = max(POOL_WINDOWS) - 1
H_B = 8
DH_B = 64
D_B = H_B * DH_B
H_IDX = 4
D_IDX = 64
TOPK_MAX = 256
N_BUCKETS = 32
MAX_DISTANCE = 128
H_C = 16
Q_LORA = 384
KV_LORA = 256
NOPE_C = 64
ROPE_C = 32
DV_C = 64
D_C = H_C * DV_C
MLA_SCALE = (NOPE_C + ROPE_C) ** -0.5
MLA_EXP2_SCALE = MLA_SCALE * math.log2(math.e)
ROPE_THETA = 10000.0
ROPE_HALF = 16
LN_EPS = 1e-5
RMS_EPS = 1e-6
ALPHA = (2 * DEPTH) ** 0.25

LANES = 128
IDX_W = 512
IDX_KI = 256
IDX_WI = 384
KVPE_W = 384
MLA_DEN_LANE = 320
QCAT_W = H_C * LANES

VMEM_LIMIT = 48 * 1024 * 1024

TM_PROJ = 256
TS_POOL = 512
G_POOL = 32
TM_OUT = 256
TQ = 128
TQ_MLA = 256
DSA_CHUNKS_PER_VARIANT = 2
R_PICK = 64
G_DSA = 16
G_MLA = 64
MLA_SUB_PAGES = 8

NEG_INF = float("-inf")
INT_MIN = -(2 ** 31)


def _cparams(n_axes):
    return pltpu.CompilerParams(dimension_semantics=("arbitrary",) * n_axes,
                                vmem_limit_bytes=VMEM_LIMIT)


def _dot(a, b):
    return jnp.dot(a, b, preferred_element_type=F32)


def _dot_nt(a, b):
    return lax.dot_general(a, b, (((1,), (1,)), ((), ())), preferred_element_type=F32)


def _t5_breaks():
    max_d = 2 * MAX_DISTANCE
    d = np.arange(max_d + 1)
    me = N_BUCKETS // 2
    large = me + (np.log(np.maximum(d, 1) / me) / math.log(MAX_DISTANCE / me) * (N_BUCKETS - me)).astype(np.int64)
    large = np.minimum(large, N_BUCKETS - 1)
    bucket = np.where(d < me, d, large)
    assert np.all(bucket[MAX_DISTANCE:] == N_BUCKETS - 1)
    return int(bucket[0]), [(int(i), int(bucket[i])) for i in range(1, max_d + 1) if bucket[i] != bucket[i - 1]]


_T5_FIRST, _T5_BREAKS = _t5_breaks()


def _bias_of_dist(dist, relb_ref, h):
    val = jnp.full(dist.shape, relb_ref[_T5_FIRST, h], F32)
    for p, bk in _T5_BREAKS:
        val = jnp.where(dist >= p, relb_ref[bk, h], val)
    return val


def _sortable_key(score):
    score = jnp.where(score == 0.0, 0.0, score)
    bits = lax.bitcast_convert_type(score, I32)
    return jnp.where(bits < 0, bits ^ jnp.int32(0x7FFFFFFF), bits)


def _kth_largest_key(keys_ref, nc, rows, k, splits):
    rs = rows // splits

    def count_ge(part, cand):
        blk = keys_ref[0:nc, part * rs:(part + 1) * rs, :]
        hit = jnp.where(blk >= cand[None], 1.0, 0.0)
        return jnp.sum(jnp.sum(hit, axis=0), axis=1, keepdims=True)

    def body(it, ts):
        bit = jnp.left_shift(jnp.int32(1), jnp.int32(31) - it)
        out = []
        for part, t in enumerate(ts):
            cand = t + bit
            out.append(jnp.where(count_ge(part, cand) >= k, cand, t))
        return tuple(out)

    init = tuple(jnp.full((rs, 1), INT_MIN, I32) for _ in range(splits))
    ts = lax.fori_loop(0, 32, body, init)
    return ts[0] if splits == 1 else jnp.concatenate(ts, axis=0)


def _select_topk(keys_ref, nc, rows, topk, splits):
    t = _kth_largest_key(keys_ref, nc, rows, topk, splits)[None]
    keys = keys_ref[0:nc]
    gt = keys > t
    eqm = keys == t
    cnt_gt = jnp.sum(jnp.sum(jnp.where(gt, 1.0, 0.0), axis=0), axis=1, keepdims=True)
    need = (topk - cnt_gt)[None]
    r = lax.broadcasted_iota(I32, (LANES, 2 * LANES), 0)
    c = lax.broadcasted_iota(I32, (LANES, 2 * LANES), 1)
    su_ones = jnp.where((r < c) | (c >= LANES), 1.0, 0.0).astype(BF16)
    eq = jnp.where(eqm, 1.0, 0.0).reshape(nc * rows, LANES).astype(BF16)
    pt = _dot(eq, su_ones)
    pre = pt[:, :LANES].reshape(nc, rows, LANES)
    tot = pt[:, LANES:].reshape(nc, rows, LANES)
    offs = []
    off = jnp.zeros((rows, LANES), F32)
    for cc in range(nc):
        offs.append(off)
        off = off + tot[cc]
    before = pre + jnp.stack(offs, axis=0)
    return gt | (eqm & (before < need))


def _silu_gate(v, g):
    return v * (g * (1.0 / (1.0 + jnp.exp(-g))))


def _even_proj_body(prompt, x_ref, wm_ref, ws_ref, wr_ref, cos_ref, sin_ref, u_ref, ga_ref, gb_ref, *outs):
    xb = x_ref[...].astype(BF16)

    def mm(n):
        return _dot(xb, wm_ref[:, n * D_A:(n + 1) * D_A])

    u_ref[...] = mm(0)
    ga_ref[...] = mm(1)
    gb_ref[...] = mm(5)
    q, k, v = mm(2), mm(3), mm(4)
    idx = _dot(xb, ws_ref[...]) * cos_ref[...] + _dot(xb, wr_ref[...]) * sin_ref[...]
    if prompt:
        kt_ref, vt_ref, idxt_ref, qtb_ref, vtb_ref, kb_ref, kib_ref = outs
        vt = v.T
        kt_ref[0] = k.T
        vt_ref[0] = vt
        vtb_ref[0] = vt.astype(BF16)
        idxt_ref[0] = idx.T
        qtb_ref[0] = (q * DH_B ** -0.5).T.astype(BF16)
        kb_ref[...] = k.astype(BF16)
        kib_ref[...] = idx[:, IDX_KI:IDX_KI + LANES].astype(BF16)
    else:
        q_ref, k_ref, v_ref, idx_ref = outs
        q_ref[...] = q
        k_ref[...] = k
        v_ref[...] = v
        idx_ref[...] = idx


def _even_proj(x2d, wm, ws, wr, cos_t, sin_t, seq_len=None):
    m = x2d.shape[0]
    tm = min(TM_PROJ, m)
    nt = cos_t.shape[0] // tm
    row = lambda w: pl.BlockSpec((tm, w), lambda i: (i, 0))
    full = lambda a: pl.BlockSpec(a.shape, lambda i: (0, 0))
    tab = pl.BlockSpec((tm, IDX_W), lambda i: (i % nt, 0))
    f = lambda w, dt: jax.ShapeDtypeStruct((m, w), dt)
    if seq_len is None:
        extra_specs = [row(D_B)] * 3 + [row(IDX_W)]
        extra_shapes = [f(D_B, F32)] * 3 + [f(IDX_W, F32)]
    else:
        spt = seq_len // tm
        fm = lambda w: pl.BlockSpec((1, w, tm), lambda i: (i // spt, 0, i % spt))
        fms = lambda w, dt: jax.ShapeDtypeStruct((m // seq_len, w, seq_len), dt)
        extra_specs = [fm(D_B), fm(D_B), fm(IDX_W), fm(D_B), fm(D_B), row(D_B), row(LANES)]
        extra_shapes = [fms(D_B, F32), fms(D_B, F32), fms(IDX_W, F32), fms(D_B, BF16), fms(D_B, BF16),
                        f(D_B, BF16), f(LANES, BF16)]
    return pl.pallas_call(
        functools.partial(_even_proj_body, seq_len is not None),
        grid=(m // tm,),
        in_specs=[row(D_MODEL), full(wm), full(ws), full(wr), tab, tab],
        out_specs=[row(D_A)] * 3 + extra_specs,
        out_shape=[f(D_A, F32)] * 3 + extra_shapes,
        compiler_params=_cparams(1),
        name="even_proj",
    )(x2d, wm, ws, wr, cos_t, sin_t)


def _pool_mix_group(win_sum, cur, inv_cnt, pw_ref, scale_ref, g):
    pooled = win_sum * inv_cnt - cur
    mixed = _dot(pooled.astype(BF16), pw_ref[g])
    return mixed * scale_ref[:, g * G_A:(g + 1) * G_A]


def _pool_prompt_body(u_ref, halo_ref, pw_ref, scale_ref, a_ref, ext_ref):
    s = pl.program_id(1)
    ts = u_ref.shape[1]
    halo = halo_ref[0]
    ext_ref[0:16, :] = jnp.where(s == 0, 0.0, halo)
    ext_ref[16:, :] = u_ref[0]
    pos = s * ts + lax.broadcasted_iota(I32, (ts, 1), 0)
    for g, w in enumerate(POOL_WINDOWS):
        sl = pl.ds(g * G_A, G_A)
        acc = ext_ref[pl.ds(16, ts), sl]
        for kk in range(1, w):
            acc = acc + ext_ref[pl.ds(16 - kk, ts), sl]
        cnt = jnp.minimum(pos + 1, w).astype(F32)
        a_ref[0, :, g * G_A:(g + 1) * G_A] = _pool_mix_group(
            acc, ext_ref[pl.ds(16, ts), sl], 1.0 / cnt, pw_ref, scale_ref, g)


def _pool_prompt(u3, pw, scale):
    b, s, _ = u3.shape
    ts = min(TS_POOL, s)
    hb = ts // 16
    return pl.pallas_call(
        _pool_prompt_body,
        grid=(b, s // ts),
        in_specs=[pl.BlockSpec((1, ts, D_A), lambda i, j: (i, j, 0)),
                  pl.BlockSpec((1, 16, D_A), lambda i, j: (i, jnp.maximum(j * hb - 1, 0), 0)),
                  pl.BlockSpec(pw.shape, lambda i, j: (0, 0, 0)),
                  pl.BlockSpec(scale.shape, lambda i, j: (0, 0))],
        out_specs=pl.BlockSpec((1, ts, D_A), lambda i, j: (i, j, 0)),
        out_shape=jax.ShapeDtypeStruct((b, s, D_A), F32),
        scratch_shapes=[pltpu.VMEM((ts + 16, D_A), F32)],
        compiler_params=_cparams(2),
        name="pool_prompt",
    )(u3, u3, pw, scale)


def _pool_sample_body(ext_ref, pw_ref, scale_ref, a_ref):
    gs = ext_ref.shape[0]
    t = ext_ref.shape[1] - 16
    for g, w in enumerate(POOL_WINDOWS):
        sl = pl.ds(g * G_A, G_A)
        cur = ext_ref[:, pl.ds(16, t), sl]
        acc = cur
        for kk in range(1, w):
            acc = acc + ext_ref[:, pl.ds(16 - kk, t), sl]
        acc = acc.reshape(gs * t, G_A)
        cur = cur.reshape(gs * t, G_A)
        a_ref[:, g * G_A:(g + 1) * G_A] = _pool_mix_group(acc, cur, 1.0 / w, pw_ref, scale_ref, g)


def _pool_sample(ext, pw, scale):
    db, e, _ = ext.shape
    t = e - 16
    gs = min(G_POOL, db)
    return pl.pallas_call(
        _pool_sample_body,
        grid=(db // gs,),
        in_specs=[pl.BlockSpec((gs, e, D_A), lambda i: (i, 0, 0)),
                  pl.BlockSpec(pw.shape, lambda i: (0, 0, 0)),
                  pl.BlockSpec(scale.shape, lambda i: (0, 0))],
        out_specs=pl.BlockSpec((gs * t, D_A), lambda i: (i, 0)),
        out_shape=jax.ShapeDtypeStruct((db * t, D_A), F32),
        compiler_params=_cparams(1),
        name="pool_sample",
    )(ext, pw, scale)


def _gate_out_ln_body(nparts, x_ref, *refs):
    parts = refs[:2 * nparts]
    w_ref, g_ref, b_ref, y_ref = refs[2 * nparts:]
    hs = [_silu_gate(parts[2 * p][...], parts[2 * p + 1][...]).astype(BF16) for p in range(nparts)]
    h = hs[0] if nparts == 1 else jnp.concatenate(hs, axis=1)
    z = ALPHA * x_ref[...] + _dot(h, w_ref[...])
    mu = jnp.mean(z, axis=1, keepdims=True)
    zc = z - mu
    var = jnp.mean(zc * zc, axis=1, keepdims=True)
    y_ref[...] = zc * lax.rsqrt(var + LN_EPS) * g_ref[...] + b_ref[...]


def _gate_out_ln(x2d, parts, w, ln_g, ln_b, name):
    m = x2d.shape[0]
    tm = min(TM_OUT, m)
    row = lambda a: pl.BlockSpec((tm, a.shape[1]), lambda i: (i, 0))
    full = lambda a: pl.BlockSpec(a.shape, lambda i: (0, 0))
    flat = [a for pair in parts for a in pair]
    return pl.pallas_call(
        functools.partial(_gate_out_ln_body, len(parts)),
        grid=(m // tm,),
        in_specs=[row(x2d)] + [row(a) for a in flat] + [full(w), full(ln_g), full(ln_b)],
        out_specs=row(x2d),
        out_shape=jax.ShapeDtypeStruct(x2d.shape, F32),
        compiler_params=_cparams(1),
        name=name,
    )(x2d, *flat, w, ln_g, ln_b)


def _kth_largest_key_keymajor(keys_ref, nc, k):
    def body(it, t):
        cand = t + jnp.left_shift(jnp.int32(1), jnp.int32(31) - it)
        hit = jnp.where(keys_ref[0:nc] >= cand[None], 1.0, 0.0)
        cnt = jnp.sum(jnp.sum(hit, axis=0), axis=0, keepdims=True)
        return jnp.where(cnt >= k, cand, t)
    return lax.fori_loop(0, 32, body, jnp.full((1, LANES), INT_MIN, I32))


def _select_topk_keymajor(keys_ref, nc, topk):
    t = _kth_largest_key_keymajor(keys_ref, nc, topk)
    r = lax.broadcasted_iota(I32, (2 * LANES, LANES), 0)
    c = lax.broadcasted_iota(I32, (2 * LANES, LANES), 1)
    sl_ones = jnp.where((c < r) | (r >= LANES), 1.0, 0.0).astype(BF16)
    gts, eqs = [], []
    cnt_gt = jnp.zeros((LANES, LANES), F32)
    for cc in range(nc):
        kc = keys_ref[cc]
        gts.append(kc > t)
        eqs.append(kc == t)
        cnt_gt = cnt_gt + jnp.where(gts[cc], 1.0, 0.0)
    need = topk - jnp.sum(cnt_gt, axis=0, keepdims=True)
    sels = []
    off = jnp.zeros((LANES, LANES), F32)
    for cc in range(nc):
        pt = _dot(sl_ones, jnp.where(eqs[cc], 1.0, 0.0).astype(BF16))
        sels.append(gts[cc] | (eqs[cc] & (pt[:LANES] + off < need)))
        off = off + pt[LANES:]
    return sels


def _dsa_prompt_tile(nc, topk, i, qt_ref, idxt_ref, kb_ref, vt_ref, kib_ref, o_ref,
                     keys_scr, madd_scr, bias_scr):
    w = nc * LANES
    row = lax.broadcasted_iota(I32, (LANES, LANES), 0)
    lane = lax.broadcasted_iota(I32, (LANES, LANES), 1)
    lo = row < DH_B
    idxt = idxt_ref[0]
    kib = kib_ref[0, 0:w, :]

    def head_pair_rhs(pair_t):
        zero = jnp.zeros_like(pair_t)
        return jnp.concatenate([jnp.where(lo, pair_t, zero), jnp.where(lo, zero, pair_t)], axis=1)

    score = None
    for j in range(H_IDX // 2):
        st = _dot(kib, head_pair_rhs(idxt[j * LANES:(j + 1) * LANES, :]).astype(BF16))
        for half in range(2):
            h = 2 * j + half
            wh = idxt[IDX_WI + h:IDX_WI + h + 1, :] * D_IDX ** -0.5
            term = wh * jnp.maximum(st[:, half * LANES:(half + 1) * LANES], 0.0)
            score = term if score is None else score + term
    qpos = i * LANES + lane
    valid = [(c * LANES + row) <= qpos for c in range(nc)]
    for c in range(nc):
        keys_scr[c] = _sortable_key(jnp.where(valid[c], score[c * LANES:(c + 1) * LANES, :], NEG_INF))
    sels = _select_topk_keymajor(keys_scr, nc, topk)
    for c in range(nc):
        madd_scr[c] = jnp.where(sels[c] & valid[c], 0.0, NEG_INF)

    for j in range(H_B // 2):
        logit2 = _dot(kb_ref[0, 0:w, j * LANES:(j + 1) * LANES],
                      head_pair_rhs(qt_ref[0, j * LANES:(j + 1) * LANES, :]))
        ps, dens = [], []
        for half in range(2):
            h = 2 * j + half
            pieces = []
            for c in range(nc):
                lc = logit2[c * LANES:(c + 1) * LANES, half * LANES:(half + 1) * LANES] + madd_scr[c]
                if c >= nc - DSA_CHUNKS_PER_VARIANT - 1:
                    lc = lc + bias_scr[h, jnp.clip(i - c, 0, 2)]
                pieces.append(lc)
            logit = jnp.concatenate(pieces, axis=0)
            p = jnp.exp(logit - jnp.max(logit, axis=0, keepdims=True))
            dens.append(jnp.sum(p, axis=0, keepdims=True))
            ps.append(p.astype(BF16))
        pv = _dot(vt_ref[0, j * LANES:(j + 1) * LANES, 0:w], jnp.concatenate(ps, axis=1))
        ot = jnp.where(lo, pv[:, :LANES] / dens[0], pv[:, LANES:] / dens[1])
        o_ref[0, :, j * LANES:(j + 1) * LANES] = ot.T


def _dsa_prompt_body(topk, relb_ref, qt_ref, idxt_ref, kb_ref, vt_ref, kib_ref, o_ref,
                     keys_scr, madd_scr, bias_scr):
    b = pl.program_id(0)
    i = pl.program_id(1)
    ncs = kb_ref.shape[1] // LANES

    @pl.when((b == 0) & (i == 0))
    def _():
        r = lax.broadcasted_iota(I32, (LANES, LANES), 0)
        c = lax.broadcasted_iota(I32, (LANES, LANES), 1)
        for h in range(H_B):
            far = relb_ref[N_BUCKETS - 1, h]
            bias_scr[h, 0] = _bias_of_dist(c - r, relb_ref, h) - far
            bias_scr[h, 1] = _bias_of_dist(LANES + c - r, relb_ref, h) - far
            bias_scr[h, 2] = jnp.zeros((LANES, LANES), F32)

    nvar = -(-ncs // DSA_CHUNKS_PER_VARIANT)
    for var in range(nvar):
        nc = min((var + 1) * DSA_CHUNKS_PER_VARIANT, ncs)

        @pl.when(i // DSA_CHUNKS_PER_VARIANT == var)
        def _(nc=nc):
            _dsa_prompt_tile(nc, topk, i, qt_ref, idxt_ref, kb_ref, vt_ref, kib_ref, o_ref,
                             keys_scr, madd_scr, bias_scr)


def _dsa_prompt(rel_bias, qt3, idxt3, kb3, vt3, kib3):
    b, s, _ = kb3.shape
    assert s % TQ == 0 and TQ == LANES
    topk = min(TOPK_MAX, s // 4)
    tspec = lambda w: pl.BlockSpec((1, w, TQ), lambda i, j: (i, 0, j))
    kspec = lambda w: pl.BlockSpec((1, s, w), lambda i, j: (i, 0, 0))
    ncs = s // LANES
    return pl.pallas_call(
        functools.partial(_dsa_prompt_body, topk),
        grid=(b, s // TQ),
        in_specs=[pl.BlockSpec(memory_space=pltpu.SMEM),
                  tspec(D_B), tspec(IDX_W), kspec(D_B), pl.BlockSpec((1, D_B, s), lambda i, j: (i, 0, 0)),
                  kspec(LANES)],
        out_specs=pl.BlockSpec((1, TQ, D_B), lambda i, j: (i, j, 0)),
        out_shape=jax.ShapeDtypeStruct((b, s, D_B), F32),
        scratch_shapes=[pltpu.VMEM((ncs, LANES, LANES), I32), pltpu.VMEM((ncs, LANES, LANES), F32),
                        pltpu.VMEM((H_B, 3, LANES, LANES), F32)],
        compiler_params=_cparams(2),
        name="dsa_prompt",
    )(rel_bias, qt3, idxt3, kb3, vt3, kib3)


class _PagedPrefetch:
    def __init__(self, pt_ref, caches, pages_per_step, steps_per_seq):
        self.pt_ref, self.caches = pt_ref, caches
        self.n, self.steps = pages_per_step, steps_per_seq

    def _copies(self, seq, group, slot):
        out = []
        for p in range(self.n):
            page = self.pt_ref[seq, group * self.n + p]
            for hbm, buf, sem in self.caches:
                out.append(pltpu.make_async_copy(hbm.at[0, page], buf.at[slot, p], sem.at[slot]))
        return out

    def advance(self, seq, group, n_seq):
        step = seq * self.steps + group
        slot = step & 1

        @pl.when(step == 0)
        def _():
            for n, cp in enumerate(self._copies(seq, group, slot)):
                cp.start(priority=n % 2)

        wrap = group + 1 == self.steps
        nseq = jnp.where(wrap, seq + 1, seq)
        ngroup = jnp.where(wrap, 0, group + 1)

        @pl.when(nseq < n_seq)
        def _():
            for n, cp in enumerate(self._copies(nseq, ngroup, 1 - slot)):
                cp.start(priority=n % 2)

        for cp in self._copies(seq, group, slot):
            cp.wait()
        return slot


def _dsa_score_body(n_pages, pt_ref, idx_ref, kidx_hbm, keys_ref, knew_ref, kbuf, ksem, kcat_scr):
    b = pl.program_id(0)
    slot = _PagedPrefetch(pt_ref, [(kidx_hbm, kbuf, ksem)], n_pages, 1).advance(b, 0, pl.num_programs(0))
    t = idx_ref.shape[1]
    idx = idx_ref[0]
    qrows = jnp.concatenate([idx[:, h * D_IDX:(h + 1) * D_IDX] for h in range(H_IDX)], axis=0).astype(BF16)

    def score_of(qk):
        sh = jnp.maximum(qk * D_IDX ** -0.5, 0.0)
        out = None
        for h in range(H_IDX):
            term = idx[:, IDX_WI + h:IDX_WI + h + 1] * sh[h * t:(h + 1) * t]
            out = term if out is None else out + term
        return out

    for g in range(n_pages):
        kcat_scr[:, g * PAGE_SIZE:(g + 1) * PAGE_SIZE] = kbuf[slot, g].astype(BF16)
    score = score_of(_dot(qrows, kcat_scr[...]))
    for g in range(n_pages):
        keys_ref[g] = _sortable_key(score[:, g * PAGE_SIZE:(g + 1) * PAGE_SIZE])

    knew = jnp.concatenate([idx[:, IDX_KI:IDX_KI + D_IDX], jnp.zeros((LANES - t, D_IDX), F32)], axis=0)
    sn = score_of(_dot_nt(qrows, knew.astype(BF16)))
    tq = lax.broadcasted_iota(I32, (t, LANES), 0)
    tk = lax.broadcasted_iota(I32, (t, LANES), 1)
    knew_ref[...] = _sortable_key(jnp.where(tk <= tq, sn, NEG_INF))


def _dsa_pick_body(topk, t, keys_ref, knew_ref, sel_ref, keys_scr):
    nc = keys_ref.shape[0]
    rows = keys_ref.shape[1]
    keys_scr[0:nc] = keys_ref[...]
    keys_scr[nc] = knew_ref[...]
    sel = _select_topk(keys_scr, nc + 1, rows, topk, 1)
    sel_ref[0:nc] = jnp.where(sel[0:nc], 1.0, 0.0)
    tq = lax.broadcasted_iota(I32, (rows, LANES), 0) & (t - 1)
    tk = lax.broadcasted_iota(I32, (rows, LANES), 1)
    sel_ref[nc] = jnp.where(sel[nc] & (tk <= tq), 1.0, 0.0)


def _dsa_select(page_table, idx3, cache_kidx):
    db, t, _ = idx3.shape
    n_pages = page_table.shape[1]
    assert t & (t - 1) == 0
    topk = min(TOPK_MAX, (n_pages * PAGE_SIZE + t) // 4)
    grid_spec = pltpu.PrefetchScalarGridSpec(
        num_scalar_prefetch=1,
        grid=(db,),
        in_specs=[pl.BlockSpec((1, t, IDX_W), lambda b, pt: (b, 0, 0)), pl.BlockSpec(memory_space=pl.ANY)],
        out_specs=[pl.BlockSpec((n_pages, t, LANES), lambda b, pt: (0, b, 0)),
                   pl.BlockSpec((t, LANES), lambda b, pt: (b, 0))],
        scratch_shapes=[pltpu.VMEM((2, n_pages, D_IDX, PAGE_SIZE), F32), pltpu.SemaphoreType.DMA((2,)),
                        pltpu.VMEM((D_IDX, n_pages * PAGE_SIZE), BF16)],
    )
    keys, knew = pl.pallas_call(
        functools.partial(_dsa_score_body, n_pages),
        grid_spec=grid_spec,
        out_shape=[jax.ShapeDtypeStruct((n_pages, db * t, LANES), I32),
                   jax.ShapeDtypeStruct((db * t, LANES), I32)],
        compiler_params=_cparams(1),
        name="dsa_score",
    )(page_table, idx3, cache_kidx)
    rows = min(R_PICK, db * t)
    return pl.pallas_call(
        functools.partial(_dsa_pick_body, topk, t),
        grid=(db * t // rows,),
        in_specs=[pl.BlockSpec((n_pages, rows, LANES), lambda r: (0, r, 0)),
                  pl.BlockSpec((rows, LANES), lambda r: (r, 0))],
        out_specs=pl.BlockSpec((n_pages + 1, rows, LANES), lambda r: (0, r, 0)),
        out_shape=jax.ShapeDtypeStruct((n_pages + 1, db * t, LANES), F32),
        scratch_shapes=[pltpu.VMEM((n_pages + 1, rows, LANES), I32)],
        compiler_params=_cparams(1),
        name="dsa_pick",
    )(keys, knew)


def _dsa_sample_body(g_pages, pt_ref, relb_ref, qb_ref, sel_ref, knew_ref, vnew_ref, k_hbm, v_hbm, o_ref,
                     kbuf, vbuf, ksem, vsem, qbd_scr, kcat_scr, vcat_scr, bias_scr, m_scr, l_scr, acc_scr):
    b = pl.program_id(0)
    c = pl.program_id(1)
    nsteps = pl.num_programs(1)
    t = qb_ref.shape[1]
    slot = _PagedPrefetch(pt_ref, [(k_hbm, kbuf, ksem), (v_hbm, vbuf, vsem)], g_pages, nsteps).advance(
        b, c, pl.num_programs(0))

    @pl.when((b == 0) & (c == 0))
    def _():
        tq = lax.broadcasted_iota(I32, (t, LANES), 0)
        col = lax.broadcasted_iota(I32, (t, LANES), 1)
        for h in range(H_B):
            rs = slice(h * t, (h + 1) * t)
            bias_scr[0, rs, :] = jnp.full((t, LANES), relb_ref[N_BUCKETS - 1, h], F32)
            bias_scr[1, rs, :] = _bias_of_dist(tq + PAGE_SIZE - col, relb_ref, h)
            bias_scr[2, rs, :] = _bias_of_dist(tq - col, relb_ref, h)

    @pl.when(c == 0)
    def _():
        q = qb_ref[0] * DH_B ** -0.5
        lane = lax.broadcasted_iota(I32, (t, D_B), 1)
        qbd = [jnp.where((lane >= h * DH_B) & (lane < (h + 1) * DH_B), q, 0.0) for h in range(H_B)]
        qbd_scr[...] = jnp.concatenate(qbd, axis=0).astype(BF16)
        m_scr[...] = jnp.full(m_scr.shape, NEG_INF, F32)
        l_scr[...] = jnp.zeros(l_scr.shape, F32)
        acc_scr[...] = jnp.zeros(acc_scr.shape, F32)

    def update(logit, sel, vmat, v_key_minor):
        selr = jnp.concatenate([sel] * H_B, axis=0) > 0.5
        lm = jnp.where(selr, logit, NEG_INF)
        m_old = m_scr[...]
        m_new = jnp.maximum(m_old, jnp.max(lm, axis=1, keepdims=True))
        m_safe = jnp.where(m_new == NEG_INF, 0.0, m_new)
        alpha = jnp.exp(m_old - m_safe)
        p = jnp.exp(lm - m_safe)
        l_scr[...] = alpha * l_scr[...] + jnp.sum(p, axis=1, keepdims=True)
        pv = _dot_nt(p.astype(BF16), vmat) if v_key_minor else _dot(p.astype(BF16), vmat)
        acc_scr[...] = alpha * acc_scr[...] + pv
        m_scr[...] = m_new

    for g in range(g_pages):
        kcat_scr[:, g * PAGE_SIZE:(g + 1) * PAGE_SIZE] = kbuf[slot, g].astype(BF16)
        vcat_scr[:, g * PAGE_SIZE:(g + 1) * PAGE_SIZE] = vbuf[slot, g].astype(BF16)
    logit = _dot(qbd_scr[...], kcat_scr[...])
    far = bias_scr[0]
    last = jnp.where(c == nsteps - 1, bias_scr[1], far)
    bias = jnp.concatenate([far] * (g_pages - 1) + [last], axis=1)
    sel = jnp.concatenate([sel_ref[c * g_pages + g] for g in range(g_pages)], axis=1)
    update(logit + bias, sel, vcat_scr[...], True)

    @pl.when(c == nsteps - 1)
    def _():
        pad = jnp.zeros((LANES - t, D_B), F32)
        knp = jnp.concatenate([knew_ref[0], pad], axis=0).astype(BF16)
        vnp = jnp.concatenate([vnew_ref[0], pad], axis=0).astype(BF16)
        ln = _dot_nt(qbd_scr[...], knp) + bias_scr[2]
        update(ln, sel_ref[sel_ref.shape[0] - 1], vnp, False)
        out = acc_scr[...] / l_scr[...]
        lane = lax.broadcasted_iota(I32, (t, D_B), 1)
        o = jnp.zeros((t, D_B), F32)
        for h in range(H_B):
            o = jnp.where((lane >= h * DH_B) & (lane < (h + 1) * DH_B), out[h * t:(h + 1) * t], o)
        o_ref[0] = o


def _dsa_sample(page_table, rel_bias, qb3, sel, k3, v3, cache_k, cache_v):
    db, t, _ = qb3.shape
    n_pages = page_table.shape[1]
    g = min(G_DSA, n_pages)
    assert n_pages % g == 0
    rows = H_B * t
    seq = lambda w: pl.BlockSpec((1, t, w), lambda b, c, pt: (b, 0, 0))
    sel_spec = pl.BlockSpec((sel.shape[0], t, LANES), lambda b, c, pt: (0, b, 0))
    grid_spec = pltpu.PrefetchScalarGridSpec(
        num_scalar_prefetch=1,
        grid=(db, n_pages // g),
        in_specs=[pl.BlockSpec(memory_space=pltpu.SMEM), seq(D_B), sel_spec, seq(D_B), seq(D_B),
                  pl.BlockSpec(memory_space=pl.ANY), pl.BlockSpec(memory_space=pl.ANY)],
        out_specs=seq(D_B),
        scratch_shapes=[pltpu.VMEM((2, g, D_B, PAGE_SIZE), F32), pltpu.VMEM((2, g, D_B, PAGE_SIZE), F32),
                        pltpu.SemaphoreType.DMA((2,)), pltpu.SemaphoreType.DMA((2,)),
                        pltpu.VMEM((rows, D_B), BF16),
                        pltpu.VMEM((D_B, g * PAGE_SIZE), BF16), pltpu.VMEM((D_B, g * PAGE_SIZE), BF16),
                        pltpu.VMEM((3, rows, LANES), F32),
                        pltpu.VMEM((rows, 1), F32), pltpu.VMEM((rows, 1), F32), pltpu.VMEM((rows, D_B), F32)],
    )
    return pl.pallas_call(
        functools.partial(_dsa_sample_body, g),
        grid_spec=grid_spec,
        out_shape=jax.ShapeDtypeStruct((db, t, D_B), F32),
        compiler_params=_cparams(2),
        name="dsa_sample",
    )(page_table, rel_bias, qb3, sel, k3, v3, cache_k, cache_v)


def _ones_beyond_rope(kpe_slab):
    lane = lax.broadcasted_iota(I32, kpe_slab.shape, 1)
    return jnp.where(lane < ROPE_C, kpe_slab, 1.0)


def _rms(x, g):
    return x * lax.rsqrt(jnp.mean(x * x, axis=1, keepdims=True) + RMS_EPS) * g


def _odd_proj_body(x_ref, wa_ref, wb_ref, wbr_ref, wg_ref, wq_ref, wqr_ref, qn_ref, kvn_ref, cos_ref, sin_ref,
                   qcat_ref, kvpe_ref, kvpeb_ref, g_ref):
    xb = x_ref[...].astype(BF16)
    cos_k, sin_k = cos_ref[:, :LANES], sin_ref[:, :LANES]
    cos_q, sin_q = cos_ref[:, LANES:], sin_ref[:, LANES:]
    g_ref[...] = _dot(xb, wg_ref[...])
    kv = _dot(xb, wb_ref[...])
    kvr = _dot(xb, wbr_ref[...])
    ckvn = _rms(kv[:, :KV_LORA], kvn_ref[...])
    kpe = kv[:, KV_LORA:] * cos_k + kvr[:, KV_LORA:] * sin_k
    kvpe = jnp.concatenate([ckvn, kpe], axis=1)
    kvpe_ref[...] = kvpe
    kvpeb_ref[...] = kvpe.astype(BF16)
    cqn = _rms(_dot(xb, wa_ref[...]), qn_ref[...]).astype(BF16)
    qc = _dot(cqn, wq_ref[...])
    qr = _dot(cqn, wqr_ref[...])
    for h in range(H_C):
        sl = slice(h * LANES, (h + 1) * LANES)
        qcat_ref[:, sl] = (qc[:, sl] * cos_q + qr[:, sl] * sin_q).astype(qcat_ref.dtype)


def _odd_proj(x2d, wa, wb, wbr, wg, wq, wqr, qn, kvn, cos_t, sin_t, qcat_dtype):
    m = x2d.shape[0]
    tm = min(TM_PROJ, m)
    nt = cos_t.shape[0] // tm
    row = lambda w: pl.BlockSpec((tm, w), lambda i: (i, 0))
    full = lambda a: pl.BlockSpec(a.shape, lambda i: (0, 0))
    tab = pl.BlockSpec((tm, 2 * LANES), lambda i: (i % nt, 0))
    f = lambda w, dt: jax.ShapeDtypeStruct((m, w), dt)
    return pl.pallas_call(
        _odd_proj_body,
        grid=(m // tm,),
        in_specs=[row(D_MODEL)] + [full(a) for a in (wa, wb, wbr, wg, wq, wqr, qn, kvn)] + [tab, tab],
        out_specs=[row(QCAT_W), row(KVPE_W), row(KVPE_W), row(D_C)],
        out_shape=[f(QCAT_W, qcat_dtype), f(KVPE_W, F32), f(KVPE_W, BF16), f(D_C, F32)],
        compiler_params=_cparams(1),
        name="odd_proj",
    )(x2d, wa, wb, wbr, wg, wq, wqr, qn, kvn, cos_t, sin_t)


def _mla_expand_q(qcat, wexp_ref):
    rows = [_dot(qcat[:, h * LANES:(h + 1) * LANES].astype(BF16), wexp_ref[h]) for h in range(H_C)]
    return jnp.concatenate(rows, axis=0).astype(BF16)


def _mla_softmax_step(s, m_old, vmat):
    m_new = jnp.maximum(m_old, jnp.max(s, axis=1, keepdims=True))
    alpha = jnp.exp2((m_old - m_new) * MLA_EXP2_SCALE)
    m_wide = jnp.concatenate([m_new] * (s.shape[1] // LANES), axis=1)
    p = jnp.exp2((s - m_wide) * MLA_EXP2_SCALE)
    pv = _dot(p.astype(BF16), vmat)
    return m_new, jnp.concatenate([alpha] * (KVPE_W // LANES), axis=1), pv


def _mla_head_pair_out(acc_a, acc_b, wuv_pair):
    lat = [a[:, :KV_LORA] / a[:, MLA_DEN_LANE:MLA_DEN_LANE + 1] for a in (acc_a, acc_b)]
    return _dot(jnp.concatenate(lat, axis=1).astype(BF16), wuv_pair)


def _mla_prompt_body(qcat_ref, kv_ref, wk_ref, wv_ref, o_ref, kx_scr, vx_scr, m_scr, acc_scr, mask_scr):
    i = pl.program_id(1)
    tq = qcat_ref.shape[1]
    s = kv_ref.shape[1]

    @pl.when(i == 0)
    def _():
        lane = lax.broadcasted_iota(I32, (tq, QCAT_W), 1)
        ones = (((lane >> 7) ^ (lane >> 6)) & 1) == 1
        for r in range(s // tq):
            rs = slice(r * tq, (r + 1) * tq)
            kv = kv_ref[0, rs, :]
            kx_scr[rs, :] = _dot(kv, wk_ref[...]).astype(BF16)
            vx_scr[rs, :] = jnp.where(ones, 1.0, _dot(kv[:, :KV_LORA], wv_ref[...])).astype(BF16)

    @pl.when((pl.program_id(0) == 0) & (i == 0))
    def _():
        row = lax.broadcasted_iota(I32, (tq, tq), 0)
        col = lax.broadcasted_iota(I32, (tq, tq), 1)
        mask_scr[0] = jnp.zeros((tq, tq), F32)
        mask_scr[1] = jnp.where(col <= row, 0.0, NEG_INF)

    m_scr[...] = jnp.full(m_scr.shape, NEG_INF, F32)
    acc_scr[...] = jnp.zeros(acc_scr.shape, F32)

    def chunk(c, carry):
        k0 = pl.multiple_of(c * tq, tq)
        mask = mask_scr[(c == i).astype(I32)]
        for h in range(H_C):
            hs = slice(h * LANES, (h + 1) * LANES)
            rs = slice(h * tq, (h + 1) * tq)
            sc = _dot_nt(qcat_ref[0, :, hs], kx_scr[pl.ds(k0, tq), hs]) + mask
            m_old = m_scr[rs, :]
            m_new = jnp.maximum(m_old, jnp.max(sc, axis=1, keepdims=True))
            alpha = jnp.exp2((m_old - m_new) * MLA_EXP2_SCALE)
            p = jnp.exp2((sc - jnp.concatenate([m_new] * (tq // LANES), axis=1)) * MLA_EXP2_SCALE)
            acc_scr[rs, :] = alpha * acc_scr[rs, :] + _dot(p.astype(BF16), vx_scr[pl.ds(k0, tq), hs])
            m_scr[rs, :] = m_new
        return carry

    lax.fori_loop(0, i + 1, chunk, 0)
    lo = lax.broadcasted_iota(I32, (tq, LANES), 1) < DV_C
    for j in range(H_C // 2):
        ae = acc_scr[2 * j * tq:(2 * j + 1) * tq, :]
        ao = acc_scr[(2 * j + 1) * tq:(2 * j + 2) * tq, :]
        o_ref[0, :, j * LANES:(j + 1) * LANES] = jnp.where(
            lo, ae / pltpu.roll(ae, DV_C, 1), ao / pltpu.roll(ao, DV_C, 1))


def _mla_prompt(qcat3, kvpeb3, wk2, wv2):
    b, s, _ = qcat3.shape
    tq = min(TQ_MLA, s)
    assert s % tq == 0 and tq % LANES == 0
    rows = H_C * tq
    return pl.pallas_call(
        _mla_prompt_body,
        grid=(b, s // tq),
        in_specs=[pl.BlockSpec((1, tq, QCAT_W), lambda i, j: (i, j, 0)),
                  pl.BlockSpec((1, s, KVPE_W), lambda i, j: (i, 0, 0)),
                  pl.BlockSpec(wk2.shape, lambda i, j: (0, 0)),
                  pl.BlockSpec(wv2.shape, lambda i, j: (0, 0))],
        out_specs=pl.BlockSpec((1, tq, D_C), lambda i, j: (i, j, 0)),
        out_shape=jax.ShapeDtypeStruct((b, s, D_C), F32),
        scratch_shapes=[pltpu.VMEM((s, QCAT_W), BF16), pltpu.VMEM((s, QCAT_W), BF16),
                        pltpu.VMEM((rows, LANES), F32), pltpu.VMEM((rows, LANES), F32),
                        pltpu.VMEM((2, tq, tq), F32)],
        compiler_params=_cparams(2),
        name="mla_prompt",
    )(qcat3, kvpeb3, wk2, wv2)


def _mla_sample_body(g_pages, pt_ref, qcat_ref, knew_ref, wexp_ref, wuv_ref, ckv_hbm, kpe_hbm, o_ref,
                     cbuf, pbuf, csem, psem, qall_scr, ccat_scr, pcat_scr, m_scr, acc_scr):
    b = pl.program_id(0)
    c = pl.program_id(1)
    nsteps = pl.num_programs(1)
    t = qcat_ref.shape[1]
    slot = _PagedPrefetch(pt_ref, [(ckv_hbm, cbuf, csem), (kpe_hbm, pbuf, psem)], g_pages, nsteps).advance(
        b, c, pl.num_programs(0))

    @pl.when((b == 0) & (c == 0))
    def _():
        pcat_scr[...] = jnp.zeros(pcat_scr.shape, BF16)
        ccat_scr[:, KV_LORA:] = jnp.ones((ccat_scr.shape[0], KVPE_W - KV_LORA), BF16)

    @pl.when(c == 0)
    def _():
        qall_scr[...] = _mla_expand_q(qcat_ref[0], wexp_ref)
        m_scr[...] = jnp.full(m_scr.shape, NEG_INF, F32)
        acc_scr[...] = jnp.zeros(acc_scr.shape, F32)

    def update(s, vmat):
        m_new, alpha, pv = _mla_softmax_step(s, m_scr[...], vmat)
        acc_scr[...] = alpha * acc_scr[...] + pv
        m_scr[...] = m_new

    q = qall_scr[...]
    sub = min(MLA_SUB_PAGES, g_pages)
    m_run = m_scr[...]
    acc_run = acc_scr[...]
    nblk = g_pages // sub

    def scores(blk):
        rs = slice(blk * sub * PAGE_SIZE, (blk + 1) * sub * PAGE_SIZE)
        for g in range(blk * sub, (blk + 1) * sub):
            ccat_scr[g * PAGE_SIZE:(g + 1) * PAGE_SIZE, 0:KV_LORA] = cbuf[slot, g].astype(BF16)
            pcat_scr[0:ROPE_C, g * PAGE_SIZE:(g + 1) * PAGE_SIZE] = pbuf[slot, g].astype(BF16)
        return _dot_nt(q[:, :KV_LORA], ccat_scr[rs, 0:KV_LORA]) + _dot(q[:, KV_LORA:], pcat_scr[:, rs])

    s_next = scores(0)
    for blk in range(nblk):
        s = s_next
        if blk + 1 < nblk:
            s_next = scores(blk + 1)
        rs = slice(blk * sub * PAGE_SIZE, (blk + 1) * sub * PAGE_SIZE)
        m_run, alpha, pv = _mla_softmax_step(s, m_run, ccat_scr[rs, :])
        acc_run = alpha * acc_run + pv
    m_scr[...] = m_run
    acc_scr[...] = acc_run

    @pl.when(c == nsteps - 1)
    def _():
        kn = knew_ref[0]
        kn = jnp.concatenate([kn[:, :KV_LORA], _ones_beyond_rope(kn[:, KV_LORA:])], axis=1)
        knp = jnp.concatenate([kn, jnp.zeros((LANES - t, KVPE_W), F32)], axis=0).astype(BF16)
        sn = _dot_nt(q, knp)
        r = lax.broadcasted_iota(I32, (H_C * t, LANES), 0)
        col = lax.broadcasted_iota(I32, (H_C * t, LANES), 1)
        sn = jnp.where(col <= (r & (t - 1)), sn, NEG_INF)
        update(sn, knp)
        acc = acc_scr[...]
        outs = [_mla_head_pair_out(acc[2 * j * t:(2 * j + 1) * t], acc[(2 * j + 1) * t:(2 * j + 2) * t], wuv_ref[j])
                for j in range(H_C // 2)]
        o_ref[0] = jnp.concatenate(outs, axis=1)


def _mla_sample(page_table, qcat3, kvpeb3, wexp, wuvp, cache_ckv, cache_kpe):
    db, t, _ = qcat3.shape
    n_pages = page_table.shape[1]
    g = min(G_MLA, n_pages)
    assert n_pages % g == 0
    rows = H_C * t
    grid_spec = pltpu.PrefetchScalarGridSpec(
        num_scalar_prefetch=1,
        grid=(db, n_pages // g),
        in_specs=[pl.BlockSpec((1, t, QCAT_W), lambda b, c, pt: (b, 0, 0)),
                  pl.BlockSpec((1, t, KVPE_W), lambda b, c, pt: (b, 0, 0)),
                  pl.BlockSpec(wexp.shape, lambda b, c, pt: (0, 0, 0)),
                  pl.BlockSpec(wuvp.shape, lambda b, c, pt: (0, 0, 0)),
                  pl.BlockSpec(memory_space=pl.ANY), pl.BlockSpec(memory_space=pl.ANY)],
        out_specs=pl.BlockSpec((1, t, D_C), lambda b, c, pt: (b, 0, 0)),
        scratch_shapes=[pltpu.VMEM((2, g, PAGE_SIZE, KV_LORA), F32), pltpu.VMEM((2, g, ROPE_C, PAGE_SIZE), F32),
                        pltpu.SemaphoreType.DMA((2,)), pltpu.SemaphoreType.DMA((2,)),
                        pltpu.VMEM((rows, KVPE_W), BF16),
                        pltpu.VMEM((g * PAGE_SIZE, KVPE_W), BF16), pltpu.VMEM((LANES, g * PAGE_SIZE), BF16),
                        pltpu.VMEM((rows, LANES), F32), pltpu.VMEM((rows, KVPE_W), F32)],
    )
    return pl.pallas_call(
        functools.partial(_mla_sample_body, g),
        grid_spec=grid_spec,
        out_shape=jax.ShapeDtypeStruct((db, t, D_C), F32),
        compiler_params=_cparams(2),
        name="mla_sample",
    )(page_table, qcat3, kvpeb3, wexp, wuvp, cache_ckv, cache_kpe)


def _rope_cos_sin(pos):
    inv = ROPE_THETA ** (-jnp.arange(ROPE_HALF, dtype=F32) / ROPE_HALF)
    ang = pos.astype(F32)[:, None] * inv
    return jnp.cos(ang), jnp.sin(ang)


def _rope_lane_pattern(width, period, rot_start, limit):
    lane = np.arange(width)
    d = lane % period - rot_start
    inside = lane < limit
    x1 = inside & (d >= 0) & (d < ROPE_HALF)
    x2 = inside & (d >= ROPE_HALF) & (d < 2 * ROPE_HALF)
    return x1, x2


def _rope_tables(cos, sin, x1, x2, base):
    width = x1.shape[0]
    reps = width // ROPE_HALF
    cos_w = jnp.tile(cos, (1, reps))
    sin_w = jnp.tile(sin, (1, reps))
    cos_t = jnp.where(x1 | x2, cos_w, jnp.asarray(base, F32)[None, :])
    sin_t = jnp.where(x1, -sin_w, jnp.where(x2, sin_w, 0.0))
    return cos_t, sin_t


def _partner_columns(w, x1, x2):
    lane = np.arange(w.shape[1])
    src = lane + ROPE_HALF * x1 - ROPE_HALF * x2
    return jnp.where((x1 | x2)[None, :], w[:, src], 0.0)


def _tile_rows(tab, reps):
    return jnp.tile(tab, (reps, 1))


def kernel(x_prompt, x_sample, state_pool, cache_k_b, cache_v_b, cache_kidx_b, cache_ckv, cache_kpe, page_table, w_in_even, pool_w, pool_scale, w_out_even, rel_bias, w_in_odd, q_norm, w_q_b, kv_norm, w_uk, w_uv, w_out_odd, ln_g, ln_b):
    bsz, seq, _ = x_prompt.shape
    db, t, _ = x_sample.shape
    n_pages = page_table.shape[1]
    past = n_pages * PAGE_SIZE
    mp, ms = bsz * seq, db * t
    assert w_in_even.shape[0] == 1 and w_in_odd.shape[0] == 1 and t == 8

    cos_p, sin_p = _rope_cos_sin(jnp.arange(seq))
    cos_s, sin_s = _rope_cos_sin(past + jnp.arange(t))

    we = w_in_even[0]
    n_main = 6 * D_A
    wm = we[:, :n_main].astype(BF16)
    w_qi = we[:, n_main:n_main + H_IDX * D_IDX]
    w_ki = we[:, n_main + H_IDX * D_IDX:n_main + H_IDX * D_IDX + D_IDX]
    w_wi = we[:, n_main + H_IDX * D_IDX + D_IDX:]
    ws32 = jnp.concatenate([w_qi, w_ki, w_ki, w_wi, jnp.zeros((D_MODEL, IDX_W - IDX_WI - H_IDX), F32)], axis=1)
    x1e, x2e = _rope_lane_pattern(IDX_W, D_IDX, 0, IDX_WI)
    base_e = np.where(np.arange(IDX_W) < IDX_WI, 1.0, np.where(np.arange(IDX_W) < IDX_WI + H_IDX, H_IDX ** -0.5, 0.0))
    ws = ws32.astype(BF16)
    wr = _partner_columns(ws32, x1e, x2e).astype(BF16)
    pw = pool_w[0].astype(BF16)
    pscale = pool_scale[0][None, :]
    woe = w_out_even[0].astype(BF16)
    lng0, lnb0 = ln_g[0][None, :], ln_b[0][None, :]
    lng1, lnb1 = ln_g[1][None, :], ln_b[1][None, :]

    def even_tables(cos, sin, reps):
        ct, st = _rope_tables(cos, sin, x1e, x2e, base_e)
        return _tile_rows(ct, reps), _tile_rows(st, reps)

    xp2 = x_prompt.reshape(mp, D_MODEL)
    ct, st = even_tables(cos_p, sin_p, 1)
    u, ga, gb, k_fm, v_fm, idx_fm, qt, vt, kb, kib = _even_proj(xp2, wm, ws, wr, ct, st, seq_len=seq)
    r3 = lambda a: a.reshape(bsz, seq, a.shape[-1])
    a_p = _pool_prompt(r3(u), pw, pscale).reshape(mp, D_A)
    o_p = _dsa_prompt(rel_bias, qt, idx_fm, r3(kb), vt, r3(kib)).reshape(mp, D_B)
    xp1 = _gate_out_ln(xp2, [(a_p, ga), (o_p, gb)], woe, lng0, lnb0, "even_out_prompt")
    pool_p = r3(u)[:, seq - POOL_STATE:][None]
    kb_p = jnp.transpose(k_fm.reshape(bsz, H_B, DH_B, seq), (0, 3, 1, 2))[None]
    vb_p = jnp.transpose(v_fm.reshape(bsz, H_B, DH_B, seq), (0, 3, 1, 2))[None]
    ki_p = jnp.transpose(idx_fm[:, IDX_KI:IDX_KI + D_IDX, :], (0, 2, 1))[None]

    xs2 = x_sample.reshape(ms, D_MODEL)
    tm_s = min(TM_PROJ, ms)
    ct, st = even_tables(cos_s, sin_s, tm_s // t)
    u, ga, gb, q, k, v, idx = _even_proj(xs2, wm, ws, wr, ct, st)
    s3 = lambda a: a.reshape(db, t, a.shape[-1])
    u_s = s3(u)
    ext = jnp.concatenate([jnp.zeros((db, 16 - POOL_STATE, D_A), F32), state_pool[0], u_s], axis=1)
    a_s = _pool_sample(ext, pw, pscale)
    n_phys = cache_k_b.shape[1]
    kidx_t = jnp.transpose(cache_kidx_b, (0, 1, 3, 2))
    ck4 = jnp.transpose(cache_k_b, (0, 1, 3, 4, 2)).reshape(1, n_phys, D_B, PAGE_SIZE)
    cv4 = jnp.transpose(cache_v_b, (0, 1, 3, 4, 2)).reshape(1, n_phys, D_B, PAGE_SIZE)
    kpe_t = jnp.transpose(cache_kpe, (0, 1, 3, 2))
    sel = _dsa_select(page_table, s3(idx), kidx_t)
    o_s = _dsa_sample(page_table, rel_bias, s3(q), sel, s3(k), s3(v), ck4, cv4).reshape(ms, D_B)
    xs1 = _gate_out_ln(xs2, [(a_s, ga), (o_s, gb)], woe, lng0, lnb0, "even_out_sample")
    pool_s = ext[:, ext.shape[1] - POOL_STATE:][None]
    kb_s = k.reshape(1, db, t, H_B, DH_B)
    vb_s = v.reshape(1, db, t, H_B, DH_B)
    ki_s = s3(idx)[:, :, IDX_KI:IDX_KI + D_IDX][None]

    wo = w_in_odd[0]
    wa = wo[:, :Q_LORA].astype(BF16)
    wb32 = jnp.concatenate([wo[:, Q_LORA:Q_LORA + KV_LORA + ROPE_C],
                            jnp.zeros((D_MODEL, KVPE_W - KV_LORA - ROPE_C), F32)], axis=1)
    x1k, x2k = _rope_lane_pattern(KVPE_W, KVPE_W, KV_LORA, KVPE_W)
    wb = wb32.astype(BF16)
    wbr = _partner_columns(wb32, x1k, x2k).astype(BF16)
    wg = wo[:, Q_LORA + KV_LORA + ROPE_C:].astype(BF16)
    wqb = w_q_b[0]
    wq32 = jnp.concatenate([wqb, jnp.zeros((Q_LORA, H_C, LANES - NOPE_C - ROPE_C), F32)], axis=2)
    wq32 = wq32.reshape(Q_LORA, QCAT_W)
    x1q, x2q = _rope_lane_pattern(QCAT_W, LANES, NOPE_C, QCAT_W)
    wq = wq32.astype(BF16)
    wqr = _partner_columns(wq32, x1q, x2q).astype(BF16)
    qn = q_norm[0][None, :]
    kvn = kv_norm[0][None, :]
    base_k = np.zeros(LANES)
    base_q = np.where(np.arange(LANES) < NOPE_C, 1.0, 0.0)

    def odd_tables(cos, sin, reps):
        ck, sk = _rope_tables(cos, sin, x1k[KV_LORA:], x2k[KV_LORA:], base_k)
        cq, sq = _rope_tables(cos, sin, x1q[:LANES], x2q[:LANES], base_q)
        return (_tile_rows(jnp.concatenate([ck, cq], axis=1), reps),
                _tile_rows(jnp.concatenate([sk, sq], axis=1), reps))

    wuk = w_uk[0]
    wexp = jnp.zeros((H_C, LANES, KVPE_W), F32)
    wexp = wexp.at[:, :NOPE_C, :KV_LORA].set(jnp.transpose(wuk, (1, 2, 0)))
    wexp = wexp.at[:, NOPE_C:NOPE_C + ROPE_C, KV_LORA:KV_LORA + ROPE_C].set(jnp.eye(ROPE_C, dtype=F32)[None])
    wexp = wexp.astype(BF16)
    wuv = jnp.transpose(w_uv[0], (1, 0, 2))
    wuvp = jnp.zeros((H_C // 2, 2 * KV_LORA, 2 * DV_C), F32)
    wuvp = wuvp.at[:, :KV_LORA, :DV_C].set(wuv[0::2])
    wuvp = wuvp.at[:, KV_LORA:, DV_C:].set(wuv[1::2])
    wuvp = wuvp.astype(BF16)
    wk2 = jnp.zeros((KVPE_W, H_C, LANES), F32)
    wk2 = wk2.at[:KV_LORA, :, :NOPE_C].set(wuk)
    wk2 = wk2.at[KV_LORA:KV_LORA + ROPE_C, :, NOPE_C:NOPE_C + ROPE_C].set(
        jnp.broadcast_to(jnp.eye(ROPE_C, dtype=F32)[:, None, :], (ROPE_C, H_C, ROPE_C)))
    wk2 = wk2.reshape(KVPE_W, QCAT_W).astype(BF16)
    wv2 = jnp.zeros((KV_LORA, H_C // 2, 2, LANES), F32)
    wv2 = wv2.at[:, :, 0, :DV_C].set(w_uv[0][:, 0::2])
    wv2 = wv2.at[:, :, 1, DV_C:].set(w_uv[0][:, 1::2])
    wv2 = wv2.reshape(KV_LORA, QCAT_W).astype(BF16)
    woo = w_out_odd[0].astype(BF16)

    ct, st = odd_tables(cos_p, sin_p, 1)
    qcat, kvpe, kvpeb, g1 = _odd_proj(xp1, wa, wb, wbr, wg, wq, wqr, qn, kvn, ct, st, BF16)
    o1 = _mla_prompt(r3(qcat), r3(kvpeb), wk2, wv2).reshape(mp, D_C)
    y_p = _gate_out_ln(xp1, [(o1, g1)], woo, lng1, lnb1, "odd_out_prompt").reshape(bsz, seq, D_MODEL)
    ckv_p = r3(kvpe)[:, :, :KV_LORA][None]
    kpe_p = r3(kvpe)[:, :, KV_LORA:KV_LORA + ROPE_C][None]

    ct, st = odd_tables(cos_s, sin_s, tm_s // t)
    qcat, kvpe, kvpeb, g1 = _odd_proj(xs1, wa, wb, wbr, wg, wq, wqr, qn, kvn, ct, st, F32)
    o1 = _mla_sample(page_table, s3(qcat), s3(kvpe), wexp, wuvp, cache_ckv, kpe_t).reshape(ms, D_C)
    y_s = _gate_out_ln(xs1, [(o1, g1)], woo, lng1, lnb1, "odd_out_sample").reshape(db, t, D_MODEL)
    ckv_s = s3(kvpe)[:, :, :KV_LORA][None]
    kpe_s = s3(kvpe)[:, :, KV_LORA:KV_LORA + ROPE_C][None]

    return (y_p, y_s, pool_p, pool_s, kb_p, kb_s, vb_p, vb_s, ki_p, ki_s, ckv_p, ckv_s, kpe_p, kpe_s)
```
